```python
import math
import jax
import jax.numpy as jnp
from jax import lax
import numpy as np

D_MODEL = 2048
BATCH = 32
SEQ = 256
DEPTH = 4
DEC_BATCH = 2
DEC_SEQ = 4096
PAST_LEN = 512

GRID_W = 64
N_MIXERS = 3
N_A_LAYERS = (DEPTH + 2) // 3
N_B_LAYERS = (DEPTH + 1) // 3
N_C_LAYERS = DEPTH // 3
EPS = 1e-6
Q_BLOCK = 128
ROPE_THETA = 10000.0

A_DK = 128
A_HEADS = D_MODEL // A_DK
A_DV = D_MODEL // A_HEADS
A_WIDTH = A_HEADS * A_DV
A_CHUNK = 16

B_DK = 128
B_DV = 2 * B_DK
B_HEADS = D_MODEL // B_DV
B_WIDTH = B_HEADS * B_DV

C_DH = 128
C_HEADS = D_MODEL // C_DH
C_KV_HEADS = C_HEADS // 4
C_GROUP = C_HEADS // C_KV_HEADS
C_WIDTH = C_HEADS * C_DH
C_KV_WIDTH = C_KV_HEADS * C_DH

kernel_name = "hybrid_diffusion_hgrn2_diffattn_gqa_step"


def rms_norm(x, w):
    xf = x.astype(jnp.float32)
    xf = xf * lax.rsqrt(jnp.mean(xf * xf, axis=-1, keepdims=True) + EPS)
    return (xf * w.astype(jnp.float32)).astype(x.dtype)


def adaln(x, norm_w, mod_w, mod_b, cond):
    m = jnp.dot(jax.nn.silu(cond), mod_w) + mod_b
    shift, scale, gate = jnp.split(m[:, None, :], 3, axis=-1)
    return rms_norm(x, norm_w) * (1 + scale) + shift, gate


def axial_rope_tables(n, dim):
    rows = n // GRID_W
    row_ids = jnp.repeat(jnp.arange(rows), GRID_W)
    col_ids = jnp.tile(jnp.arange(GRID_W), rows)
    quarter = dim // 4
    inv_freq = ROPE_THETA ** (-jnp.arange(quarter, dtype=jnp.float32) / quarter)

    def axis_angles(pos):
        ang = pos.astype(jnp.float32)[:, None] * inv_freq[None, :]
        return jnp.concatenate([ang, ang], axis=-1)

    ang = jnp.concatenate([axis_angles(row_ids), axis_angles(col_ids)], axis=-1)
    return jnp.cos(ang), jnp.sin(ang)


def apply_rope(x, cos, sin):
    shp = x.shape
    n, d = shp[1], shp[-1]
    bshape = (n,) + (1,) * (x.ndim - 3) + (d,)
    r = x.reshape(shp[:-1] + (2, 2, d // 4))
    rot = jnp.stack([-r[..., 1, :], r[..., 0, :]], axis=-2).reshape(shp)
    return x * cos.reshape(bshape).astype(x.dtype) + rot * sin.reshape(bshape).astype(x.dtype)


def sweep_query_blocks(block_fn, q):
    b, n = q.shape[:2]
    nb = n // Q_BLOCK
    qb = jnp.moveaxis(q.reshape((b, nb, Q_BLOCK) + q.shape[2:]), 1, 0)
    ob = lax.map(block_fn, qb)
    return jnp.moveaxis(ob, 0, 1).reshape((b, n) + ob.shape[3:])


def diff_attend(q, k, v, lam):
    scale = q.shape[-1] ** -0.5

    def block(qb):
        s = jnp.einsum('bqhcd,bkhcd->bhcqk', qb, k).astype(jnp.float32) * scale
        p = jax.nn.softmax(s, axis=-1)
        w = (p[:, :, 0] - lam * p[:, :, 1]).astype(v.dtype)
        return jnp.einsum('bhqk,bkhv->bqhv', w, v)

    return sweep_query_blocks(block, q)


def gqa_attend(q, k, v):
    scale = q.shape[-1] ** -0.5

    def block(qb):
        s = jnp.einsum('bqhgd,bkhd->bhgqk', qb, k).astype(jnp.float32) * scale
        p = jax.nn.softmax(s, axis=-1).astype(v.dtype)
        return jnp.einsum('bhgqk,bkhd->bqhgd', p, v)

    return sweep_query_blocks(block, q)


def forget_gate(z, lb):
    logf = jnp.logaddexp(jnp.log(lb), jnp.log1p(-lb) + jax.nn.log_sigmoid(z))
    return logf, -jnp.expm1(logf)


def chunk_scan(q, k, v, logf, s0):
    b, n, h, _ = q.shape
    nc = n // A_CHUNK

    def chunks(t):
        return jnp.moveaxis(t.reshape(b, nc, A_CHUNK, h, t.shape[-1]), 1, 0)

    causal = jnp.tril(jnp.ones((A_CHUNK, A_CHUNK), dtype=bool))[None, :, :, None, None]

    def step(s, inp):
        qc, kc, vc, gc = inp
        cum = jnp.cumsum(gc, axis=1)
        o_inter = jnp.einsum('bthk,bhkv->bthv', qc * jnp.exp(cum), s)
        decay = jnp.exp(jnp.where(causal, cum[:, :, None] - cum[:, None, :], -jnp.inf))
        att = jnp.einsum('bthk,bshk,btshk->bhts', qc, kc, decay)
        o_intra = jnp.einsum('bhts,bshv->bthv', att, vc)
        last = cum[:, -1]
        s_new = jnp.exp(last)[..., None] * s + jnp.einsum(
            'bshk,bshv->bhkv', kc * jnp.exp(last[:, None] - cum), vc)
        return s_new, o_inter + o_intra

    s_fin, o = lax.scan(step, s0.astype(jnp.float32), (chunks(q), chunks(k), chunks(v), chunks(logf)))
    return jnp.moveaxis(o, 0, 1).reshape(b, n, h, v.shape[-1]), s_fin


def hgrn2_mix(h, w_in, w_out, o_norm, lb, s0_fwd, s0_bwd):
    b, n, _ = h.shape
    q, zf, zb, v, g = jnp.split(jnp.dot(h, w_in), 5, axis=-1)
    shp = (b, n, A_HEADS, A_DK)
    q = jax.nn.silu(q).reshape(shp).astype(jnp.float32)
    v = v.reshape(b, n, A_HEADS, A_DV).astype(jnp.float32)
    lb = lb.reshape(2, A_HEADS, A_DK)
    logf_f, k_f = forget_gate(zf.reshape(shp).astype(jnp.float32), lb[0])
    logf_b, k_b = forget_gate(zb.reshape(shp).astype(jnp.float32), lb[1])
    o_f, s_f = chunk_scan(q, k_f, v, logf_f, s0_fwd)
    flip = lambda t: jnp.flip(t, axis=1)
    o_b, s_b = chunk_scan(flip(q), flip(k_b), flip(v), flip(logf_b), s0_bwd)
    o = rms_norm((o_f + flip(o_b)).astype(h.dtype), o_norm)
    o = o * jax.nn.silu(g).reshape(b, n, A_HEADS, A_DV)
    return jnp.dot(o.reshape(b, n, A_WIDTH), w_out), s_f, s_b


def diff_mix(h, w_in, w_out, q_norm, k_norm, lam_vecs, subln, lam_init, rope, ctx):
    b, n, _ = h.shape
    q, k, v, g = jnp.split(jnp.dot(h, w_in), 4, axis=-1)
    q = rms_norm(q.reshape(b, n, B_HEADS, 2, B_DK), q_norm)
    k = rms_norm(k.reshape(b, n, B_HEADS, 2, B_DK), k_norm)
    v = v.reshape(b, n, B_HEADS, B_DV)
    lv = lam_vecs.astype(jnp.float32)
    lam = jnp.exp(jnp.sum(lv[0] * lv[1])) - jnp.exp(jnp.sum(lv[2] * lv[3])) + lam_init
    if ctx is None:
        k_all, v_all = k, v
    else:
        cos, sin = rope
        q = apply_rope(q, cos, sin)
        k_all = jnp.concatenate([apply_rope(k, cos, sin), ctx[0].astype(k.dtype)], axis=1)
        v_all = jnp.concatenate([v, ctx[1].astype(v.dtype)], axis=1)
    o = diff_attend(q, k_all, v_all, lam)
    o = rms_norm(o, subln) * (1.0 - lam_init)
    o = o * jax.nn.silu(g).reshape(b, n, B_HEADS, B_DV)
    return jnp.dot(o.reshape(b, n, B_WIDTH), w_out), (k, v)


def gqa_mix(h, w_in, w_out, q_norm, k_norm, rope, ctx):
    b, n, _ = h.shape
    q, k, v, g = jnp.split(jnp.dot(h, w_in), [C_WIDTH, C_WIDTH + C_KV_WIDTH, C_WIDTH + 2 * C_KV_WIDTH], axis=-1)
    q = rms_norm(q.reshape(b, n, C_KV_HEADS, C_GROUP, C_DH), q_norm)
    k = rms_norm(k.reshape(b, n, C_KV_HEADS, C_DH), k_norm)
    v = v.reshape(b, n, C_KV_HEADS, C_DH)
    if ctx is None:
        k_all, v_all = k, v
    else:
        cos, sin = rope
        q = apply_rope(q, cos, sin)
        k_all = jnp.concatenate([apply_rope(k, cos, sin), ctx[0].astype(k.dtype)], axis=1)
        v_all = jnp.concatenate([v, ctx[1].astype(v.dtype)], axis=1)
    o = gqa_attend(q, k_all, v_all)
    o = o.reshape(b, n, C_WIDTH) * jax.nn.silu(g)
    return jnp.dot(o, w_out), (k, v)


def diff_lambda_init(layer):
    return 0.8 - 0.6 * math.exp(-0.3 * layer)


def setup_inputs(seed: int = 0) -> dict:
    key = jax.random.key(seed)
    ks = jax.random.split(key, 32)
    D = D_MODEL

    def nrm(k, shape, s):
        return jax.random.normal(k, shape, jnp.float32) * s

    return {
        "x_prompt": nrm(ks[0], (BATCH, SEQ, D), 1.0),
        "x_sample": nrm(ks[1], (DEC_BATCH, DEC_SEQ, D), 1.0),
        "state_a": nrm(ks[2], (DEC_BATCH, N_A_LAYERS, 2, A_HEADS, A_DK, A_DV), 0.5),
        "cache_b_k": nrm(ks[3], (DEC_BATCH, N_B_LAYERS, PAST_LEN, B_HEADS, 2, B_DK), 1.0),
        "cache_b_v": nrm(ks[4], (DEC_BATCH, N_B_LAYERS, PAST_LEN, B_HEADS, B_DV), 1.0),
        "cache_c_k": nrm(ks[5], (DEC_BATCH, N_C_LAYERS, PAST_LEN, C_KV_HEADS, C_DH), 1.0),
        "cache_c_v": nrm(ks[6], (DEC_BATCH, N_C_LAYERS, PAST_LEN, C_KV_HEADS, C_DH), 1.0),
        "c": nrm(ks[7], (DEC_BATCH, D), 1.0),
        "c_ctx": nrm(ks[8], (D,), 1.0),
        "norm_w": 1.0 + nrm(ks[9], (DEPTH, D), 0.02),
        "mod_w": nrm(ks[10], (DEPTH, D, 3 * D), D ** -0.5),
        "mod_b": nrm(ks[11], (DEPTH, 3 * D), 0.01),
        "a_w_in": nrm(ks[12], (N_A_LAYERS, D, 5 * A_WIDTH), D ** -0.5),
        "a_w_out": nrm(ks[13], (N_A_LAYERS, A_WIDTH, D), A_WIDTH ** -0.5),
        "a_o_norm": 1.0 + nrm(ks[14], (N_A_LAYERS, A_DV), 0.02),
        "a_lower_bound": nrm(ks[15], (DEPTH, 2, A_HEADS * A_DK), 0.1),
        "b_w_in": nrm(ks[16], (N_B_LAYERS, D, 4 * B_WIDTH), D ** -0.5),
        "b_w_out": nrm(ks[17], (N_B_LAYERS, B_WIDTH, D), B_WIDTH ** -0.5),
        "b_q_norm": 1.0 + nrm(ks[18], (N_B_LAYERS, B_DK), 0.02),
        "b_k_norm": 1.0 + nrm(ks[19], (N_B_LAYERS, B_DK), 0.02),
        "b_lambda": nrm(ks[20], (N_B_LAYERS, 4, B_DK), 0.1),
        "b_subln": 1.0 + nrm(ks[21], (N_B_LAYERS, B_DV), 0.02),
        "c_w_in": nrm(ks[22], (N_C_LAYERS, D, 2 * C_WIDTH + 2 * C_KV_WIDTH), D ** -0.5),
        "c_w_out": nrm(ks[23], (N_C_LAYERS, C_WIDTH, D), C_WIDTH ** -0.5),
        "c_q_norm": 1.0 + nrm(ks[24], (N_C_LAYERS, C_DH), 0.02),
        "c_k_norm": 1.0 + nrm(ks[25], (N_C_LAYERS, C_DH), 0.02),
    }


def reference(x_prompt, x_sample, state_a, cache_b_k, cache_b_v, cache_c_k, cache_c_v, c, c_ctx,
              norm_w, mod_w, mod_b, a_w_in, a_w_out, a_o_norm, a_lower_bound,
              b_w_in, b_w_out, b_q_norm, b_k_norm, b_lambda, b_subln,
              c_w_in, c_w_out, c_q_norm, c_k_norm):
    lb_all = jnp.cumsum(jax.nn.softmax(a_lower_bound.astype(jnp.float32), axis=0), axis=0)
    lb_all = lb_all - lb_all[0:1]

    def layer_mixer(layer, h, ctx, rope):
        kind, j = layer % N_MIXERS, layer // N_MIXERS
        if kind == 0:
            if ctx is None:
                zero = jnp.zeros((h.shape[0], A_HEADS, A_DK, A_DV), jnp.float32)
                ctx = (zero, zero)
            out, s_f, s_b = hgrn2_mix(h, a_w_in[j], a_w_out[j], a_o_norm[j], lb_all[layer], ctx[0], ctx[1])
            return out, (s_f, s_b)
        if kind == 1:
            return diff_mix(h, b_w_in[j], b_w_out[j], b_q_norm[j], b_k_norm[j], b_lambda[j], b_subln[j],
                            diff_lambda_init(layer), rope, ctx)
        return gqa_mix(h, c_w_in[j], c_w_out[j], c_q_norm[j], c_k_norm[j], rope, ctx)

    y = x_prompt
    cond_ctx = c_ctx[None, :]
    ctx_out = []
    for layer in range(DEPTH):
        h, gate = adaln(y, norm_w[layer], mod_w[layer], mod_b[layer], cond_ctx)
        out, ctx_t = layer_mixer(layer, h, None, None)
        y = y + gate * out
        ctx_out.append(ctx_t)
    y_prompt = y

    a_layers = [l for l in range(DEPTH) if l % N_MIXERS == 0]
    b_layers = [l for l in range(DEPTH) if l % N_MIXERS == 1]
    c_layers = [l for l in range(DEPTH) if l % N_MIXERS == 2]
    new_state_a = jnp.stack([jnp.stack(ctx_out[l], axis=1) for l in a_layers], axis=1).astype(x_prompt.dtype)
    new_cache_b_k = jnp.stack([ctx_out[l][0] for l in b_layers], axis=1)
    new_cache_b_v = jnp.stack([ctx_out[l][1] for l in b_layers], axis=1)
    new_cache_c_k = jnp.stack([ctx_out[l][0] for l in c_layers], axis=1)
    new_cache_c_v = jnp.stack([ctx_out[l][1] for l in c_layers], axis=1)

    rope = axial_rope_tables(x_sample.shape[1], B_DK)
    z = x_sample
    for layer in range(DEPTH):
        kind, j = layer % N_MIXERS, layer // N_MIXERS
        if kind == 0:
            ctx = (state_a[:, j, 0], state_a[:, j, 1])
        elif kind == 1:
            ctx = (cache_b_k[:, j], cache_b_v[:, j])
        else:
            ctx = (cache_c_k[:, j], cache_c_v[:, j])
        h, gate = adaln(z, norm_w[layer], mod_w[layer], mod_b[layer], c)
        out, _ = layer_mixer(layer, h, ctx, rope)
        z = z + gate * out
    y_sample = z

    return (y_prompt, y_sample, new_state_a, new_cache_b_k, new_cache_b_v, new_cache_c_k, new_cache_c_v)
```

```python
import functools
import math

import jax
import jax.numpy as jnp
from jax import lax
from jax.experimental import pallas as pl
from jax.experimental.pallas import tpu as pltpu

F32 = jnp.float32
BF16 = jnp.bfloat16

D_MODEL = 2048
DEPTH = 4
GRID_W = 64
N_MIXERS = 3
EPS = 1e-6
ROPE_THETA = 10000.0
HEAD = 128
A_HEADS = D_MODEL // HEAD
B_HEADS = D_MODEL // (2 * HEAD)
C_HEADS = D_MODEL // HEAD
C_KV_HEADS = C_HEADS // 4
C_GROUP = C_HEADS // C_KV_HEADS
COND_ROWS = 8

V7X_VMEM_LIMIT_BYTES = 56 * 1024 * 1024
SCAN_CHUNK = 256


def _cparams(n_axes):
    return pltpu.CompilerParams(
        dimension_semantics=("arbitrary",) * n_axes,
        vmem_limit_bytes=V7X_VMEM_LIMIT_BYTES,
    )


def _sigmoid(x):
    return 1.0 / (1.0 + jnp.exp(-x))


def _silu(x):
    return x * _sigmoid(x)


def _rms(x, w):
    return x * lax.rsqrt(jnp.mean(x * x, axis=-1, keepdims=True) + EPS) * w


def _mod_kernel(cond_ref, w_ref, b_ref, o_ref):
    s = _silu(cond_ref[...]).astype(BF16)
    o_ref[...] = jnp.dot(s, w_ref[...].astype(BF16), preferred_element_type=F32) + b_ref[...]


def _mod_rows(cond, mod_w, mod_b):
    d3 = 3 * D_MODEL
    tn = 768
    out = pl.pallas_call(
        _mod_kernel,
        grid=(DEPTH, d3 // tn),
        in_specs=[
            pl.BlockSpec((COND_ROWS, D_MODEL), lambda l, j: (0, 0)),
            pl.BlockSpec((None, D_MODEL, tn), lambda l, j: (l, 0, j)),
            pl.BlockSpec((None, 1, tn), lambda l, j: (l, 0, j)),
        ],
        out_specs=pl.BlockSpec((None, COND_ROWS, tn), lambda l, j: (l, 0, j)),
        out_shape=jax.ShapeDtypeStruct((DEPTH, COND_ROWS, d3), F32),
        compiler_params=_cparams(2),
        name="adaln_rows",
    )(cond, mod_w, mod_b.reshape(DEPTH, 1, d3))
    return out.reshape(DEPTH, COND_ROWS, 3, D_MODEL)


def _inproj_kernel(x_ref, mod_ref, nw_ref, w_ref, o_ref, h_ref):
    @pl.when(pl.program_id(1) == 0)
    def _():
        xn = _rms(x_ref[...], nw_ref[...])
        h_ref[...] = (xn * (1.0 + mod_ref[1:2, :]) + mod_ref[0:1, :]).astype(BF16)

    o_ref[...] = jnp.dot(h_ref[...], w_ref[...], preferred_element_type=F32).astype(o_ref.dtype)


def _in_proj(x, mod, norm_w, w, row_of_tile, tm, tn):
    t, n = x.shape[0], w.shape[1]
    return pl.pallas_call(
        _inproj_kernel,
        grid=(t // tm, n // tn),
        in_specs=[
            pl.BlockSpec((tm, D_MODEL), lambda i, j: (i, 0)),
            pl.BlockSpec((None, 3, D_MODEL), lambda i, j: (row_of_tile(i), 0, 0)),
            pl.BlockSpec((1, D_MODEL), lambda i, j: (0, 0)),
            pl.BlockSpec((D_MODEL, tn), lambda i, j: (0, j)),
        ],
        out_specs=pl.BlockSpec((tm, tn), lambda i, j: (i, j)),
        out_shape=jax.ShapeDtypeStruct((t, n), F32),
        scratch_shapes=[pltpu.VMEM((tm, D_MODEL), BF16)],
        compiler_params=_cparams(2),
        name="in_proj",
    )(x, mod, norm_w.reshape(1, D_MODEL), w)


def _outproj_kernel(o_ref, w_ref, x_ref, mod_ref, y_ref):
    acc = jnp.dot(o_ref[...], w_ref[...], preferred_element_type=F32)
    y_ref[...] = x_ref[...] + mod_ref[2:3, :] * acc


def _out_proj(o, w, x, mod, row_of_tile, tm, tn):
    t, kdim = o.shape
    return pl.pallas_call(
        _outproj_kernel,
        grid=(t // tm, D_MODEL // tn),
        in_specs=[
            pl.BlockSpec((tm, kdim), lambda i, j: (i, 0)),
            pl.BlockSpec((kdim, tn), lambda i, j: (0, j)),
            pl.BlockSpec((tm, tn), lambda i, j: (i, j)),
            pl.BlockSpec((None, 3, tn), lambda i, j: (row_of_tile(i), 0, j)),
        ],
        out_specs=pl.BlockSpec((tm, tn), lambda i, j: (i, j)),
        out_shape=jax.ShapeDtypeStruct((t, D_MODEL), F32),
        compiler_params=_cparams(2),
        name="out_proj",
    )(o, w, x, mod)


def _rope_tables(n):
    pos = jnp.arange(n)
    quarter = HEAD // 4
    inv_freq = ROPE_THETA ** (-jnp.arange(quarter, dtype=F32) / quarter)

    def axis_angles(p):
        ang = p.astype(F32)[:, None] * inv_freq[None, :]
        return jnp.concatenate([ang, ang], axis=-1)

    ang = jnp.concatenate([axis_angles(pos // GRID_W), axis_angles(pos % GRID_W)], axis=-1)
    cos, sin = jnp.cos(ang), jnp.sin(ang)
    first = (jnp.arange(HEAD) % (2 * quarter)) < quarter
    return jnp.stack([cos, jnp.where(first, -sin, 0.0), jnp.where(first, 0.0, sin)])


def _rope(x, tab_ref):
    up = pltpu.roll(x, HEAD - HEAD // 4, 1)
    down = pltpu.roll(x, HEAD // 4, 1)
    return x * tab_ref[0] + up * tab_ref[1] + down * tab_ref[2]


def _kvprep_kernel(*refs, n_kheads, rope, emit_norm):
    k_ref, v_ref, kn_ref = refs[:3]
    pos = 3
    tab_ref = None
    if rope:
        tab_ref = refs[pos]
        pos += 1
    kh_ref, vh_ref = refs[pos], refs[pos + 1]
    kn_out = refs[pos + 2] if emit_norm else None
    for h in range(n_kheads):
        sl = slice(h * HEAD, (h + 1) * HEAD)
        kn = _rms(k_ref[:, sl], kn_ref[...])
        if emit_norm:
            kn_out[:, sl] = kn
        if rope:
            kn = _rope(kn, tab_ref)
        kh_ref[:, sl] = kn.astype(BF16)
    vh_ref[...] = v_ref[...].astype(BF16)


def _kv_prep(p, k_col, k_w, v_col, v_w, k_norm, rope_tab, n_seq, tr=512):
    t = p.shape[0]
    rope = rope_tab is not None
    emit_norm = not rope
    per_seq = n_seq // tr
    in_specs = [
        pl.BlockSpec((tr, k_w), lambda i: (i, k_col // k_w)),
        pl.BlockSpec((tr, v_w), lambda i: (i, v_col // v_w)),
        pl.BlockSpec((1, HEAD), lambda i: (0, 0)),
    ]
    args = [p, p, k_norm.reshape(1, HEAD)]
    if rope:
        in_specs.append(pl.BlockSpec((3, tr, HEAD), lambda i: (0, i % per_seq, 0)))
        args.append(rope_tab)
    out_specs = [pl.BlockSpec((tr, k_w), lambda i: (i, 0)), pl.BlockSpec((tr, v_w), lambda i: (i, 0))]
    out_shape = [jax.ShapeDtypeStruct((t, k_w), BF16), jax.ShapeDtypeStruct((t, v_w), BF16)]
    if emit_norm:
        out_specs.append(pl.BlockSpec((tr, k_w), lambda i: (i, 0)))
        out_shape.append(jax.ShapeDtypeStruct((t, k_w), F32))
    return pl.pallas_call(
        functools.partial(_kvprep_kernel, n_kheads=k_w // HEAD, rope=rope, emit_norm=emit_norm),
        grid=(t // tr,),
        in_specs=in_specs,
        out_specs=out_specs,
        out_shape=out_shape,
        compiler_params=_cparams(1),
        name="kv_prep",
    )(*args)


_NT = (((1,), (1,)), ((), ()))


def _softmax_parts(q, k_list):
    c = (HEAD ** -0.5) * math.log2(math.e)
    s_list = [lax.dot_general(q, k, _NT, preferred_element_type=F32) for k in k_list]
    m = s_list[0].max(axis=-1, keepdims=True)
    for s in s_list[1:]:
        m = jnp.maximum(m, s.max(axis=-1, keepdims=True))
    e_list = [jnp.exp2((s - m) * c) for s in s_list]
    l = e_list[0].sum(axis=-1, keepdims=True)
    for e in e_list[1:]:
        l = l + e.sum(axis=-1, keepdims=True)
    return e_list, l


def _diff_attn_kernel(*refs, has_ctx, lam_init):
    q_ref, g_ref, kh_ref, vh_ref, qn_ref, lamv_ref, sub_ref = refs[:7]
    pos = 7
    if has_ctx:
        tab_ref, ck_ref, cv_ref = refs[pos:pos + 3]
        pos += 3
    o_ref = refs[pos]

    lv = lamv_ref[...]
    lam = (jnp.exp(jnp.sum(lv[0:1] * lv[1:2], keepdims=True))
           - jnp.exp(jnp.sum(lv[2:3] * lv[3:4], keepdims=True)) + lam_init)

    e_comp, l_comp = [], []
    for comp in range(2):
        sl = slice(comp * HEAD, (comp + 1) * HEAD)
        q = _rms(q_ref[:, sl], qn_ref[...])
        k_list = [kh_ref[:, sl]]
        if has_ctx:
            q = _rope(q, tab_ref)
            k_list.append(ck_ref[:, sl].astype(BF16))
        e_list, l = _softmax_parts(q.astype(BF16), k_list)
        e_comp.append(e_list)
        l_comp.append(l)
    r1 = 1.0 / l_comp[0]
    r2 = lam / l_comp[1]
    v_list = [vh_ref[...]] + ([cv_ref[...].astype(BF16)] if has_ctx else [])
    o = None
    for e1, e2, v in zip(e_comp[0], e_comp[1], v_list):
        w = (e1 * r1 - e2 * r2).astype(BF16)
        part = jnp.dot(w, v, preferred_element_type=F32)
        o = part if o is None else o + part
    o = _rms(o, sub_ref[...]) * (1.0 - lam_init)
    o_ref[...] = (o * _silu(g_ref[...])).astype(BF16)


def _diff_attention(p, kh, vh, q_norm, lam_vecs, subln, lam_init, n_batch, n_seq, tq, rope_tab=None, ctx=None):
    has_ctx = ctx is not None
    w2 = 2 * HEAD
    nq = n_seq // tq
    g_col0 = 3 * D_MODEL // w2
    in_specs = [
        pl.BlockSpec((tq, w2), lambda b, h, i: (b * nq + i, h)),
        pl.BlockSpec((tq, w2), lambda b, h, i: (b * nq + i, g_col0 + h)),
        pl.BlockSpec((n_seq, w2), lambda b, h, i: (b, h)),
        pl.BlockSpec((n_seq, w2), lambda b, h, i: (b, h)),
        pl.BlockSpec((1, HEAD), lambda b, h, i: (0, 0)),
        pl.BlockSpec((4, HEAD), lambda b, h, i: (0, 0)),
        pl.BlockSpec((1, w2), lambda b, h, i: (0, 0)),
    ]
    args = [p, p, kh, vh, q_norm.reshape(1, HEAD), lam_vecs, subln.reshape(1, w2)]
    if has_ctx:
        ck, cv = ctx
        n_ctx = ck.shape[1]
        in_specs += [
            pl.BlockSpec((3, tq, HEAD), lambda b, h, i: (0, i, 0)),
            pl.BlockSpec((None, n_ctx, w2), lambda b, h, i: (b, 0, h)),
            pl.BlockSpec((None, n_ctx, w2), lambda b, h, i: (b, 0, h)),
        ]
        args += [rope_tab, ck, cv]
    return pl.pallas_call(
        functools.partial(_diff_attn_kernel, has_ctx=has_ctx, lam_init=lam_init),
        grid=(n_batch, B_HEADS, nq),
        in_specs=in_specs,
        out_specs=pl.BlockSpec((tq, w2), lambda b, h, i: (b * nq + i, h)),
        out_shape=jax.ShapeDtypeStruct((n_batch * n_seq, D_MODEL), BF16),
        compiler_params=_cparams(3),
        name="diff_attention",
    )(*args)


def _gqa_kernel(*refs, has_ctx, tq):
    q_ref, g_ref, kh_ref, vh_ref, qn_ref = refs[:5]
    pos = 5
    if has_ctx:
        tab_ref, ck_ref, cv_ref = refs[pos:pos + 3]
        pos += 3
    o_ref = refs[pos]

    qs = []
    for gi in range(C_GROUP):
        q = _rms(q_ref[:, gi * HEAD:(gi + 1) * HEAD], qn_ref[...])
        if has_ctx:
            q = _rope(q, tab_ref)
        qs.append(q.astype(BF16))
    q_all = jnp.concatenate(qs, axis=0)
    k_list = [kh_ref[...]] + ([ck_ref[...].astype(BF16)] if has_ctx else [])
    v_list = [vh_ref[...]] + ([cv_ref[...].astype(BF16)] if has_ctx else [])
    e_list, l = _softmax_parts(q_all, k_list)
    o = None
    for e, v in zip(e_list, v_list):
        part = jnp.dot(e.astype(BF16), v, preferred_element_type=F32)
        o = part if o is None else o + part
    o = o * (1.0 / l)
    for gi in range(C_GROUP):
        sl = slice(gi * HEAD, (gi + 1) * HEAD)
        o_ref[:, sl] = (o[gi * tq:(gi + 1) * tq] * _silu(g_ref[:, sl])).astype(BF16)


def _gqa_attention(p, kh, vh, q_norm, n_batch, n_seq, tq, rope_tab=None, ctx=None):
    has_ctx = ctx is not None
    wq = C_GROUP * HEAD
    nq = n_seq // tq
    g_col0 = (D_MODEL + 2 * C_KV_HEADS * HEAD) // wq
    in_specs = [
        pl.BlockSpec((tq, wq), lambda b, h, i: (b * nq + i, h)),
        pl.BlockSpec((tq, wq), lambda b, h, i: (b * nq + i, g_col0 + h)),
        pl.BlockSpec((n_seq, HEAD), lambda b, h, i: (b, h)),
        pl.BlockSpec((n_seq, HEAD), lambda b, h, i: (b, h)),
        pl.BlockSpec((1, HEAD), lambda b, h, i: (0, 0)),
    ]
    args = [p, p, kh, vh, q_norm.reshape(1, HEAD)]
    if has_ctx:
        ck, cv = ctx
        n_ctx = ck.shape[1]
        in_specs += [
            pl.BlockSpec((3, tq, HEAD), lambda b, h, i: (0, i, 0)),
            pl.BlockSpec((None, n_ctx, HEAD), lambda b, h, i: (b, 0, h)),
            pl.BlockSpec((None, n_ctx, HEAD), lambda b, h, i: (b, 0, h)),
        ]
        args += [rope_tab, ck, cv]
    return pl.pallas_call(
        functools.partial(_gqa_kernel, has_ctx=has_ctx, tq=tq),
        grid=(n_batch, C_KV_HEADS, nq),
        in_specs=in_specs,
        out_specs=pl.BlockSpec((tq, wq), lambda b, h, i: (b * nq + i, h)),
        out_shape=jax.ShapeDtypeStruct((n_batch * n_seq, D_MODEL), BF16),
        compiler_params=_cparams(3),
        name="gqa_attention",
    )(*args)


def _forget_gate(z, lb):
    t = jnp.exp(-jnp.abs(z))
    r = 1.0 / (1.0 + t)
    tr = t * r
    pos = z >= 0.0
    sig = jnp.where(pos, r, tr)
    nsig = jnp.where(pos, tr, r)
    log_sig = jnp.minimum(z, 0.0) - jnp.log1p(t)
    one_m = 1.0 - lb
    logf = jnp.where(lb > 0.0, jnp.log(lb + one_m * sig), log_sig)
    return logf, one_m * nsig


def _block_first(x, b, rowmod):
    c = x.shape[0]
    if b == 1:
        return x
    if b >= 8:
        g = x.reshape(c // b, b, HEAD)
        return jnp.broadcast_to(g[:, 0:1, :], g.shape).reshape(c, HEAD)
    out = x
    for d in range(1, b):
        out = jnp.where(rowmod == d, pltpu.roll(x, d, 0), out)
    return out


def _block_last(x, b, rowmod):
    c = x.shape[0]
    if b == 1:
        return x
    if b >= 8:
        g = x.reshape(c // b, b, HEAD)
        return jnp.broadcast_to(g[:, b - 1:b, :], g.shape).reshape(c, HEAD)
    out = x
    for e in range(1, b):
        out = jnp.where(rowmod == b - 1 - e, pltpu.roll(x, c - e, 0), out)
    return out


def _scan_chunk(q, kd, logf, v, s_in, tri, level, rows, rev):
    c = q.shape[0]
    cum = jnp.dot(tri, logf, precision=lax.Precision.HIGHEST, preferred_element_type=F32)
    cum_ex = cum - logf
    v_b = v.astype(BF16)
    near, far = (_block_last, _block_first) if rev else (_block_first, _block_last)

    att = jnp.zeros((c, c), F32)
    b = c // 2
    while b >= 1:
        rowmod = rows % b if 1 < b < 8 else None
        qt = (q * jnp.exp(cum - near(cum_ex, b, rowmod))).astype(BF16)
        kt = (kd * jnp.exp(far(cum, b, rowmod) - cum)).astype(BF16)
        a = lax.dot_general(qt, kt, _NT, preferred_element_type=F32)
        att = jnp.where(level == int(math.log2(b)), a, att)
        b //= 2
    o = jnp.dot(att.astype(BF16), v_b, preferred_element_type=F32)
    o = o + jnp.sum(q * kd, axis=-1, keepdims=True) * v
    if s_in is not None:
        o = o + jnp.dot((q * jnp.exp(cum)).astype(BF16), s_in.astype(BF16), preferred_element_type=F32)

    tot = cum[0:1, :] if rev else cum[c - 1:c, :]
    kl = (kd * jnp.exp(tot - cum)).T.astype(BF16)
    s_out = jnp.dot(kl, v_b, preferred_element_type=F32)
    if s_in is not None:
        s_out = s_out + jnp.broadcast_to(jnp.exp(tot), (HEAD, HEAD)).T * s_in
    return o, s_out


def _hgrn_kernel(*refs, n_seq, n_heads, has_state):
    q_ref, zf_ref, zb_ref, v_ref, g_ref, lb_ref, on_ref = refs[:7]
    pos = 7
    s0_ref = None
    if has_state:
        s0_ref = refs[pos]
        pos += 1
    o_ref = refs[pos]
    pos += 1
    sout_ref = None
    if not has_state:
        sout_ref = refs[pos]
        pos += 1
    of_ref = refs[pos]

    c = SCAN_CHUNK
    n_chunks = n_seq // c
    row = lax.broadcasted_iota(jnp.int32, (c, c), 0)
    col = lax.broadcasted_iota(jnp.int32, (c, c), 1)
    x = row ^ col
    lv = jnp.full((c, c), -1, jnp.int32)
    for j in range(int(math.log2(c))):
        lv = lv + (x >= (1 << j)).astype(jnp.int32)
    level_f = jnp.where(col < row, lv, -1)
    level_b = jnp.where(col > row, lv, -1)
    tri_f = (col <= row).astype(F32)
    tri_b = (col >= row).astype(F32)
    rows = lax.broadcasted_iota(jnp.int32, (c, HEAD), 0)

    for h in range(n_heads):
        sl = slice(h * HEAD, (h + 1) * HEAD)
        lb_f = lb_ref[0:1, sl]
        lb_b = lb_ref[1:2, sl]

        def fwd_chunk(ci, s, sl=sl, lb_f=lb_f):
            r = pl.ds(pl.multiple_of(ci * c, c), c)
            logf, kd = _forget_gate(zf_ref[r, sl], lb_f)
            o, s_new = _scan_chunk(_silu(q_ref[r, sl]), kd, logf, v_ref[r, sl], s, tri_f, level_f, rows, False)
            of_ref[r, :] = o
            return s_new

        def bwd_chunk(ci, s, sl=sl, lb_b=lb_b):
            r = pl.ds(pl.multiple_of(ci * c, c), c)
            logf, kd = _forget_gate(zb_ref[r, sl], lb_b)
            o, s_new = _scan_chunk(_silu(q_ref[r, sl]), kd, logf, v_ref[r, sl], s, tri_b, level_b, rows, True)
            o = _rms(o + of_ref[r, :], on_ref[...]) * _silu(g_ref[r, sl])
            o_ref[r, sl] = o.astype(BF16)
            return s_new

        if has_state:
            s_f = lax.fori_loop(0, n_chunks, fwd_chunk, s0_ref[0, h])
            lax.fori_loop(0, n_chunks, lambda i, s: bwd_chunk(n_chunks - 1 - i, s), s0_ref[1, h])
        else:
            s_f = fwd_chunk(0, None)
            if n_chunks > 1:
                s_f = lax.fori_loop(1, n_chunks, fwd_chunk, s_f)
            s_b = bwd_chunk(n_chunks - 1, None)
            if n_chunks > 1:
                s_b = lax.fori_loop(1, n_chunks, lambda i, s: bwd_chunk(n_chunks - 1 - i, s), s_b)
            sout_ref[0, h] = s_f
            sout_ref[1, h] = s_b


def _hgrn_scan(p, lb, o_norm, n_batch, n_seq, heads_per_step, state=None):
    has_state = state is not None
    wb = heads_per_step * HEAD
    sec = D_MODEL // wb
    col = lambda s: (lambda b, h: (b, s * sec + h))
    in_specs = [pl.BlockSpec((n_seq, wb), col(s)) for s in range(5)]
    in_specs += [
        pl.BlockSpec((2, wb), lambda b, h: (0, h)),
        pl.BlockSpec((1, HEAD), lambda b, h: (0, 0)),
    ]
    args = [p] * 5 + [lb, o_norm.reshape(1, HEAD)]
    state_spec = pl.BlockSpec((None, 2, heads_per_step, HEAD, HEAD), lambda b, h: (b, 0, h, 0, 0))
    out_specs = [pl.BlockSpec((n_seq, wb), lambda b, h: (b, h))]
    out_shape = [jax.ShapeDtypeStruct((n_batch * n_seq, D_MODEL), BF16)]
    if has_state:
        in_specs.append(state_spec)
        args.append(state)
    else:
        out_specs.append(state_spec)
        out_shape.append(jax.ShapeDtypeStruct((n_batch, 2, A_HEADS, HEAD, HEAD), F32))
    res = pl.pallas_call(
        functools.partial(_hgrn_kernel, n_seq=n_seq, n_heads=heads_per_step, has_state=has_state),
        grid=(n_batch, sec),
        in_specs=in_specs,
        out_specs=out_specs,
        out_shape=out_shape,
        scratch_shapes=[pltpu.VMEM((n_seq, HEAD), F32)],
        compiler_params=_cparams(2),
        name="hgrn2_scan",
    )(*args)
    return (res[0], None) if has_state else (res[0], res[1])


def _diff_lambda_init(layer):
    return 0.8 - 0.6 * math.exp(-0.3 * layer)


def kernel(x_prompt, x_sample, state_a, cache_b_k, cache_b_v, cache_c_k, cache_c_v, c, c_ctx, norm_w, mod_w, mod_b, a_w_in, a_w_out, a_o_norm, a_lower_bound, b_w_in, b_w_out, b_q_norm, b_k_norm, b_lambda, b_subln, c_w_in, c_w_out, c_q_norm, c_k_norm):
    n_ctx_b, n_ctx_s = x_prompt.shape[:2]
    n_lat_b, n_lat_s = x_sample.shape[:2]
    past = cache_b_k.shape[2]
    tm = tn = 1024

    lb_all = jnp.cumsum(jax.nn.softmax(a_lower_bound.astype(F32), axis=0), axis=0)
    lb_all = lb_all - lb_all[0:1]

    cond = jnp.zeros((COND_ROWS, D_MODEL), F32).at[0].set(c_ctx).at[1:1 + n_lat_b].set(c)
    mods = _mod_rows(cond, mod_w, mod_b)
    rope_tab = _rope_tables(n_lat_s)

    lat_tiles = n_lat_s // tm
    groups = [
        dict(x=x_prompt.reshape(-1, D_MODEL), nb=n_ctx_b, ns=n_ctx_s, row=lambda i: 0, latent=False),
        dict(x=x_sample.reshape(-1, D_MODEL), nb=n_lat_b, ns=n_lat_s, row=lambda i: 1 + i // lat_tiles, latent=True),
    ]
    new_a, new_bk, new_bv, new_ck, new_cv = [], [], [], [], []

    for layer in range(DEPTH):
        kind, j = layer % N_MIXERS, layer // N_MIXERS
        w_in = (a_w_in, b_w_in, c_w_in)[kind][j].astype(BF16)
        w_out = (a_w_out, b_w_out, c_w_out)[kind][j].astype(BF16)
        for grp in groups:
            nb, ns, latent = grp["nb"], grp["ns"], grp["latent"]
            p = _in_proj(grp["x"], mods[layer], norm_w[layer], w_in, grp["row"], tm, tn)
            if kind == 0:
                if latent:
                    o, _ = _hgrn_scan(p, lb_all[layer], a_o_norm[j], nb, ns, 1, state=state_a[:, j])
                else:
                    o, s_fin = _hgrn_scan(p, lb_all[layer], a_o_norm[j], nb, ns, 4)
                    new_a.append(s_fin)
            elif kind == 1:
                tab = rope_tab if latent else None
                res = _kv_prep(p, D_MODEL, D_MODEL, 2 * D_MODEL, D_MODEL, b_k_norm[j], tab, ns)
                ctx = None
                if latent:
                    ctx = (cache_b_k[:, j].reshape(nb, past, D_MODEL), cache_b_v[:, j].reshape(nb, past, D_MODEL))
                else:
                    new_bk.append(res[2].reshape(nb, ns, B_HEADS, 2, HEAD))
                    new_bv.append(p[:, 2 * D_MODEL:3 * D_MODEL].reshape(nb, ns, B_HEADS, 2 * HEAD))
                o = _diff_attention(p, res[0], res[1], b_q_norm[j], b_lambda[j], b_subln[j],
                                    _diff_lambda_init(layer), nb, ns, 256, rope_tab=tab, ctx=ctx)
            else:
                kvw = C_KV_HEADS * HEAD
                tab = rope_tab if latent else None
                res = _kv_prep(p, D_MODEL, kvw, D_MODEL + kvw, kvw, c_k_norm[j], tab, ns)
                ctx = None
                if latent:
                    ctx = (cache_c_k[:, j].reshape(nb, past, kvw), cache_c_v[:, j].reshape(nb, past, kvw))
                else:
                    new_ck.append(res[2].reshape(nb, ns, C_KV_HEADS, HEAD))
                    new_cv.append(p[:, D_MODEL + kvw:D_MODEL + 2 * kvw].reshape(nb, ns, C_KV_HEADS, HEAD))
                o = _gqa_attention(p, res[0], res[1], c_q_norm[j], nb, ns, 256 if not latent else 128,
                                   rope_tab=tab, ctx=ctx)
            grp["x"] = _out_proj(o, w_out, grp["x"], mods[layer], grp["row"], tm, tn)

    y_prompt = groups[0]["x"].reshape(x_prompt.shape)
    y_sample = groups[1]["x"].reshape(x_sample.shape)
    return (y_prompt, y_sample, jnp.stack(new_a, axis=1), jnp.stack(new_bk, axis=1), jnp.stack(new_bv, axis=1),
            jnp.stack(new_ck, axis=1), jnp.stack(new_cv, axis=1))
```

```python
import functools
import math

import jax
import jax.numpy as jnp
from jax import lax
from jax.experimental import pallas as pl
from jax.experimental.pallas import tpu as pltpu

F32 = jnp.float32
BF16 = jnp.bfloat16

D_MODEL = 2048
DEPTH = 4
GRID_W = 64
N_MIXERS = 3
EPS = 1e-6
ROPE_THETA = 10000.0
HEAD = 128
A_HEADS = D_MODEL // HEAD
B_HEADS = D_MODEL // (2 * HEAD)
C_HEADS = D_MODEL // HEAD
C_KV_HEADS = C_HEADS // 4
C_GROUP = C_HEADS // C_KV_HEADS
COND_ROWS = 8

V7X_VMEM_LIMIT_BYTES = 56 * 1024 * 1024
SCAN_CHUNK = 256


def _cparams(n_axes):
    return pltpu.CompilerParams(
        dimension_semantics=("arbitrary",) * n_axes,
        vmem_limit_bytes=V7X_VMEM_LIMIT_BYTES,
    )


def _sigmoid(x):
    return 1.0 / (1.0 + jnp.exp(-x))


def _silu(x):
    return x * _sigmoid(x)


def _rms(x, w):
    return x * lax.rsqrt(jnp.mean(x * x, axis=-1, keepdims=True) + EPS) * w


def _mod_kernel(cond_ref, w_ref, b_ref, o_ref):
    s = _silu(cond_ref[...]).astype(BF16)
    o_ref[...] = jnp.dot(s, w_ref[...].astype(BF16), preferred_element_type=F32) + b_ref[...]


def _mod_rows(cond, mod_w, mod_b):
    d3 = 3 * D_MODEL
    tn = 768
    out = pl.pallas_call(
        _mod_kernel,
        grid=(DEPTH, d3 // tn),
        in_specs=[
            pl.BlockSpec((COND_ROWS, D_MODEL), lambda l, j: (0, 0)),
            pl.BlockSpec((None, D_MODEL, tn), lambda l, j: (l, 0, j)),
            pl.BlockSpec((None, 1, tn), lambda l, j: (l, 0, j)),
        ],
        out_specs=pl.BlockSpec((None, COND_ROWS, tn), lambda l, j: (l, 0, j)),
        out_shape=jax.ShapeDtypeStruct((DEPTH, COND_ROWS, d3), F32),
        compiler_params=_cparams(2),
        name="adaln_rows",
    )(cond, mod_w, mod_b.reshape(DEPTH, 1, d3))
    return out.reshape(DEPTH, COND_ROWS, 3, D_MODEL)


def _inproj_kernel(x_ref, mod_ref, nw_ref, w_ref, o_ref, h_ref):
    @pl.when(pl.program_id(1) == 0)
    def _():
        xn = _rms(x_ref[...], nw_ref[...])
        h_ref[...] = (xn * (1.0 + mod_ref[1:2, :]) + mod_ref[0:1, :]).astype(BF16)

    o_ref[...] = jnp.dot(h_ref[...], w_ref[...], preferred_element_type=F32).astype(o_ref.dtype)


def _in_proj(x, mod, norm_w, w, row_of_tile, tm, tn):
    t, n = x.shape[0], w.shape[1]
    return pl.pallas_call(
        _inproj_kernel,
        grid=(t // tm, n // tn),
        in_specs=[
            pl.BlockSpec((tm, D_MODEL), lambda i, j: (i, 0)),
            pl.BlockSpec((None, 3, D_MODEL), lambda i, j: (row_of_tile(i), 0, 0)),
            pl.BlockSpec((1, D_MODEL), lambda i, j: (0, 0)),
            pl.BlockSpec((D_MODEL, tn), lambda i, j: (0, j)),
        ],
        out_specs=pl.BlockSpec((tm, tn), lambda i, j: (i, j)),
        out_shape=jax.ShapeDtypeStruct((t, n), F32),
        scratch_shapes=[pltpu.VMEM((tm, D_MODEL), BF16)],
        compiler_params=_cparams(2),
        name="in_proj",
    )(x, mod, norm_w.reshape(1, D_MODEL), w)


def _outproj_kernel(o_ref, w_ref, x_ref, mod_ref, y_ref):
    acc = jnp.dot(o_ref[...], w_ref[...], preferred_element_type=F32)
    y_ref[...] = x_ref[...] + mod_ref[2:3, :] * acc


def _out_proj(o, w, x, mod, row_of_tile, tm, tn):
    t, kdim = o.shape
    return pl.pallas_call(
        _outproj_kernel,
        grid=(t // tm, D_MODEL // tn),
        in_specs=[
            pl.BlockSpec((tm, kdim), lambda i, j: (i, 0)),
            pl.BlockSpec((kdim, tn), lambda i, j: (0, j)),
            pl.BlockSpec((tm, tn), lambda i, j: (i, j)),
            pl.BlockSpec((None, 3, tn), lambda i, j: (row_of_tile(i), 0, j)),
        ],
        out_specs=pl.BlockSpec((tm, tn), lambda i, j: (i, j)),
        out_shape=jax.ShapeDtypeStruct((t, D_MODEL), F32),
        compiler_params=_cparams(2),
        name="out_proj",
    )(o, w, x, mod)


def _rope_tables(n):
    pos = jnp.arange(n)
    quarter = HEAD // 4
    inv_freq = ROPE_THETA ** (-jnp.arange(quarter, dtype=F32) / quarter)

    def axis_angles(p):
        ang = p.astype(F32)[:, None] * inv_freq[None, :]
        return jnp.concatenate([ang, ang], axis=-1)

    ang = jnp.concatenate([axis_angles(pos // GRID_W), axis_angles(pos % GRID_W)], axis=-1)
    cos, sin = jnp.cos(ang), jnp.sin(ang)
    first = (jnp.arange(HEAD) % (2 * quarter)) < quarter
    return jnp.stack([cos, jnp.where(first, -sin, 0.0), jnp.where(first, 0.0, sin)])


def _rope(x, tab_ref):
    up = pltpu.roll(x, HEAD - HEAD // 4, 1)
    down = pltpu.roll(x, HEAD // 4, 1)
    return x * tab_ref[0] + up * tab_ref[1] + down * tab_ref[2]


def _kvprep_kernel(*refs, n_kheads, rope, emit_norm):
    k_ref, v_ref, kn_ref = refs[:3]
    pos = 3
    tab_ref = None
    if rope:
        tab_ref = refs[pos]
        pos += 1
    kh_ref, vt_ref = refs[pos], refs[pos + 1]
    kn_out = refs[pos + 2] if emit_norm else None
    for h in range(n_kheads):
        sl = slice(h * HEAD, (h + 1) * HEAD)
        kn = _rms(k_ref[:, sl], kn_ref[...])
        if emit_norm:
            kn_out[:, sl] = kn
        if rope:
            kn = _rope(kn, tab_ref)
        kh_ref[:, sl] = kn.astype(BF16)
    vt_ref[...] = v_ref[...].T.astype(BF16)


def _kv_prep(p, k_col, k_w, v_col, v_w, k_norm, rope_tab, n_batch, n_seq):
    t = p.shape[0]
    rope = rope_tab is not None
    emit_norm = not rope
    tr = min(512, n_seq)
    per_seq = n_seq // tr
    in_specs = [
        pl.BlockSpec((tr, k_w), lambda i: (i, k_col // k_w)),
        pl.BlockSpec((tr, v_w), lambda i: (i, v_col // v_w)),
        pl.BlockSpec((1, HEAD), lambda i: (0, 0)),
    ]
    args = [p, p, k_norm.reshape(1, HEAD)]
    if rope:
        in_specs.append(pl.BlockSpec((3, tr, HEAD), lambda i: (0, i % per_seq, 0)))
        args.append(rope_tab)
    out_specs = [
        pl.BlockSpec((tr, k_w), lambda i: (i, 0)),
        pl.BlockSpec((None, v_w, tr), lambda i: (i // per_seq, 0, i % per_seq)),
    ]
    out_shape = [jax.ShapeDtypeStruct((t, k_w), BF16), jax.ShapeDtypeStruct((n_batch, v_w, n_seq), BF16)]
    if emit_norm:
        out_specs.append(pl.BlockSpec((tr, k_w), lambda i: (i, 0)))
        out_shape.append(jax.ShapeDtypeStruct((t, k_w), F32))
    return pl.pallas_call(
        functools.partial(_kvprep_kernel, n_kheads=k_w // HEAD, rope=rope, emit_norm=emit_norm),
        grid=(t // tr,),
        in_specs=in_specs,
        out_specs=out_specs,
        out_shape=out_shape,
        compiler_params=_cparams(1),
        name="kv_prep",
    )(*args)


_NT = (((1,), (1,)), ((), ()))


def _softmax_cols(q, k_list):
    c = (HEAD ** -0.5) * math.log2(math.e)
    s_list = [lax.dot_general(k, q, _NT, preferred_element_type=F32) for k in k_list]
    m = s_list[0].max(axis=0, keepdims=True)
    for s in s_list[1:]:
        m = jnp.maximum(m, s.max(axis=0, keepdims=True))
    e_list = [jnp.exp2((s - m) * c) for s in s_list]
    l = e_list[0].sum(axis=0, keepdims=True)
    for e in e_list[1:]:
        l = l + e.sum(axis=0, keepdims=True)
    return e_list, l


def _diff_attn_kernel(*refs, has_ctx, lam_init, tq):
    q_ref, g_ref, kh_ref, vt_ref, qn_ref, lamv_ref, sub_ref = refs[:7]
    pos = 7
    if has_ctx:
        tab_ref, ck_ref, cvt_ref = refs[pos:pos + 3]
        pos += 3
    o_ref = refs[pos]

    lv = lamv_ref[...]
    lam = (jnp.exp(jnp.sum(lv[0:1] * lv[1:2], keepdims=True))
           - jnp.exp(jnp.sum(lv[2:3] * lv[3:4], keepdims=True)) + lam_init)

    zero = jnp.zeros((tq, HEAD), BF16)
    rows = []
    for comp in range(2):
        q = _rms(q_ref[:, comp * HEAD:(comp + 1) * HEAD], qn_ref[...])
        if has_ctx:
            q = _rope(q, tab_ref)
        q = q.astype(BF16)
        rows.append(jnp.concatenate([q, zero] if comp == 0 else [zero, q], axis=1))
    q_bd = jnp.concatenate(rows, axis=0)
    k_list = [kh_ref[...]] + ([ck_ref[...]] if has_ctx else [])
    vt_list = [vt_ref[...]] + ([cvt_ref[...]] if has_ctx else [])
    e_list, l = _softmax_cols(q_bd, k_list)
    r1 = 1.0 / l[:, :tq]
    r2 = lam / l[:, tq:]
    o_t = None
    for e, vt in zip(e_list, vt_list):
        w = (e[:, :tq] * r1 - e[:, tq:] * r2).astype(BF16)
        part = jnp.dot(vt, w, preferred_element_type=F32)
        o_t = part if o_t is None else o_t + part
    o = _rms(o_t.T, sub_ref[...]) * (1.0 - lam_init)
    o_ref[...] = (o * _silu(g_ref[...])).astype(BF16)


def _diff_attention(p, kh, vt, q_norm, lam_vecs, subln, lam_init, n_batch, n_seq, tq, rope_tab=None, ctx=None):
    has_ctx = ctx is not None
    w2 = 2 * HEAD
    nq = n_seq // tq
    g_col0 = 3 * D_MODEL // w2
    in_specs = [
        pl.BlockSpec((tq, w2), lambda b, h, i: (b * nq + i, h)),
        pl.BlockSpec((tq, w2), lambda b, h, i: (b * nq + i, g_col0 + h)),
        pl.BlockSpec((n_seq, w2), lambda b, h, i: (b, h)),
        pl.BlockSpec((None, w2, n_seq), lambda b, h, i: (b, h, 0)),
        pl.BlockSpec((1, HEAD), lambda b, h, i: (0, 0)),
        pl.BlockSpec((4, HEAD), lambda b, h, i: (0, 0)),
        pl.BlockSpec((1, w2), lambda b, h, i: (0, 0)),
    ]
    args = [p, p, kh, vt, q_norm.reshape(1, HEAD), lam_vecs, subln.reshape(1, w2)]
    if has_ctx:
        ck, cvt = ctx
        n_ctx = ck.shape[1]
        in_specs += [
            pl.BlockSpec((3, tq, HEAD), lambda b, h, i: (0, i, 0)),
            pl.BlockSpec((None, n_ctx, w2), lambda b, h, i: (b, 0, h)),
            pl.BlockSpec((None, w2, n_ctx), lambda b, h, i: (b, h, 0)),
        ]
        args += [rope_tab, ck, cvt]
    return pl.pallas_call(
        functools.partial(_diff_attn_kernel, has_ctx=has_ctx, lam_init=lam_init, tq=tq),
        grid=(n_batch, B_HEADS, nq),
        in_specs=in_specs,
        out_specs=pl.BlockSpec((tq, w2), lambda b, h, i: (b * nq + i, h)),
        out_shape=jax.ShapeDtypeStruct((n_batch * n_seq, D_MODEL), BF16),
        compiler_params=_cparams(3),
        name="diff_attention",
    )(*args)


def _gqa_kernel(*refs, has_ctx, tq):
    q_ref, g_ref, kh_ref, vt_ref, qn_ref = refs[:5]
    pos = 5
    if has_ctx:
        tab_ref, ck_ref, cvt_ref = refs[pos:pos + 3]
        pos += 3
    o_ref = refs[pos]

    qs = []
    for gi in range(C_GROUP):
        q = _rms(q_ref[:, gi * HEAD:(gi + 1) * HEAD], qn_ref[...])
        if has_ctx:
            q = _rope(q, tab_ref)
        qs.append(q.astype(BF16))
    q_all = jnp.concatenate(qs, axis=0)
    k_list = [kh_ref[...]] + ([ck_ref[...]] if has_ctx else [])
    vt_list = [vt_ref[...]] + ([cvt_ref[...]] if has_ctx else [])
    e_list, l = _softmax_cols(q_all, k_list)
    o_t = None
    for e, vt in zip(e_list, vt_list):
        part = jnp.dot(vt, e.astype(BF16), preferred_element_type=F32)
        o_t = part if o_t is None else o_t + part
    o = (o_t * (1.0 / l)).T
    for gi in range(C_GROUP):
        sl = slice(gi * HEAD, (gi + 1) * HEAD)
        o_ref[:, sl] = (o[gi * tq:(gi + 1) * tq] * _silu(g_ref[:, sl])).astype(BF16)


def _gqa_attention(p, kh, vt, q_norm, n_batch, n_seq, tq, rope_tab=None, ctx=None):
    has_ctx = ctx is not None
    wq = C_GROUP * HEAD
    nq = n_seq // tq
    g_col0 = (D_MODEL + 2 * C_KV_HEADS * HEAD) // wq
    in_specs = [
        pl.BlockSpec((tq, wq), lambda b, h, i: (b * nq + i, h)),
        pl.BlockSpec((tq, wq), lambda b, h, i: (b * nq + i, g_col0 + h)),
        pl.BlockSpec((n_seq, HEAD), lambda b, h, i: (b, h)),
        pl.BlockSpec((None, HEAD, n_seq), lambda b, h, i: (b, h, 0)),
        pl.BlockSpec((1, HEAD), lambda b, h, i: (0, 0)),
    ]
    args = [p, p, kh, vt, q_norm.reshape(1, HEAD)]
    if has_ctx:
        ck, cvt = ctx
        n_ctx = ck.shape[1]
        in_specs += [
            pl.BlockSpec((3, tq, HEAD), lambda b, h, i: (0, i, 0)),
            pl.BlockSpec((None, n_ctx, HEAD), lambda b, h, i: (b, 0, h)),
            pl.BlockSpec((None, HEAD, n_ctx), lambda b, h, i: (b, h, 0)),
        ]
        args += [rope_tab, ck, cvt]
    return pl.pallas_call(
        functools.partial(_gqa_kernel, has_ctx=has_ctx, tq=tq),
        grid=(n_batch, C_KV_HEADS, nq),
        in_specs=in_specs,
        out_specs=pl.BlockSpec((tq, wq), lambda b, h, i: (b * nq + i, h)),
        out_shape=jax.ShapeDtypeStruct((n_batch * n_seq, D_MODEL), BF16),
        compiler_params=_cparams(3),
        name="gqa_attention",
    )(*args)


def _forget_gate(z, lb):
    t = jnp.exp(-jnp.abs(z))
    r = 1.0 / (1.0 + t)
    tr = t * r
    pos = z >= 0.0
    sig = jnp.where(pos, r, tr)
    nsig = jnp.where(pos, tr, r)
    log_sig = jnp.minimum(z, 0.0) - jnp.log1p(t)
    one_m = 1.0 - lb
    logf = jnp.where(lb > 0.0, jnp.log(lb + one_m * sig), log_sig)
    return logf, one_m * nsig


def _block_first(x, b, rowmod):
    c = x.shape[0]
    if b == 1:
        return x
    if b >= 8:
        g = x.reshape(c // b, b, HEAD)
        return jnp.broadcast_to(g[:, 0:1, :], g.shape).reshape(c, HEAD)
    out = x
    for d in range(1, b):
        out = jnp.where(rowmod == d, pltpu.roll(x, d, 0), out)
    return out


def _block_last(x, b, rowmod):
    c = x.shape[0]
    if b == 1:
        return x
    if b >= 8:
        g = x.reshape(c // b, b, HEAD)
        return jnp.broadcast_to(g[:, b - 1:b, :], g.shape).reshape(c, HEAD)
    out = x
    for e in range(1, b):
        out = jnp.where(rowmod == b - 1 - e, pltpu.roll(x, c - e, 0), out)
    return out


def _scan_chunk(q, kd, logf, v, s_in, tri, level, rows, rev):
    c = q.shape[0]
    cum = jnp.dot(tri, logf, precision=lax.Precision.HIGHEST, preferred_element_type=F32)
    cum_ex = cum - logf
    v_b = v.astype(BF16)
    near, far = (_block_last, _block_first) if rev else (_block_first, _block_last)

    att = jnp.zeros((c, c), F32)
    b = c // 2
    while b >= 1:
        rowmod = rows % b if 1 < b < 8 else None
        qt = (q * jnp.exp(cum - near(cum_ex, b, rowmod))).astype(BF16)
        kt = (kd * jnp.exp(far(cum, b, rowmod) - cum)).astype(BF16)
        a = lax.dot_general(qt, kt, _NT, preferred_element_type=F32)
        att = jnp.where(level == int(math.log2(b)), a, att)
        b //= 2
    o = jnp.dot(att.astype(BF16), v_b, preferred_element_type=F32)
    o = o + jnp.sum(q * kd, axis=-1, keepdims=True) * v
    if s_in is not None:
        o = o + jnp.dot((q * jnp.exp(cum)).astype(BF16), s_in.astype(BF16), preferred_element_type=F32)

    tot = cum[0:1, :] if rev else cum[c - 1:c, :]
    kl = (kd * jnp.exp(tot - cum)).T.astype(BF16)
    s_out = jnp.dot(kl, v_b, preferred_element_type=F32)
    if s_in is not None:
        s_out = s_out + jnp.broadcast_to(jnp.exp(tot), (HEAD, HEAD)).T * s_in
    return o, s_out


def _hgrn_kernel(*refs, n_seq, n_heads, has_state):
    q_ref, zf_ref, zb_ref, v_ref, g_ref, lb_ref, on_ref = refs[:7]
    pos = 7
    s0_ref = None
    if has_state:
        s0_ref = refs[pos]
        pos += 1
    o_ref = refs[pos]
    pos += 1
    sout_ref = None
    if not has_state:
        sout_ref = refs[pos]
        pos += 1
    of_ref = refs[pos]

    c = SCAN_CHUNK
    n_chunks = n_seq // c
    row = lax.broadcasted_iota(jnp.int32, (c, c), 0)
    col = lax.broadcasted_iota(jnp.int32, (c, c), 1)
    x = row ^ col
    lv = jnp.full((c, c), -1, jnp.int32)
    for j in range(int(math.log2(c))):
        lv = lv + (x >= (1 << j)).astype(jnp.int32)
    level_f = jnp.where(col < row, lv, -1)
    level_b = jnp.where(col > row, lv, -1)
    tri_f = (col <= row).astype(F32)
    tri_b = (col >= row).astype(F32)
    rows = lax.broadcasted_iota(jnp.int32, (c, HEAD), 0)

    for h in range(n_heads):
        sl = slice(h * HEAD, (h + 1) * HEAD)
        lb_f = lb_ref[0:1, sl]
        lb_b = lb_ref[1:2, sl]

        def fwd_chunk(ci, s, sl=sl, lb_f=lb_f):
            r = pl.ds(pl.multiple_of(ci * c, c), c)
            logf, kd = _forget_gate(zf_ref[r, sl], lb_f)
            o, s_new = _scan_chunk(_silu(q_ref[r, sl]), kd, logf, v_ref[r, sl], s, tri_f, level_f, rows, False)
            of_ref[r, :] = o
            return s_new

        def bwd_chunk(ci, s, sl=sl, lb_b=lb_b):
            r = pl.ds(pl.multiple_of(ci * c, c), c)
            logf, kd = _forget_gate(zb_ref[r, sl], lb_b)
            o, s_new = _scan_chunk(_silu(q_ref[r, sl]), kd, logf, v_ref[r, sl], s, tri_b, level_b, rows, True)
            o = _rms(o + of_ref[r, :], on_ref[...]) * _silu(g_ref[r, sl])
            o_ref[r, sl] = o.astype(BF16)
            return s_new

        if has_state:
            s_f = lax.fori_loop(0, n_chunks, fwd_chunk, s0_ref[0, h])
            lax.fori_loop(0, n_chunks, lambda i, s: bwd_chunk(n_chunks - 1 - i, s), s0_ref[1, h])
        else:
            s_f = fwd_chunk(0, None)
            if n_chunks > 1:
                s_f = lax.fori_loop(1, n_chunks, fwd_chunk, s_f)
            s_b = bwd_chunk(n_chunks - 1, None)
            if n_chunks > 1:
                s_b = lax.fori_loop(1, n_chunks, lambda i, s: bwd_chunk(n_chunks - 1 - i, s), s_b)
            sout_ref[0, h] = s_f
            sout_ref[1, h] = s_b


def _hgrn_scan(p, lb, o_norm, n_batch, n_seq, heads_per_step, state=None):
    has_state = state is not None
    wb = heads_per_step * HEAD
    sec = D_MODEL // wb
    col = lambda s: (lambda b, h: (b, s * sec + h))
    in_specs = [pl.BlockSpec((n_seq, wb), col(s)) for s in range(5)]
    in_specs += [
        pl.BlockSpec((2, wb), lambda b, h: (0, h)),
        pl.BlockSpec((1, HEAD), lambda b, h: (0, 0)),
    ]
    args = [p] * 5 + [lb, o_norm.reshape(1, HEAD)]
    state_spec = pl.BlockSpec((None, 2, heads_per_step, HEAD, HEAD), lambda b, h: (b, 0, h, 0, 0))
    out_specs = [pl.BlockSpec((n_seq, wb), lambda b, h: (b, h))]
    out_shape = [jax.ShapeDtypeStruct((n_batch * n_seq, D_MODEL), BF16)]
    if has_state:
        in_specs.append(state_spec)
        args.append(state)
    else:
        out_specs.append(state_spec)
        out_shape.append(jax.ShapeDtypeStruct((n_batch, 2, A_HEADS, HEAD, HEAD), F32))
    res = pl.pallas_call(
        functools.partial(_hgrn_kernel, n_seq=n_seq, n_heads=heads_per_step, has_state=has_state),
        grid=(n_batch, sec),
        in_specs=in_specs,
        out_specs=out_specs,
        out_shape=out_shape,
        scratch_shapes=[pltpu.VMEM((n_seq, HEAD), F32)],
        compiler_params=_cparams(2),
        name="hgrn2_scan",
    )(*args)
    return (res[0], None) if has_state else (res[0], res[1])


def _diff_lambda_init(layer):
    return 0.8 - 0.6 * math.exp(-0.3 * layer)


def kernel(x_prompt, x_sample, state_a, cache_b_k, cache_b_v, cache_c_k, cache_c_v, c, c_ctx, norm_w, mod_w, mod_b, a_w_in, a_w_out, a_o_norm, a_lower_bound, b_w_in, b_w_out, b_q_norm, b_k_norm, b_lambda, b_subln, c_w_in, c_w_out, c_q_norm, c_k_norm):
    n_ctx_b, n_ctx_s = x_prompt.shape[:2]
    n_lat_b, n_lat_s = x_sample.shape[:2]
    past = cache_b_k.shape[2]
    tm = tn = 1024

    lb_all = jnp.cumsum(jax.nn.softmax(a_lower_bound.astype(F32), axis=0), axis=0)
    lb_all = lb_all - lb_all[0:1]

    cond = jnp.zeros((COND_ROWS, D_MODEL), F32).at[0].set(c_ctx).at[1:1 + n_lat_b].set(c)
    mods = _mod_rows(cond, mod_w, mod_b)
    rope_tab = _rope_tables(n_lat_s)

    lat_tiles = n_lat_s // tm
    groups = [
        dict(x=x_prompt.reshape(-1, D_MODEL), nb=n_ctx_b, ns=n_ctx_s, row=lambda i: 0, latent=False),
        dict(x=x_sample.reshape(-1, D_MODEL), nb=n_lat_b, ns=n_lat_s, row=lambda i: 1 + i // lat_tiles, latent=True),
    ]
    new_a, new_bk, new_bv, new_ck, new_cv = [], [], [], [], []

    for layer in range(DEPTH):
        kind, j = layer % N_MIXERS, layer // N_MIXERS
        w_in = (a_w_in, b_w_in, c_w_in)[kind][j].astype(BF16)
        w_out = (a_w_out, b_w_out, c_w_out)[kind][j].astype(BF16)
        for grp in groups:
            nb, ns, latent = grp["nb"], grp["ns"], grp["latent"]
            p = _in_proj(grp["x"], mods[layer], norm_w[layer], w_in, grp["row"], tm, tn)
            if kind == 0:
                if latent:
                    o, _ = _hgrn_scan(p, lb_all[layer], a_o_norm[j], nb, ns, 1, state=state_a[:, j])
                else:
                    o, s_fin = _hgrn_scan(p, lb_all[layer], a_o_norm[j], nb, ns, 4)
                    new_a.append(s_fin)
            elif kind == 1:
                tab = rope_tab if latent else None
                res = _kv_prep(p, D_MODEL, D_MODEL, 2 * D_MODEL, D_MODEL, b_k_norm[j], tab, nb, ns)
                ctx = None
                if latent:
                    ctx = (cache_b_k[:, j].reshape(nb, past, D_MODEL).astype(BF16),
                           jnp.swapaxes(cache_b_v[:, j].reshape(nb, past, D_MODEL), 1, 2).astype(BF16))
                else:
                    new_bk.append(res[2].reshape(nb, ns, B_HEADS, 2, HEAD))
                    new_bv.append(p[:, 2 * D_MODEL:3 * D_MODEL].reshape(nb, ns, B_HEADS, 2 * HEAD))
                o = _diff_attention(p, res[0], res[1], b_q_norm[j], b_lambda[j], b_subln[j],
                                    _diff_lambda_init(layer), nb, ns, 256, rope_tab=tab, ctx=ctx)
            else:
                kvw = C_KV_HEADS * HEAD
                tab = rope_tab if latent else None
                res = _kv_prep(p, D_MODEL, kvw, D_MODEL + kvw, kvw, c_k_norm[j], tab, nb, ns)
                ctx = None
                if latent:
                    ctx = (cache_c_k[:, j].reshape(nb, past, kvw).astype(BF16),
                           jnp.swapaxes(cache_c_v[:, j].reshape(nb, past, kvw), 1, 2).astype(BF16))
                else:
                    new_ck.append(res[2].reshape(nb, ns, C_KV_HEADS, HEAD))
                    new_cv.append(p[:, D_MODEL + kvw:D_MODEL + 2 * kvw].reshape(nb, ns, C_KV_HEADS, HEAD))
                o = _gqa_attention(p, res[0], res[1], c_q_norm[j], nb, ns, 256 if not latent else 128,
                                   rope_tab=tab, ctx=ctx)
            grp["x"] = _out_proj(o, w_out, grp["x"], mods[layer], grp["row"], tm, tn)

    y_prompt = groups[0]["x"].reshape(x_prompt.shape)
    y_sample = groups[1]["x"].reshape(x_sample.shape)
    return (y_prompt, y_sample, jnp.stack(new_a, axis=1), jnp.stack(new_bk, axis=1), jnp.stack(new_bv, axis=1),
            jnp.stack(new_ck, axis=1), jnp.stack(new_cv, axis=1))
```

```python
import functools
import math

import jax
import jax.numpy as jnp
from jax import lax
from jax.experimental import pallas as pl
from jax.experimental.pallas import tpu as pltpu

F32 = jnp.float32
BF16 = jnp.bfloat16

D_MODEL = 2048
DEPTH = 4
GRID_W = 64
N_MIXERS = 3
EPS = 1e-6
ROPE_THETA = 10000.0
HEAD = 128
A_HEADS = D_MODEL // HEAD
B_HEADS = D_MODEL // (2 * HEAD)
C_HEADS = D_MODEL // HEAD
C_KV_HEADS = C_HEADS // 4
C_GROUP = C_HEADS // C_KV_HEADS
COND_ROWS = 8

V7X_VMEM_LIMIT_BYTES = 56 * 1024 * 1024
SCAN_CHUNK = 256
SCAN_HALF = SCAN_CHUNK // 2
SCAN_LEVELS = SCAN_CHUNK.bit_length() - 1
LOG2_E = math.log2(math.e)


def _cparams(n_axes):
    return pltpu.CompilerParams(
        dimension_semantics=("arbitrary",) * n_axes,
        vmem_limit_bytes=V7X_VMEM_LIMIT_BYTES,
    )


def _sigmoid(x):
    return 1.0 / (1.0 + jnp.exp(-x))


def _silu(x):
    return x * _sigmoid(x)


def _rms(x, w):
    return x * lax.rsqrt(jnp.mean(x * x, axis=-1, keepdims=True) + EPS) * w


def _mod_kernel(cond_ref, w_ref, b_ref, o_ref):
    s = _silu(cond_ref[...]).astype(BF16)
    o_ref[...] = jnp.dot(s, w_ref[...].astype(BF16), preferred_element_type=F32) + b_ref[...]


def _mod_rows(cond, mod_w, mod_b):
    d3 = 3 * D_MODEL
    tn = 768
    out = pl.pallas_call(
        _mod_kernel,
        grid=(DEPTH, d3 // tn),
        in_specs=[
            pl.BlockSpec((COND_ROWS, D_MODEL), lambda l, j: (0, 0)),
            pl.BlockSpec((None, D_MODEL, tn), lambda l, j: (l, 0, j)),
            pl.BlockSpec((None, 1, tn), lambda l, j: (l, 0, j)),
        ],
        out_specs=pl.BlockSpec((None, COND_ROWS, tn), lambda l, j: (l, 0, j)),
        out_shape=jax.ShapeDtypeStruct((DEPTH, COND_ROWS, d3), F32),
        compiler_params=_cparams(2),
        name="adaln_rows",
    )(cond, mod_w, mod_b.reshape(DEPTH, 1, d3))
    return out.reshape(DEPTH, COND_ROWS, 3, D_MODEL)


def _inproj_kernel(x_ref, mod_ref, nw_ref, w_ref, o_ref, h_ref):
    @pl.when(pl.program_id(1) == 0)
    def _():
        xn = _rms(x_ref[...], nw_ref[...])
        h_ref[...] = (xn * (1.0 + mod_ref[1:2, :]) + mod_ref[0:1, :]).astype(BF16)

    o_ref[...] = jnp.dot(h_ref[...], w_ref[...], preferred_element_type=F32).astype(o_ref.dtype)


def _in_proj(x, mod, norm_w, w, row_of_tile, tm, tn):
    t, n = x.shape[0], w.shape[1]
    return pl.pallas_call(
        _inproj_kernel,
        grid=(t // tm, n // tn),
        in_specs=[
            pl.BlockSpec((tm, D_MODEL), lambda i, j: (i, 0)),
            pl.BlockSpec((None, 3, D_MODEL), lambda i, j: (row_of_tile(i), 0, 0)),
            pl.BlockSpec((1, D_MODEL), lambda i, j: (0, 0)),
            pl.BlockSpec((D_MODEL, tn), lambda i, j: (0, j)),
        ],
        out_specs=pl.BlockSpec((tm, tn), lambda i, j: (i, j)),
        out_shape=jax.ShapeDtypeStruct((t, n), F32),
        scratch_shapes=[pltpu.VMEM((tm, D_MODEL), BF16)],
        compiler_params=_cparams(2),
        name="in_proj",
    )(x, mod, norm_w.reshape(1, D_MODEL), w)


def _outproj_kernel(o_ref, w_ref, x_ref, mod_ref, y_ref):
    acc = jnp.dot(o_ref[...], w_ref[...], preferred_element_type=F32)
    y_ref[...] = x_ref[...] + mod_ref[2:3, :] * acc


def _out_proj(o, w, x, mod, row_of_tile, tm, tn):
    t, kdim = o.shape
    return pl.pallas_call(
        _outproj_kernel,
        grid=(t // tm, D_MODEL // tn),
        in_specs=[
            pl.BlockSpec((tm, kdim), lambda i, j: (i, 0)),
            pl.BlockSpec((kdim, tn), lambda i, j: (0, j)),
            pl.BlockSpec((tm, tn), lambda i, j: (i, j)),
            pl.BlockSpec((None, 3, tn), lambda i, j: (row_of_tile(i), 0, j)),
        ],
        out_specs=pl.BlockSpec((tm, tn), lambda i, j: (i, j)),
        out_shape=jax.ShapeDtypeStruct((t, D_MODEL), F32),
        compiler_params=_cparams(2),
        name="out_proj",
    )(o, w, x, mod)


def _rope_tables(n):
    pos = jnp.arange(n)
    quarter = HEAD // 4
    inv_freq = ROPE_THETA ** (-jnp.arange(quarter, dtype=F32) / quarter)

    def axis_angles(p):
        ang = p.astype(F32)[:, None] * inv_freq[None, :]
        return jnp.concatenate([ang, ang], axis=-1)

    ang = jnp.concatenate([axis_angles(pos // GRID_W), axis_angles(pos % GRID_W)], axis=-1)
    cos, sin = jnp.cos(ang), jnp.sin(ang)
    first = (jnp.arange(HEAD) % (2 * quarter)) < quarter
    return jnp.stack([cos, jnp.where(first, -sin, 0.0), jnp.where(first, 0.0, sin)])


def _rope(x, tab_ref):
    up = pltpu.roll(x, HEAD - HEAD // 4, 1)
    down = pltpu.roll(x, HEAD // 4, 1)
    return x * tab_ref[0] + up * tab_ref[1] + down * tab_ref[2]


def _kvprep_kernel(*refs, n_kheads, rope, emit_norm):
    k_ref, v_ref, kn_ref = refs[:3]
    pos = 3
    tab_ref = None
    if rope:
        tab_ref = refs[pos]
        pos += 1
    kh_ref, vt_ref = refs[pos], refs[pos + 1]
    kn_out = refs[pos + 2] if emit_norm else None
    for h in range(n_kheads):
        sl = slice(h * HEAD, (h + 1) * HEAD)
        kn = _rms(k_ref[:, sl], kn_ref[...])
        if emit_norm:
            kn_out[:, sl] = kn
        if rope:
            kn = _rope(kn, tab_ref)
        kh_ref[:, sl] = kn.astype(BF16)
    vt_ref[...] = v_ref[...].T.astype(BF16)


def _kv_prep(p, k_col, k_w, v_col, v_w, k_norm, rope_tab, n_batch, n_seq):
    t = p.shape[0]
    rope = rope_tab is not None
    emit_norm = not rope
    tr = min(512, n_seq)
    per_seq = n_seq // tr
    in_specs = [
        pl.BlockSpec((tr, k_w), lambda i: (i, k_col // k_w)),
        pl.BlockSpec((tr, v_w), lambda i: (i, v_col // v_w)),
        pl.BlockSpec((1, HEAD), lambda i: (0, 0)),
    ]
    args = [p, p, k_norm.reshape(1, HEAD)]
    if rope:
        in_specs.append(pl.BlockSpec((3, tr, HEAD), lambda i: (0, i % per_seq, 0)))
        args.append(rope_tab)
    out_specs = [
        pl.BlockSpec((tr, k_w), lambda i: (i, 0)),
        pl.BlockSpec((None, v_w, tr), lambda i: (i // per_seq, 0, i % per_seq)),
    ]
    out_shape = [jax.ShapeDtypeStruct((t, k_w), BF16), jax.ShapeDtypeStruct((n_batch, v_w, n_seq), BF16)]
    if emit_norm:
        out_specs.append(pl.BlockSpec((tr, k_w), lambda i: (i, 0)))
        out_shape.append(jax.ShapeDtypeStruct((t, k_w), F32))
    return pl.pallas_call(
        functools.partial(_kvprep_kernel, n_kheads=k_w // HEAD, rope=rope, emit_norm=emit_norm),
        grid=(t // tr,),
        in_specs=in_specs,
        out_specs=out_specs,
        out_shape=out_shape,
        compiler_params=_cparams(1),
        name="kv_prep",
    )(*args)


_NT = (((1,), (1,)), ((), ()))


def _softmax_cols(q, k_list):
    c = (HEAD ** -0.5) * math.log2(math.e)
    s_list = [lax.dot_general(k, q, _NT, preferred_element_type=F32) for k in k_list]
    m = s_list[0].max(axis=0, keepdims=True)
    for s in s_list[1:]:
        m = jnp.maximum(m, s.max(axis=0, keepdims=True))
    e_list = [jnp.exp2((s - m) * c) for s in s_list]
    l = e_list[0].sum(axis=0, keepdims=True)
    for e in e_list[1:]:
        l = l + e.sum(axis=0, keepdims=True)
    return e_list, l


def _diff_attn_kernel(*refs, has_ctx, lam_init, tq):
    q_ref, g_ref, kh_ref, vt_ref, qn_ref, lamv_ref, sub_ref = refs[:7]
    pos = 7
    if has_ctx:
        tab_ref, ck_ref, cvt_ref = refs[pos:pos + 3]
        pos += 3
    o_ref = refs[pos]

    lv = lamv_ref[...]
    lam = (jnp.exp(jnp.sum(lv[0:1] * lv[1:2], keepdims=True))
           - jnp.exp(jnp.sum(lv[2:3] * lv[3:4], keepdims=True)) + lam_init)

    zero = jnp.zeros((tq, HEAD), BF16)
    rows = []
    for comp in range(2):
        q = _rms(q_ref[:, comp * HEAD:(comp + 1) * HEAD], qn_ref[...])
        if has_ctx:
            q = _rope(q, tab_ref)
        q = q.astype(BF16)
        rows.append(jnp.concatenate([q, zero] if comp == 0 else [zero, q], axis=1))
    q_bd = jnp.concatenate(rows, axis=0)
    k_list = [kh_ref[...]] + ([ck_ref[...]] if has_ctx else [])
    vt_list = [vt_ref[...]] + ([cvt_ref[...]] if has_ctx else [])
    e_list, l = _softmax_cols(q_bd, k_list)
    r1 = 1.0 / l[:, :tq]
    r2 = lam / l[:, tq:]
    o_t = None
    for e, vt in zip(e_list, vt_list):
        w = (e[:, :tq] * r1 - e[:, tq:] * r2).astype(BF16)
        part = jnp.dot(vt, w, preferred_element_type=F32)
        o_t = part if o_t is None else o_t + part
    o = _rms(o_t.T, sub_ref[...]) * (1.0 - lam_init)
    o_ref[...] = (o * _silu(g_ref[...])).astype(BF16)


def _diff_attention(p, kh, vt, q_norm, lam_vecs, subln, lam_init, n_batch, n_seq, tq, rope_tab=None, ctx=None):
    has_ctx = ctx is not None
    w2 = 2 * HEAD
    nq = n_seq // tq
    g_col0 = 3 * D_MODEL // w2
    in_specs = [
        pl.BlockSpec((tq, w2), lambda b, h, i: (b * nq + i, h)),
        pl.BlockSpec((tq, w2), lambda b, h, i: (b * nq + i, g_col0 + h)),
        pl.BlockSpec((n_seq, w2), lambda b, h, i: (b, h)),
        pl.BlockSpec((None, w2, n_seq), lambda b, h, i: (b, h, 0)),
        pl.BlockSpec((1, HEAD), lambda b, h, i: (0, 0)),
        pl.BlockSpec((4, HEAD), lambda b, h, i: (0, 0)),
        pl.BlockSpec((1, w2), lambda b, h, i: (0, 0)),
    ]
    args = [p, p, kh, vt, q_norm.reshape(1, HEAD), lam_vecs, subln.reshape(1, w2)]
    if has_ctx:
        ck, cvt = ctx
        n_ctx = ck.shape[1]
        in_specs += [
            pl.BlockSpec((3, tq, HEAD), lambda b, h, i: (0, i, 0)),
            pl.BlockSpec((None, n_ctx, w2), lambda b, h, i: (b, 0, h)),
            pl.BlockSpec((None, w2, n_ctx), lambda b, h, i: (b, h, 0)),
        ]
        args += [rope_tab, ck, cvt]
    return pl.pallas_call(
        functools.partial(_diff_attn_kernel, has_ctx=has_ctx, lam_init=lam_init, tq=tq),
        grid=(n_batch, B_HEADS, nq),
        in_specs=in_specs,
        out_specs=pl.BlockSpec((tq, w2), lambda b, h, i: (b * nq + i, h)),
        out_shape=jax.ShapeDtypeStruct((n_batch * n_seq, D_MODEL), BF16),
        compiler_params=_cparams(3),
        name="diff_attention",
    )(*args)


def _gqa_kernel(*refs, has_ctx, tq):
    q_ref, g_ref, kh_ref, vt_ref, qn_ref = refs[:5]
    pos = 5
    if has_ctx:
        tab_ref, ck_ref, cvt_ref = refs[pos:pos + 3]
        pos += 3
    o_ref = refs[pos]

    qs = []
    for gi in range(C_GROUP):
        q = _rms(q_ref[:, gi * HEAD:(gi + 1) * HEAD], qn_ref[...])
        if has_ctx:
            q = _rope(q, tab_ref)
        qs.append(q.astype(BF16))
    q_all = jnp.concatenate(qs, axis=0)
    k_list = [kh_ref[...]] + ([ck_ref[...]] if has_ctx else [])
    vt_list = [vt_ref[...]] + ([cvt_ref[...]] if has_ctx else [])
    e_list, l = _softmax_cols(q_all, k_list)
    o_t = None
    for e, vt in zip(e_list, vt_list):
        part = jnp.dot(vt, e.astype(BF16), preferred_element_type=F32)
        o_t = part if o_t is None else o_t + part
    o = (o_t * (1.0 / l)).T
    for gi in range(C_GROUP):
        sl = slice(gi * HEAD, (gi + 1) * HEAD)
        o_ref[:, sl] = (o[gi * tq:(gi + 1) * tq] * _silu(g_ref[:, sl])).astype(BF16)


def _gqa_attention(p, kh, vt, q_norm, n_batch, n_seq, tq, rope_tab=None, ctx=None):
    has_ctx = ctx is not None
    wq = C_GROUP * HEAD
    nq = n_seq // tq
    g_col0 = (D_MODEL + 2 * C_KV_HEADS * HEAD) // wq
    in_specs = [
        pl.BlockSpec((tq, wq), lambda b, h, i: (b * nq + i, h)),
        pl.BlockSpec((tq, wq), lambda b, h, i: (b * nq + i, g_col0 + h)),
        pl.BlockSpec((n_seq, HEAD), lambda b, h, i: (b, h)),
        pl.BlockSpec((None, HEAD, n_seq), lambda b, h, i: (b, h, 0)),
        pl.BlockSpec((1, HEAD), lambda b, h, i: (0, 0)),
    ]
    args = [p, p, kh, vt, q_norm.reshape(1, HEAD)]
    if has_ctx:
        ck, cvt = ctx
        n_ctx = ck.shape[1]
        in_specs += [
            pl.BlockSpec((3, tq, HEAD), lambda b, h, i: (0, i, 0)),
            pl.BlockSpec((None, n_ctx, HEAD), lambda b, h, i: (b, 0, h)),
            pl.BlockSpec((None, HEAD, n_ctx), lambda b, h, i: (b, h, 0)),
        ]
        args += [rope_tab, ck, cvt]
    return pl.pallas_call(
        functools.partial(_gqa_kernel, has_ctx=has_ctx, tq=tq),
        grid=(n_batch, C_KV_HEADS, nq),
        in_specs=in_specs,
        out_specs=pl.BlockSpec((tq, wq), lambda b, h, i: (b * nq + i, h)),
        out_shape=jax.ShapeDtypeStruct((n_batch * n_seq, D_MODEL), BF16),
        compiler_params=_cparams(3),
        name="gqa_attention",
    )(*args)


def _forget_gate(z, lb):
    t = jnp.exp(-jnp.abs(z))
    r = 1.0 / (1.0 + t)
    tr = t * r
    pos = z >= 0.0
    sig = jnp.where(pos, r, tr)
    nsig = jnp.where(pos, tr, r)
    log_sig = jnp.minimum(z, 0.0) - jnp.log1p(t)
    one_m = 1.0 - lb
    logf = jnp.where(lb > 0.0, jnp.log(lb + one_m * sig), log_sig)
    return logf, one_m * nsig


def _cumsum_rows(tri, x):
    hi = x.astype(BF16)
    mid = (x - hi.astype(F32)).astype(BF16)
    r = jnp.dot(tri, jnp.concatenate([hi, mid], axis=1), preferred_element_type=F32)
    return r[:, :HEAD] + r[:, HEAD:]


def _pair_rows(ref, b, odd):
    off = b if odd else 0
    if b >= 8:
        parts = [ref[pl.ds(i * 2 * b + off, b), :] for i in range(SCAN_HALF // b)]
    else:
        parts = [ref[pl.ds(off + r, SCAN_HALF // b, stride=2 * b), :] for r in range(b)]
    return parts[0] if len(parts) == 1 else jnp.concatenate(parts, axis=0)


def _pair_bcast(ref, b, row):
    if b >= 8:
        parts = [jnp.broadcast_to(ref[pl.ds(i * 2 * b + row, 1), :], (b, HEAD)) for i in range(SCAN_HALF // b)]
    else:
        parts = [ref[pl.ds(row, SCAN_HALF // b, stride=2 * b), :]] * b
    return parts[0] if len(parts) == 1 else jnp.concatenate(parts, axis=0)


def _pair_store(ref, b, odd, val):
    off = b if odd else 0
    if b >= 8:
        for i in range(SCAN_HALF // b):
            ref[pl.ds(i * 2 * b + off, b), :] = val[i * b:(i + 1) * b]
    else:
        n = SCAN_HALF // b
        for r in range(b):
            ref[pl.ds(off + r, n, stride=2 * b), :] = val[r * n:(r + 1) * n]


def _hgrn_kernel(*refs, n_chunks, has_state):
    q_ref, zf_ref, zb_ref, v_ref, g_ref, lb_ref, on_ref = refs[:7]
    pos = 7
    s0_ref = None
    if has_state:
        s0_ref = refs[pos]
        pos += 1
    o_ref = refs[pos]
    pos += 1
    sout_ref = None
    if not has_state:
        sout_ref = refs[pos]
        pos += 1
    tri_ref, mask_ref, sq, sv, skf, scf, sxf, skb, scb, sxb = refs[pos:pos + 10]
    pos += 10
    lvl_out = refs[pos:pos + SCAN_LEVELS]
    pos += SCAN_LEVELS
    if has_state:
        oall, qb_all, klb_all, decb_all = refs[pos:pos + 4]

    c = SCAN_CHUNK
    row = lax.broadcasted_iota(jnp.int32, (c, c), 0)
    col = lax.broadcasted_iota(jnp.int32, (c, c), 1)
    tri_ref[0] = (col <= row).astype(BF16)
    tri_ref[1] = (col >= row).astype(BF16)
    prow = lax.broadcasted_iota(jnp.int32, (SCAN_HALF, SCAN_HALF), 0)
    pcol = lax.broadcasted_iota(jnp.int32, (SCAN_HALF, SCAN_HALF), 1)
    for j in range(SCAN_LEVELS):
        b = 1 << j
        if b >= 8:
            same = (prow // b) == (pcol // b)
        else:
            same = (prow % (SCAN_HALF // b)) == (pcol % (SCAN_HALF // b))
        mask_ref[j] = same.astype(F32)

    lb_f = lb_ref[0:1, :]
    lb_b = lb_ref[1:2, :]

    def intra(r):
        q = _silu(q_ref[r, :])
        v = v_ref[r, :]
        lf, kf = _forget_gate(zf_ref[r, :], lb_f)
        lbk, kb = _forget_gate(zb_ref[r, :], lb_b)
        lf = lf * LOG2_E
        lbk = lbk * LOG2_E
        cf = _cumsum_rows(tri_ref[0], lf)
        cb = _cumsum_rows(tri_ref[1], lbk)
        sq[...] = q
        sv[...] = v
        skf[...] = kf
        scf[...] = cf
        sxf[...] = cf - lf
        skb[...] = kb
        scb[...] = cb
        sxb[...] = cb - lbk
        scores = []
        for j in range(SCAN_LEVELS):
            b = 1 << j
            mask = mask_ref[j]
            qt = _pair_rows(sq, b, True) * jnp.exp2(_pair_rows(scf, b, True) - _pair_bcast(sxf, b, b))
            kt = _pair_rows(skf, b, False) * jnp.exp2(_pair_bcast(scf, b, b - 1) - _pair_rows(scf, b, False))
            z = lax.dot_general(qt.astype(BF16), kt.astype(BF16), _NT, preferred_element_type=F32) * mask
            scores.append((j, True, z.astype(BF16)))
            qt = _pair_rows(sq, b, False) * jnp.exp2(_pair_rows(scb, b, False) - _pair_bcast(sxb, b, b - 1))
            kt = _pair_rows(skb, b, True) * jnp.exp2(_pair_bcast(scb, b, b) - _pair_rows(scb, b, True))
            z = lax.dot_general(qt.astype(BF16), kt.astype(BF16), _NT, preferred_element_type=F32) * mask
            scores.append((j, False, z.astype(BF16)))
        for j, q_odd, z in scores:
            b = 1 << j
            _pair_store(lvl_out[j], b, q_odd, jnp.dot(z, _pair_rows(sv, b, not q_odd).astype(BF16),
                                                      preferred_element_type=F32))
        o = jnp.sum(q * (kf + kb), axis=-1, keepdims=True) * v
        for j in range(SCAN_LEVELS):
            o = o + lvl_out[j][...]
        tot_f = cf[c - 1:c, :]
        tot_b = cb[0:1, :]
        klf = kf * jnp.exp2(tot_f - cf)
        klb = kb * jnp.exp2(tot_b - cb)
        return o, q, v, cf, cb, tot_f, tot_b, klf, klb

    def finish(r, o):
        o = _rms(o, on_ref[...]) * _silu(g_ref[r, :])
        o_ref[r, :] = o.astype(BF16)

    def rows_of(ci):
        return pl.ds(pl.multiple_of(ci * c, c), c)

    if not has_state:
        def seq(ci, carry):
            r = rows_of(ci)
            o, q, v, cf, cb, tot_f, tot_b, klf, klb = intra(r)
            v_b = v.astype(BF16)
            sout_ref[ci, 0] = jnp.dot(klf.T.astype(BF16), v_b, preferred_element_type=F32)
            sout_ref[ci, 1] = jnp.dot(klb.T.astype(BF16), v_b, preferred_element_type=F32)
            finish(r, o)
            return carry

        lax.fori_loop(0, n_chunks, seq, 0)
        return

    def row_to_col(x):
        return jnp.broadcast_to(x, (HEAD, HEAD)).T

    def fwd(ci, s):
        r = rows_of(ci)
        o, q, v, cf, cb, tot_f, tot_b, klf, klb = intra(r)
        oall[r, :] = o + jnp.dot((q * jnp.exp2(cf)).astype(BF16), s.astype(BF16), preferred_element_type=F32)
        qb_all[r, :] = (q * jnp.exp2(cb)).astype(BF16)
        klb_all[r, :] = klb
        decb_all[pl.ds(pl.multiple_of(ci * 8, 8), 1), :] = jnp.exp2(tot_b)
        return row_to_col(jnp.exp2(tot_f)) * s + jnp.dot(klf.T.astype(BF16), v.astype(BF16),
                                                         preferred_element_type=F32)

    def bwd(i, s):
        ci = n_chunks - 1 - i
        r = rows_of(ci)
        finish(r, oall[r, :] + jnp.dot(qb_all[r, :], s.astype(BF16), preferred_element_type=F32))
        dec = decb_all[pl.ds(pl.multiple_of(ci * 8, 8), 1), :]
        return row_to_col(dec) * s + jnp.dot(klb_all[r, :].T.astype(BF16), v_ref[r, :].astype(BF16),
                                             preferred_element_type=F32)

    lax.fori_loop(0, n_chunks, fwd, s0_ref[0])
    lax.fori_loop(0, n_chunks, bwd, s0_ref[1])


def _hgrn_scan(p, lb, o_norm, n_batch, n_seq, state=None, seqs_per_step=8):
    has_state = state is not None
    c = SCAN_CHUNK
    if has_state:
        rows, n_steps = n_seq, n_batch
    else:
        assert n_seq == c and n_batch % seqs_per_step == 0
        rows, n_steps = seqs_per_step * c, n_batch // seqs_per_step
    n_chunks = rows // c
    col = lambda s: (lambda b, h: (b, s * A_HEADS + h))
    in_specs = [pl.BlockSpec((rows, HEAD), col(s)) for s in range(5)]
    in_specs += [
        pl.BlockSpec((2, HEAD), lambda b, h: (0, h)),
        pl.BlockSpec((1, HEAD), lambda b, h: (0, 0)),
    ]
    args = [p] * 5 + [lb, o_norm.reshape(1, HEAD)]
    out_specs = [pl.BlockSpec((rows, HEAD), lambda b, h: (b, h))]
    out_shape = [jax.ShapeDtypeStruct((n_batch * n_seq, D_MODEL), BF16)]
    scratch = [pltpu.VMEM((2, c, c), BF16), pltpu.VMEM((SCAN_LEVELS, SCAN_HALF, SCAN_HALF), F32)]
    scratch += [pltpu.VMEM((c, HEAD), F32)] * (8 + SCAN_LEVELS)
    if has_state:
        in_specs.append(pl.BlockSpec((None, 2, None, HEAD, HEAD), lambda b, h: (b, 0, h, 0, 0)))
        args.append(state)
        scratch += [pltpu.VMEM((rows, HEAD), F32), pltpu.VMEM((rows, HEAD), BF16),
                    pltpu.VMEM((rows, HEAD), F32), pltpu.VMEM((n_chunks * 8, HEAD), F32)]
    else:
        out_specs.append(pl.BlockSpec((n_chunks, 2, None, HEAD, HEAD), lambda b, h: (b, 0, h, 0, 0)))
        out_shape.append(jax.ShapeDtypeStruct((n_batch, 2, A_HEADS, HEAD, HEAD), F32))
    res = pl.pallas_call(
        functools.partial(_hgrn_kernel, n_chunks=n_chunks, has_state=has_state),
        grid=(n_steps, A_HEADS),
        in_specs=in_specs,
        out_specs=out_specs,
        out_shape=out_shape,
        scratch_shapes=scratch,
        compiler_params=_cparams(2),
        name="hgrn2_scan",
    )(*args)
    return (res[0], None) if has_state else (res[0], res[1])


def _diff_lambda_init(layer):
    return 0.8 - 0.6 * math.exp(-0.3 * layer)


def kernel(x_prompt, x_sample, state_a, cache_b_k, cache_b_v, cache_c_k, cache_c_v, c, c_ctx, norm_w, mod_w, mod_b, a_w_in, a_w_out, a_o_norm, a_lower_bound, b_w_in, b_w_out, b_q_norm, b_k_norm, b_lambda, b_subln, c_w_in, c_w_out, c_q_norm, c_k_norm):
    n_ctx_b, n_ctx_s = x_prompt.shape[:2]
    n_lat_b, n_lat_s = x_sample.shape[:2]
    past = cache_b_k.shape[2]
    tm = tn = 1024

    lb_all = jnp.cumsum(jax.nn.softmax(a_lower_bound.astype(F32), axis=0), axis=0)
    lb_all = lb_all - lb_all[0:1]

    cond = jnp.zeros((COND_ROWS, D_MODEL), F32).at[0].set(c_ctx).at[1:1 + n_lat_b].set(c)
    mods = _mod_rows(cond, mod_w, mod_b)
    rope_tab = _rope_tables(n_lat_s)

    lat_tiles = n_lat_s // tm
    groups = [
        dict(x=x_prompt.reshape(-1, D_MODEL), nb=n_ctx_b, ns=n_ctx_s, row=lambda i: 0, latent=False),
        dict(x=x_sample.reshape(-1, D_MODEL), nb=n_lat_b, ns=n_lat_s, row=lambda i: 1 + i // lat_tiles, latent=True),
    ]
    new_a, new_bk, new_bv, new_ck, new_cv = [], [], [], [], []

    for layer in range(DEPTH):
        kind, j = layer % N_MIXERS, layer // N_MIXERS
        w_in = (a_w_in, b_w_in, c_w_in)[kind][j].astype(BF16)
        w_out = (a_w_out, b_w_out, c_w_out)[kind][j].astype(BF16)
        for grp in groups:
            nb, ns, latent = grp["nb"], grp["ns"], grp["latent"]
            p = _in_proj(grp["x"], mods[layer], norm_w[layer], w_in, grp["row"], tm, tn)
            if kind == 0:
                if latent:
                    o, _ = _hgrn_scan(p, lb_all[layer], a_o_norm[j], nb, ns, state=state_a[:, j])
                else:
                    o, s_fin = _hgrn_scan(p, lb_all[layer], a_o_norm[j], nb, ns)
                    new_a.append(s_fin)
            elif kind == 1:
                tab = rope_tab if latent else None
                res = _kv_prep(p, D_MODEL, D_MODEL, 2 * D_MODEL, D_MODEL, b_k_norm[j], tab, nb, ns)
                ctx = None
                if latent:
                    ctx = (cache_b_k[:, j].reshape(nb, past, D_MODEL).astype(BF16),
                           jnp.swapaxes(cache_b_v[:, j].reshape(nb, past, D_MODEL), 1, 2).astype(BF16))
                else:
                    new_bk.append(res[2].reshape(nb, ns, B_HEADS, 2, HEAD))
                    new_bv.append(p[:, 2 * D_MODEL:3 * D_MODEL].reshape(nb, ns, B_HEADS, 2 * HEAD))
                o = _diff_attention(p, res[0], res[1], b_q_norm[j], b_lambda[j], b_subln[j],
                                    _diff_lambda_init(layer), nb, ns, 256, rope_tab=tab, ctx=ctx)
            else:
                kvw = C_KV_HEADS * HEAD
                tab = rope_tab if latent else None
                res = _kv_prep(p, D_MODEL, kvw, D_MODEL + kvw, kvw, c_k_norm[j], tab, nb, ns)
                ctx = None
                if latent:
                    ctx = (cache_c_k[:, j].reshape(nb, past, kvw).astype(BF16),
                           jnp.swapaxes(cache_c_v[:, j].reshape(nb, past, kvw), 1, 2).astype(BF16))
                else:
                    new_ck.append(res[2].reshape(nb, ns, C_KV_HEADS, HEAD))
                    new_cv.append(p[:, D_MODEL + kvw:D_MODEL + 2 * kvw].reshape(nb, ns, C_KV_HEADS, HEAD))
                o = _gqa_attention(p, res[0], res[1], c_q_norm[j], nb, ns, 256 if not latent else 128,
                                   rope_tab=tab, ctx=ctx)
            grp["x"] = _out_proj(o, w_out, grp["x"], mods[layer], grp["row"], tm, tn)

    y_prompt = groups[0]["x"].reshape(x_prompt.shape)
    y_sample = groups[1]["x"].reshape(x_sample.shape)
    return (y_prompt, y_sample, jnp.stack(new_a, axis=1), jnp.stack(new_bk, axis=1), jnp.stack(new_bv, axis=1),
            jnp.stack(new_ck, axis=1), jnp.stack(new_cv, axis=1))
```

```python
import functools
import math

import jax
import jax.numpy as jnp
from jax import lax
from jax.experimental import pallas as pl
from jax.experimental.pallas import tpu as pltpu

F32 = jnp.float32
BF16 = jnp.bfloat16

D_MODEL = 2048
DEPTH = 4
GRID_W = 64
N_MIXERS = 3
EPS = 1e-6
ROPE_THETA = 10000.0
HEAD = 128
A_HEADS = D_MODEL // HEAD
B_HEADS = D_MODEL // (2 * HEAD)
C_HEADS = D_MODEL // HEAD
C_KV_HEADS = C_HEADS // 4
C_GROUP = C_HEADS // C_KV_HEADS
COND_ROWS = 8

V7X_VMEM_LIMIT_BYTES = 56 * 1024 * 1024
SCAN_CHUNK = 256
SCAN_HALF = SCAN_CHUNK // 2
SCAN_LEVELS = SCAN_CHUNK.bit_length() - 1
LOG2_E = math.log2(math.e)
ATTN_KEY_CHUNK = 1024
ATTN_SCALE_LOG2 = (HEAD ** -0.5) * LOG2_E
BF16_SUBLANES = 16


def _cparams(n_axes):
    return pltpu.CompilerParams(
        dimension_semantics=("arbitrary",) * n_axes,
        vmem_limit_bytes=V7X_VMEM_LIMIT_BYTES,
    )


def _sigmoid(x):
    return 1.0 / (1.0 + jnp.exp(-x))


def _silu(x):
    return x * _sigmoid(x)


def _rms(x, w):
    return x * lax.rsqrt(jnp.mean(x * x, axis=-1, keepdims=True) + EPS) * w


def _mod_kernel(cond_ref, w_ref, b_ref, o_ref):
    s = _silu(cond_ref[...]).astype(BF16)
    o_ref[...] = jnp.dot(s, w_ref[...].astype(BF16), preferred_element_type=F32) + b_ref[...]


def _mod_rows(cond, mod_w, mod_b):
    d3 = 3 * D_MODEL
    tn = 768
    out = pl.pallas_call(
        _mod_kernel,
        grid=(DEPTH, d3 // tn),
        in_specs=[
            pl.BlockSpec((COND_ROWS, D_MODEL), lambda l, j: (0, 0)),
            pl.BlockSpec((None, D_MODEL, tn), lambda l, j: (l, 0, j)),
            pl.BlockSpec((None, 1, tn), lambda l, j: (l, 0, j)),
        ],
        out_specs=pl.BlockSpec((None, COND_ROWS, tn), lambda l, j: (l, 0, j)),
        out_shape=jax.ShapeDtypeStruct((DEPTH, COND_ROWS, d3), F32),
        compiler_params=_cparams(2),
        name="adaln_rows",
    )(cond, mod_w, mod_b.reshape(DEPTH, 1, d3))
    return out.reshape(DEPTH, COND_ROWS, 3, D_MODEL)


def _inproj_kernel(x_ref, mod_ref, nw_ref, w_ref, o_ref, h_ref):
    @pl.when(pl.program_id(1) == 0)
    def _():
        xn = _rms(x_ref[...], nw_ref[...])
        h_ref[...] = (xn * (1.0 + mod_ref[1:2, :]) + mod_ref[0:1, :]).astype(BF16)

    o_ref[...] = jnp.dot(h_ref[...], w_ref[...], preferred_element_type=F32).astype(o_ref.dtype)


def _in_proj(x, mod, norm_w, w, row_of_tile, tm, tn):
    t, n = x.shape[0], w.shape[1]
    return pl.pallas_call(
        _inproj_kernel,
        grid=(t // tm, n // tn),
        in_specs=[
            pl.BlockSpec((tm, D_MODEL), lambda i, j: (i, 0)),
            pl.BlockSpec((None, 3, D_MODEL), lambda i, j: (row_of_tile(i), 0, 0)),
            pl.BlockSpec((1, D_MODEL), lambda i, j: (0, 0)),
            pl.BlockSpec((D_MODEL, tn), lambda i, j: (0, j)),
        ],
        out_specs=pl.BlockSpec((tm, tn), lambda i, j: (i, j)),
        out_shape=jax.ShapeDtypeStruct((t, n), F32),
        scratch_shapes=[pltpu.VMEM((tm, D_MODEL), BF16)],
        compiler_params=_cparams(2),
        name="in_proj",
    )(x, mod, norm_w.reshape(1, D_MODEL), w)


def _outproj_kernel(o_ref, w_ref, x_ref, mod_ref, y_ref):
    acc = jnp.dot(o_ref[...], w_ref[...], preferred_element_type=F32)
    y_ref[...] = x_ref[...] + mod_ref[2:3, :] * acc


def _out_proj(o, w, x, mod, row_of_tile, tm, tn):
    t, kdim = o.shape
    return pl.pallas_call(
        _outproj_kernel,
        grid=(t // tm, D_MODEL // tn),
        in_specs=[
            pl.BlockSpec((tm, kdim), lambda i, j: (i, 0)),
            pl.BlockSpec((kdim, tn), lambda i, j: (0, j)),
            pl.BlockSpec((tm, tn), lambda i, j: (i, j)),
            pl.BlockSpec((None, 3, tn), lambda i, j: (row_of_tile(i), 0, j)),
        ],
        out_specs=pl.BlockSpec((tm, tn), lambda i, j: (i, j)),
        out_shape=jax.ShapeDtypeStruct((t, D_MODEL), F32),
        compiler_params=_cparams(2),
        name="out_proj",
    )(o, w, x, mod)


def _rope_tables(n):
    pos = jnp.arange(n)
    quarter = HEAD // 4
    inv_freq = ROPE_THETA ** (-jnp.arange(quarter, dtype=F32) / quarter)

    def axis_angles(p):
        ang = p.astype(F32)[:, None] * inv_freq[None, :]
        return jnp.concatenate([ang, ang], axis=-1)

    ang = jnp.concatenate([axis_angles(pos // GRID_W), axis_angles(pos % GRID_W)], axis=-1)
    cos, sin = jnp.cos(ang), jnp.sin(ang)
    first = (jnp.arange(HEAD) % (2 * quarter)) < quarter
    return jnp.stack([cos, jnp.where(first, -sin, 0.0), jnp.where(first, 0.0, sin)])


def _rope(x, tab_ref):
    up = pltpu.roll(x, HEAD - HEAD // 4, 1)
    down = pltpu.roll(x, HEAD // 4, 1)
    return x * tab_ref[0] + up * tab_ref[1] + down * tab_ref[2]


def _kvprep_kernel(*refs, n_kheads, rope, emit_norm):
    k_ref, v_ref, kn_ref = refs[:3]
    pos = 3
    tab_ref = None
    if rope:
        tab_ref = refs[pos]
        pos += 1
    kh_ref, vt_ref = refs[pos], refs[pos + 1]
    kn_out = refs[pos + 2] if emit_norm else None
    for h in range(n_kheads):
        sl = slice(h * HEAD, (h + 1) * HEAD)
        kn = _rms(k_ref[:, sl], kn_ref[...])
        if emit_norm:
            kn_out[:, sl] = kn
        if rope:
            kn = _rope(kn, tab_ref)
        kh_ref[:, sl] = kn.astype(BF16)
    vt_ref[...] = v_ref[...].T.astype(BF16)


def _kv_prep(p, k_col, k_w, v_col, v_w, k_norm, rope_tab, n_batch, n_seq):
    t = p.shape[0]
    rope = rope_tab is not None
    emit_norm = not rope
    tr = min(512, n_seq)
    per_seq = n_seq // tr
    in_specs = [
        pl.BlockSpec((tr, k_w), lambda i: (i, k_col // k_w)),
        pl.BlockSpec((tr, v_w), lambda i: (i, v_col // v_w)),
        pl.BlockSpec((1, HEAD), lambda i: (0, 0)),
    ]
    args = [p, p, k_norm.reshape(1, HEAD)]
    if rope:
        in_specs.append(pl.BlockSpec((3, tr, HEAD), lambda i: (0, i % per_seq, 0)))
        args.append(rope_tab)
    out_specs = [
        pl.BlockSpec((tr, k_w), lambda i: (i, 0)),
        pl.BlockSpec((None, v_w, tr), lambda i: (i // per_seq, 0, i % per_seq)),
    ]
    out_shape = [jax.ShapeDtypeStruct((t, k_w), BF16), jax.ShapeDtypeStruct((n_batch, v_w, n_seq), BF16)]
    if emit_norm:
        out_specs.append(pl.BlockSpec((tr, k_w), lambda i: (i, 0)))
        out_shape.append(jax.ShapeDtypeStruct((t, k_w), F32))
    return pl.pallas_call(
        functools.partial(_kvprep_kernel, n_kheads=k_w // HEAD, rope=rope, emit_norm=emit_norm),
        grid=(t // tr,),
        in_specs=in_specs,
        out_specs=out_specs,
        out_shape=out_shape,
        compiler_params=_cparams(1),
        name="kv_prep",
    )(*args)


_NT = (((1,), (1,)), ((), ()))


def _key_chunks(k_ref, vt_ref):
    n = k_ref.shape[0]
    step = min(n, ATTN_KEY_CHUNK)
    return [(k_ref[c0:c0 + step, :], vt_ref[:, c0:c0 + step]) for c0 in range(0, n, step)]


def _attend_chunks(q, chunks):
    n = len(chunks)
    n_v = chunks[0][1].shape[0]
    s, m, e, o = ([None] * n for _ in range(4))

    def score(j):
        s[j] = lax.dot_general(chunks[j][0], q, _NT, preferred_element_type=F32)
        m[j] = s[j].max(axis=0, keepdims=True)

    for j in range(min(2, n)):
        score(j)
    for j in range(n):
        e[j] = jnp.exp2(s[j] - m[j]).astype(BF16)
        if j + 2 < n:
            score(j + 2)
        vt = chunks[j][1]
        vt_ones = jnp.concatenate([vt, jnp.ones((BF16_SUBLANES, vt.shape[1]), BF16)], axis=0)
        o[j] = jnp.dot(vt_ones, e[j], preferred_element_type=F32)
    if n == 1:
        acc = o[0]
    else:
        m_all = functools.reduce(jnp.maximum, m)
        acc = None
        for j in range(n):
            part = o[j] * jnp.exp2(m[j] - m_all)
            acc = part if acc is None else acc + part
    return acc[:n_v], acc[n_v:n_v + 1]


def _diff_attn_kernel(*refs, has_ctx, lam_init, tq):
    q_ref, g_ref, kh_ref, vt_ref, qn_ref, lamv_ref, sub_ref = refs[:7]
    pos = 7
    if has_ctx:
        tab_ref, ck_ref, cvt_ref = refs[pos:pos + 3]
        pos += 3
    o_ref = refs[pos]

    lv = lamv_ref[...]
    lam = (jnp.exp(jnp.sum(lv[0:1] * lv[1:2], keepdims=True))
           - jnp.exp(jnp.sum(lv[2:3] * lv[3:4], keepdims=True)) + lam_init)

    zero = jnp.zeros((tq, HEAD), BF16)
    rows = []
    for comp in range(2):
        q = _rms(q_ref[:, comp * HEAD:(comp + 1) * HEAD], qn_ref[...])
        if has_ctx:
            q = _rope(q, tab_ref)
        q = (q * ATTN_SCALE_LOG2).astype(BF16)
        rows.append(jnp.concatenate([q, zero] if comp == 0 else [zero, q], axis=1))
    q_bd = jnp.concatenate(rows, axis=0)
    chunks = _key_chunks(kh_ref, vt_ref) + (_key_chunks(ck_ref, cvt_ref) if has_ctx else [])
    acc, l = _attend_chunks(q_bd, chunks)
    o_t = acc[:, :tq] * (1.0 / l[:, :tq]) - acc[:, tq:] * (lam / l[:, tq:])
    o = _rms(o_t.T, sub_ref[...]) * (1.0 - lam_init)
    o_ref[...] = (o * _silu(g_ref[...])).astype(BF16)


def _diff_attention(p, kh, vt, q_norm, lam_vecs, subln, lam_init, n_batch, n_seq, tq, rope_tab=None, ctx=None):
    has_ctx = ctx is not None
    w2 = 2 * HEAD
    nq = n_seq // tq
    g_col0 = 3 * D_MODEL // w2
    in_specs = [
        pl.BlockSpec((tq, w2), lambda b, h, i: (b * nq + i, h)),
        pl.BlockSpec((tq, w2), lambda b, h, i: (b * nq + i, g_col0 + h)),
        pl.BlockSpec((n_seq, w2), lambda b, h, i: (b, h)),
        pl.BlockSpec((None, w2, n_seq), lambda b, h, i: (b, h, 0)),
        pl.BlockSpec((1, HEAD), lambda b, h, i: (0, 0)),
        pl.BlockSpec((4, HEAD), lambda b, h, i: (0, 0)),
        pl.BlockSpec((1, w2), lambda b, h, i: (0, 0)),
    ]
    args = [p, p, kh, vt, q_norm.reshape(1, HEAD), lam_vecs, subln.reshape(1, w2)]
    if has_ctx:
        ck, cvt = ctx
        n_ctx = ck.shape[1]
        in_specs += [
            pl.BlockSpec((3, tq, HEAD), lambda b, h, i: (0, i, 0)),
            pl.BlockSpec((None, n_ctx, w2), lambda b, h, i: (b, 0, h)),
            pl.BlockSpec((None, w2, n_ctx), lambda b, h, i: (b, h, 0)),
        ]
        args += [rope_tab, ck, cvt]
    return pl.pallas_call(
        functools.partial(_diff_attn_kernel, has_ctx=has_ctx, lam_init=lam_init, tq=tq),
        grid=(n_batch, B_HEADS, nq),
        in_specs=in_specs,
        out_specs=pl.BlockSpec((tq, w2), lambda b, h, i: (b * nq + i, h)),
        out_shape=jax.ShapeDtypeStruct((n_batch * n_seq, D_MODEL), BF16),
        compiler_params=_cparams(3),
        name="diff_attention",
    )(*args)


def _gqa_kernel(*refs, has_ctx, tq):
    q_ref, g_ref, kh_ref, vt_ref, qn_ref = refs[:5]
    pos = 5
    if has_ctx:
        tab_ref, ck_ref, cvt_ref = refs[pos:pos + 3]
        pos += 3
    o_ref = refs[pos]

    qs = []
    for gi in range(C_GROUP):
        q = _rms(q_ref[:, gi * HEAD:(gi + 1) * HEAD], qn_ref[...])
        if has_ctx:
            q = _rope(q, tab_ref)
        qs.append((q * ATTN_SCALE_LOG2).astype(BF16))
    q_all = jnp.concatenate(qs, axis=0)
    chunks = _key_chunks(kh_ref, vt_ref) + (_key_chunks(ck_ref, cvt_ref) if has_ctx else [])
    acc, l = _attend_chunks(q_all, chunks)
    o = (acc * (1.0 / l)).T
    for gi in range(C_GROUP):
        sl = slice(gi * HEAD, (gi + 1) * HEAD)
        o_ref[:, sl] = (o[gi * tq:(gi + 1) * tq] * _silu(g_ref[:, sl])).astype(BF16)


def _gqa_attention(p, kh, vt, q_norm, n_batch, n_seq, tq, rope_tab=None, ctx=None):
    has_ctx = ctx is not None
    wq = C_GROUP * HEAD
    nq = n_seq // tq
    g_col0 = (D_MODEL + 2 * C_KV_HEADS * HEAD) // wq
    in_specs = [
        pl.BlockSpec((tq, wq), lambda b, h, i: (b * nq + i, h)),
        pl.BlockSpec((tq, wq), lambda b, h, i: (b * nq + i, g_col0 + h)),
        pl.BlockSpec((n_seq, HEAD), lambda b, h, i: (b, h)),
        pl.BlockSpec((None, HEAD, n_seq), lambda b, h, i: (b, h, 0)),
        pl.BlockSpec((1, HEAD), lambda b, h, i: (0, 0)),
    ]
    args = [p, p, kh, vt, q_norm.reshape(1, HEAD)]
    if has_ctx:
        ck, cvt = ctx
        n_ctx = ck.shape[1]
        in_specs += [
            pl.BlockSpec((3, tq, HEAD), lambda b, h, i: (0, i, 0)),
            pl.BlockSpec((None, n_ctx, HEAD), lambda b, h, i: (b, 0, h)),
            pl.BlockSpec((None, HEAD, n_ctx), lambda b, h, i: (b, h, 0)),
        ]
        args += [rope_tab, ck, cvt]
    return pl.pallas_call(
        functools.partial(_gqa_kernel, has_ctx=has_ctx, tq=tq),
        grid=(n_batch, C_KV_HEADS, nq),
        in_specs=in_specs,
        out_specs=pl.BlockSpec((tq, wq), lambda b, h, i: (b * nq + i, h)),
        out_shape=jax.ShapeDtypeStruct((n_batch * n_seq, D_MODEL), BF16),
        compiler_params=_cparams(3),
        name="gqa_attention",
    )(*args)


def _forget_gate(z, lb):
    t = jnp.exp(-jnp.abs(z))
    r = 1.0 / (1.0 + t)
    tr = t * r
    pos = z >= 0.0
    sig = jnp.where(pos, r, tr)
    nsig = jnp.where(pos, tr, r)
    log_sig = jnp.minimum(z, 0.0) - jnp.log1p(t)
    one_m = 1.0 - lb
    logf = jnp.where(lb > 0.0, jnp.log(lb + one_m * sig), log_sig)
    return logf, one_m * nsig


def _cumsum_rows(tri, x):
    hi = x.astype(BF16)
    mid = (x - hi.astype(F32)).astype(BF16)
    r = jnp.dot(tri, jnp.concatenate([hi, mid], axis=1), preferred_element_type=F32)
    return r[:, :HEAD] + r[:, HEAD:]


def _pair_rows(ref, b, odd):
    off = b if odd else 0
    if b >= 8:
        parts = [ref[pl.ds(i * 2 * b + off, b), :] for i in range(SCAN_HALF // b)]
    else:
        parts = [ref[pl.ds(off + r, SCAN_HALF // b, stride=2 * b), :] for r in range(b)]
    return parts[0] if len(parts) == 1 else jnp.concatenate(parts, axis=0)


def _pair_bcast(ref, b, row):
    if b >= 8:
        parts = [jnp.broadcast_to(ref[pl.ds(i * 2 * b + row, 1), :], (b, HEAD)) for i in range(SCAN_HALF // b)]
    else:
        parts = [ref[pl.ds(row, SCAN_HALF // b, stride=2 * b), :]] * b
    return parts[0] if len(parts) == 1 else jnp.concatenate(parts, axis=0)


def _pair_store(ref, b, odd, val):
    off = b if odd else 0
    if b >= 8:
        for i in range(SCAN_HALF // b):
            ref[pl.ds(i * 2 * b + off, b), :] = val[i * b:(i + 1) * b]
    else:
        n = SCAN_HALF // b
        for r in range(b):
            ref[pl.ds(off + r, n, stride=2 * b), :] = val[r * n:(r + 1) * n]


def _hgrn_kernel(*refs, n_chunks, has_state):
    q_ref, zf_ref, zb_ref, v_ref, g_ref, lb_ref, on_ref = refs[:7]
    pos = 7
    s0_ref = None
    if has_state:
        s0_ref = refs[pos]
        pos += 1
    o_ref = refs[pos]
    pos += 1
    sout_ref = None
    if not has_state:
        sout_ref = refs[pos]
        pos += 1
    tri_ref, mask_ref, sq, sv, skf, scf, sxf, skb, scb, sxb = refs[pos:pos + 10]
    pos += 10
    lvl_out = refs[pos:pos + SCAN_LEVELS]
    pos += SCAN_LEVELS
    if has_state:
        oall, qb_all, klb_all, decb_all = refs[pos:pos + 4]

    c = SCAN_CHUNK
    row = lax.broadcasted_iota(jnp.int32, (c, c), 0)
    col = lax.broadcasted_iota(jnp.int32, (c, c), 1)
    tri_ref[0] = (col <= row).astype(BF16)
    tri_ref[1] = (col >= row).astype(BF16)
    prow = lax.broadcasted_iota(jnp.int32, (SCAN_HALF, SCAN_HALF), 0)
    pcol = lax.broadcasted_iota(jnp.int32, (SCAN_HALF, SCAN_HALF), 1)
    for j in range(SCAN_LEVELS):
        b = 1 << j
        if b >= 8:
            same = (prow // b) == (pcol // b)
        else:
            same = (prow % (SCAN_HALF // b)) == (pcol % (SCAN_HALF // b))
        mask_ref[j] = same.astype(F32)

    lb_f = lb_ref[0:1, :]
    lb_b = lb_ref[1:2, :]

    def intra(r):
        q = _silu(q_ref[r, :])
        v = v_ref[r, :]
        lf, kf = _forget_gate(zf_ref[r, :], lb_f)
        lbk, kb = _forget_gate(zb_ref[r, :], lb_b)
        lf = lf * LOG2_E
        lbk = lbk * LOG2_E
        cf = _cumsum_rows(tri_ref[0], lf)
        cb = _cumsum_rows(tri_ref[1], lbk)
        sq[...] = q
        sv[...] = v
        skf[...] = kf
        scf[...] = cf
        sxf[...] = cf - lf
        skb[...] = kb
        scb[...] = cb
        sxb[...] = cb - lbk
        scores = []
        for j in range(SCAN_LEVELS):
            b = 1 << j
            mask = mask_ref[j]
            qt = _pair_rows(sq, b, True) * jnp.exp2(_pair_rows(scf, b, True) - _pair_bcast(sxf, b, b))
            kt = _pair_rows(skf, b, False) * jnp.exp2(_pair_bcast(scf, b, b - 1) - _pair_rows(scf, b, False))
            z = lax.dot_general(qt.astype(BF16), kt.astype(BF16), _NT, preferred_element_type=F32) * mask
            scores.append((j, True, z.astype(BF16)))
            qt = _pair_rows(sq, b, False) * jnp.exp2(_pair_rows(scb, b, False) - _pair_bcast(sxb, b, b - 1))
            kt = _pair_rows(skb, b, True) * jnp.exp2(_pair_bcast(scb, b, b) - _pair_rows(scb, b, True))
            z = lax.dot_general(qt.astype(BF16), kt.astype(BF16), _NT, preferred_element_type=F32) * mask
            scores.append((j, False, z.astype(BF16)))
        for j, q_odd, z in scores:
            b = 1 << j
            _pair_store(lvl_out[j], b, q_odd, jnp.dot(z, _pair_rows(sv, b, not q_odd).astype(BF16),
                                                      preferred_element_type=F32))
        o = jnp.sum(q * (kf + kb), axis=-1, keepdims=True) * v
        for j in range(SCAN_LEVELS):
            o = o + lvl_out[j][...]
        tot_f = cf[c - 1:c, :]
        tot_b = cb[0:1, :]
        klf = kf * jnp.exp2(tot_f - cf)
        klb = kb * jnp.exp2(tot_b - cb)
        return o, q, v, cf, cb, tot_f, tot_b, klf, klb

    def finish(r, o):
        o = _rms(o, on_ref[...]) * _silu(g_ref[r, :])
        o_ref[r, :] = o.astype(BF16)

    def rows_of(ci):
        return pl.ds(pl.multiple_of(ci * c, c), c)

    if not has_state:
        def seq(ci, carry):
            r = rows_of(ci)
            o, q, v, cf, cb, tot_f, tot_b, klf, klb = intra(r)
            v_b = v.astype(BF16)
            sout_ref[ci, 0] = jnp.dot(klf.T.astype(BF16), v_b, preferred_element_type=F32)
            sout_ref[ci, 1] = jnp.dot(klb.T.astype(BF16), v_b, preferred_element_type=F32)
            finish(r, o)
            return carry

        lax.fori_loop(0, n_chunks, seq, 0)
        return

    def row_to_col(x):
        return jnp.broadcast_to(x, (HEAD, HEAD)).T

    def fwd(ci, s):
        r = rows_of(ci)
        o, q, v, cf, cb, tot_f, tot_b, klf, klb = intra(r)
        oall[r, :] = o + jnp.dot((q * jnp.exp2(cf)).astype(BF16), s.astype(BF16), preferred_element_type=F32)
        qb_all[r, :] = (q * jnp.exp2(cb)).astype(BF16)
        klb_all[r, :] = klb
        decb_all[pl.ds(pl.multiple_of(ci * 8, 8), 1), :] = jnp.exp2(tot_b)
        return row_to_col(jnp.exp2(tot_f)) * s + jnp.dot(klf.T.astype(BF16), v.astype(BF16),
                                                         preferred_element_type=F32)

    def bwd(i, s):
        ci = n_chunks - 1 - i
        r = rows_of(ci)
        finish(r, oall[r, :] + jnp.dot(qb_all[r, :], s.astype(BF16), preferred_element_type=F32))
        dec = decb_all[pl.ds(pl.multiple_of(ci * 8, 8), 1), :]
        return row_to_col(dec) * s + jnp.dot(klb_all[r, :].T.astype(BF16), v_ref[r, :].astype(BF16),
                                             preferred_element_type=F32)

    lax.fori_loop(0, n_chunks, fwd, s0_ref[0])
    lax.fori_loop(0, n_chunks, bwd, s0_ref[1])


def _hgrn_scan(p, lb, o_norm, n_batch, n_seq, state=None, seqs_per_step=8):
    has_state = state is not None
    c = SCAN_CHUNK
    if has_state:
        rows, n_steps = n_seq, n_batch
    else:
        assert n_seq == c and n_batch % seqs_per_step == 0
        rows, n_steps = seqs_per_step * c, n_batch // seqs_per_step
    n_chunks = rows // c
    col = lambda s: (lambda b, h: (b, s * A_HEADS + h))
    in_specs = [pl.BlockSpec((rows, HEAD), col(s)) for s in range(5)]
    in_specs += [
        pl.BlockSpec((2, HEAD), lambda b, h: (0, h)),
        pl.BlockSpec((1, HEAD), lambda b, h: (0, 0)),
    ]
    args = [p] * 5 + [lb, o_norm.reshape(1, HEAD)]
    out_specs = [pl.BlockSpec((rows, HEAD), lambda b, h: (b, h))]
    out_shape = [jax.ShapeDtypeStruct((n_batch * n_seq, D_MODEL), BF16)]
    scratch = [pltpu.VMEM((2, c, c), BF16), pltpu.VMEM((SCAN_LEVELS, SCAN_HALF, SCAN_HALF), F32)]
    scratch += [pltpu.VMEM((c, HEAD), F32)] * (8 + SCAN_LEVELS)
    if has_state:
        in_specs.append(pl.BlockSpec((None, 2, None, HEAD, HEAD), lambda b, h: (b, 0, h, 0, 0)))
        args.append(state)
        scratch += [pltpu.VMEM((rows, HEAD), F32), pltpu.VMEM((rows, HEAD), BF16),
                    pltpu.VMEM((rows, HEAD), F32), pltpu.VMEM((n_chunks * 8, HEAD), F32)]
    else:
        out_specs.append(pl.BlockSpec((n_chunks, 2, None, HEAD, HEAD), lambda b, h: (b, 0, h, 0, 0)))
        out_shape.append(jax.ShapeDtypeStruct((n_batch, 2, A_HEADS, HEAD, HEAD), F32))
    res = pl.pallas_call(
        functools.partial(_hgrn_kernel, n_chunks=n_chunks, has_state=has_state),
        grid=(n_steps, A_HEADS),
        in_specs=in_specs,
        out_specs=out_specs,
        out_shape=out_shape,
        scratch_shapes=scratch,
        compiler_params=_cparams(2),
        name="hgrn2_scan",
    )(*args)
    return (res[0], None) if has_state else (res[0], res[1])


def _diff_lambda_init(layer):
    return 0.8 - 0.6 * math.exp(-0.3 * layer)


def kernel(x_prompt, x_sample, state_a, cache_b_k, cache_b_v, cache_c_k, cache_c_v, c, c_ctx, norm_w, mod_w, mod_b, a_w_in, a_w_out, a_o_norm, a_lower_bound, b_w_in, b_w_out, b_q_norm, b_k_norm, b_lambda, b_subln, c_w_in, c_w_out, c_q_norm, c_k_norm):
    n_ctx_b, n_ctx_s = x_prompt.shape[:2]
    n_lat_b, n_lat_s = x_sample.shape[:2]
    past = cache_b_k.shape[2]
    tm = tn = 1024

    lb_all = jnp.cumsum(jax.nn.softmax(a_lower_bound.astype(F32), axis=0), axis=0)
    lb_all = lb_all - lb_all[0:1]

    cond = jnp.zeros((COND_ROWS, D_MODEL), F32).at[0].set(c_ctx).at[1:1 + n_lat_b].set(c)
    mods = _mod_rows(cond, mod_w, mod_b)
    rope_tab = _rope_tables(n_lat_s)

    lat_tiles = n_lat_s // tm
    groups = [
        dict(x=x_prompt.reshape(-1, D_MODEL), nb=n_ctx_b, ns=n_ctx_s, row=lambda i: 0, latent=False),
        dict(x=x_sample.reshape(-1, D_MODEL), nb=n_lat_b, ns=n_lat_s, row=lambda i: 1 + i // lat_tiles, latent=True),
    ]
    new_a, new_bk, new_bv, new_ck, new_cv = [], [], [], [], []

    for layer in range(DEPTH):
        kind, j = layer % N_MIXERS, layer // N_MIXERS
        w_in = (a_w_in, b_w_in, c_w_in)[kind][j].astype(BF16)
        w_out = (a_w_out, b_w_out, c_w_out)[kind][j].astype(BF16)
        for grp in groups:
            nb, ns, latent = grp["nb"], grp["ns"], grp["latent"]
            p = _in_proj(grp["x"], mods[layer], norm_w[layer], w_in, grp["row"], tm, tn)
            if kind == 0:
                if latent:
                    o, _ = _hgrn_scan(p, lb_all[layer], a_o_norm[j], nb, ns, state=state_a[:, j])
                else:
                    o, s_fin = _hgrn_scan(p, lb_all[layer], a_o_norm[j], nb, ns)
                    new_a.append(s_fin)
            elif kind == 1:
                tab = rope_tab if latent else None
                res = _kv_prep(p, D_MODEL, D_MODEL, 2 * D_MODEL, D_MODEL, b_k_norm[j], tab, nb, ns)
                ctx = None
                if latent:
                    ctx = (cache_b_k[:, j].reshape(nb, past, D_MODEL).astype(BF16),
                           jnp.swapaxes(cache_b_v[:, j].reshape(nb, past, D_MODEL), 1, 2).astype(BF16))
                else:
                    new_bk.append(res[2].reshape(nb, ns, B_HEADS, 2, HEAD))
                    new_bv.append(p[:, 2 * D_MODEL:3 * D_MODEL].reshape(nb, ns, B_HEADS, 2 * HEAD))
                o = _diff_attention(p, res[0], res[1], b_q_norm[j], b_lambda[j], b_subln[j],
                                    _diff_lambda_init(layer), nb, ns, 512 if latent else 256, rope_tab=tab, ctx=ctx)
            else:
                kvw = C_KV_HEADS * HEAD
                tab = rope_tab if latent else None
                res = _kv_prep(p, D_MODEL, kvw, D_MODEL + kvw, kvw, c_k_norm[j], tab, nb, ns)
                ctx = None
                if latent:
                    ctx = (cache_c_k[:, j].reshape(nb, past, kvw).astype(BF16),
                           jnp.swapaxes(cache_c_v[:, j].reshape(nb, past, kvw), 1, 2).astype(BF16))
                else:
                    new_ck.append(res[2].reshape(nb, ns, C_KV_HEADS, HEAD))
                    new_cv.append(p[:, D_MODEL + kvw:D_MODEL + 2 * kvw].reshape(nb, ns, C_KV_HEADS, HEAD))
                o = _gqa_attention(p, res[0], res[1], c_q_norm[j], nb, ns, 256, rope_tab=tab, ctx=ctx)
            grp["x"] = _out_proj(o, w_out, grp["x"], mods[layer], grp["row"], tm, tn)

    y_prompt = groups[0]["x"].reshape(x_prompt.shape)
    y_sample = groups[1]["x"].reshape(x_sample.shape)
    return (y_prompt, y_sample, jnp.stack(new_a, axis=1), jnp.stack(new_bk, axis=1), jnp.stack(new_bv, axis=1),
            jnp.stack(new_ck, axis=1), jnp.stack(new_cv, axis=1))
```

```python
import functools
import math

import jax
import jax.numpy as jnp
from jax import lax
from jax.experimental import pallas as pl
from jax.experimental.pallas import tpu as pltpu

F32 = jnp.float32
BF16 = jnp.bfloat16

D_MODEL = 2048
DEPTH = 4
GRID_W = 64
N_MIXERS = 3
EPS = 1e-6
ROPE_THETA = 10000.0
HEAD = 128
A_HEADS = D_MODEL // HEAD
B_HEADS = D_MODEL // (2 * HEAD)
C_HEADS = D_MODEL // HEAD
C_KV_HEADS = C_HEADS // 4
C_GROUP = C_HEADS // C_KV_HEADS
COND_ROWS = 8

V7X_VMEM_LIMIT_BYTES = 56 * 1024 * 1024
SCAN_CHUNK = 256
SCAN_HALF = SCAN_CHUNK // 2
SCAN_LEVELS = SCAN_CHUNK.bit_length() - 1
LOG2_E = math.log2(math.e)
ATTN_KEY_CHUNK = 1024
ATTN_SCALE_LOG2 = (HEAD ** -0.5) * LOG2_E
BF16_SUBLANES = 16


def _cparams(n_axes):
    return pltpu.CompilerParams(
        dimension_semantics=("arbitrary",) * n_axes,
        vmem_limit_bytes=V7X_VMEM_LIMIT_BYTES,
    )


def _sigmoid(x):
    return 1.0 / (1.0 + jnp.exp(-x))


def _silu(x):
    return x * _sigmoid(x)


def _rms(x, w):
    return x * lax.rsqrt(jnp.mean(x * x, axis=-1, keepdims=True) + EPS) * w


def _mod_kernel(cond_ref, w_ref, b_ref, o_ref):
    s = _silu(cond_ref[...]).astype(BF16)
    o_ref[...] = jnp.dot(s, w_ref[...].astype(BF16), preferred_element_type=F32) + b_ref[...]


def _mod_rows(cond, mod_w, mod_b):
    d3 = 3 * D_MODEL
    tn = 768
    out = pl.pallas_call(
        _mod_kernel,
        grid=(DEPTH, d3 // tn),
        in_specs=[
            pl.BlockSpec((COND_ROWS, D_MODEL), lambda l, j: (0, 0)),
            pl.BlockSpec((None, D_MODEL, tn), lambda l, j: (l, 0, j)),
            pl.BlockSpec((None, 1, tn), lambda l, j: (l, 0, j)),
        ],
        out_specs=pl.BlockSpec((None, COND_ROWS, tn), lambda l, j: (l, 0, j)),
        out_shape=jax.ShapeDtypeStruct((DEPTH, COND_ROWS, d3), F32),
        compiler_params=_cparams(2),
        name="adaln_rows",
    )(cond, mod_w, mod_b.reshape(DEPTH, 1, d3))
    return out.reshape(DEPTH, COND_ROWS, 3, D_MODEL)


def _inproj_kernel(x_ref, mod_ref, nw_ref, w_ref, o_ref, h_ref):
    @pl.when(pl.program_id(1) == 0)
    def _():
        xn = _rms(x_ref[...], nw_ref[...])
        h_ref[...] = (xn * (1.0 + mod_ref[1:2, :]) + mod_ref[0:1, :]).astype(BF16)

    o_ref[...] = jnp.dot(h_ref[...], w_ref[...], preferred_element_type=F32).astype(o_ref.dtype)


def _in_proj(x, mod, norm_w, w, row_of_tile, tm, tn):
    t, n = x.shape[0], w.shape[1]
    return pl.pallas_call(
        _inproj_kernel,
        grid=(t // tm, n // tn),
        in_specs=[
            pl.BlockSpec((tm, D_MODEL), lambda i, j: (i, 0)),
            pl.BlockSpec((None, 3, D_MODEL), lambda i, j: (row_of_tile(i), 0, 0)),
            pl.BlockSpec((1, D_MODEL), lambda i, j: (0, 0)),
            pl.BlockSpec((D_MODEL, tn), lambda i, j: (0, j)),
        ],
        out_specs=pl.BlockSpec((tm, tn), lambda i, j: (i, j)),
        out_shape=jax.ShapeDtypeStruct((t, n), F32),
        scratch_shapes=[pltpu.VMEM((tm, D_MODEL), BF16)],
        compiler_params=_cparams(2),
        name="in_proj",
    )(x, mod, norm_w.reshape(1, D_MODEL), w)


def _outproj_kernel(o_ref, w_ref, x_ref, mod_ref, y_ref):
    acc = jnp.dot(o_ref[...], w_ref[...], preferred_element_type=F32)
    y_ref[...] = x_ref[...] + mod_ref[2:3, :] * acc


def _out_proj(o, w, x, mod, row_of_tile, tm, tn):
    t, kdim = o.shape
    return pl.pallas_call(
        _outproj_kernel,
        grid=(t // tm, D_MODEL // tn),
        in_specs=[
            pl.BlockSpec((tm, kdim), lambda i, j: (i, 0)),
            pl.BlockSpec((kdim, tn), lambda i, j: (0, j)),
            pl.BlockSpec((tm, tn), lambda i, j: (i, j)),
            pl.BlockSpec((None, 3, tn), lambda i, j: (row_of_tile(i), 0, j)),
        ],
        out_specs=pl.BlockSpec((tm, tn), lambda i, j: (i, j)),
        out_shape=jax.ShapeDtypeStruct((t, D_MODEL), F32),
        compiler_params=_cparams(2),
        name="out_proj",
    )(o, w, x, mod)


def _rope_tables(n):
    pos = jnp.arange(n)
    quarter = HEAD // 4
    inv_freq = ROPE_THETA ** (-jnp.arange(quarter, dtype=F32) / quarter)

    def axis_angles(p):
        ang = p.astype(F32)[:, None] * inv_freq[None, :]
        return jnp.concatenate([ang, ang], axis=-1)

    ang = jnp.concatenate([axis_angles(pos // GRID_W), axis_angles(pos % GRID_W)], axis=-1)
    cos, sin = jnp.cos(ang), jnp.sin(ang)
    first = (jnp.arange(HEAD) % (2 * quarter)) < quarter
    return jnp.stack([cos, jnp.where(first, -sin, 0.0), jnp.where(first, 0.0, sin)])


def _rope(x, tab_ref):
    up = pltpu.roll(x, HEAD - HEAD // 4, 1)
    down = pltpu.roll(x, HEAD // 4, 1)
    return x * tab_ref[0] + up * tab_ref[1] + down * tab_ref[2]


def _kvprep_kernel(*refs, n_kheads, rope, emit_norm):
    k_ref, v_ref, kn_ref = refs[:3]
    pos = 3
    tab_ref = None
    if rope:
        tab_ref = refs[pos]
        pos += 1
    kh_ref, vt_ref = refs[pos], refs[pos + 1]
    kn_out, v_out = (refs[pos + 2], refs[pos + 3]) if emit_norm else (None, None)
    for h in range(n_kheads):
        sl = slice(h * HEAD, (h + 1) * HEAD)
        kn = _rms(k_ref[:, sl], kn_ref[...])
        if emit_norm:
            kn_out[:, sl] = kn
        if rope:
            kn = _rope(kn, tab_ref)
        kh_ref[:, sl] = kn.astype(BF16)
    v = v_ref[...]
    if emit_norm:
        v_out[...] = v
    vt_ref[...] = v.T.astype(BF16)


def _kv_prep(p, k_col, k_w, v_col, v_w, k_norm, rope_tab, n_batch, n_seq):
    t = p.shape[0]
    rope = rope_tab is not None
    emit_norm = not rope
    tr = min(512, n_seq)
    per_seq = n_seq // tr
    in_specs = [
        pl.BlockSpec((tr, k_w), lambda i: (i, k_col // k_w)),
        pl.BlockSpec((tr, v_w), lambda i: (i, v_col // v_w)),
        pl.BlockSpec((1, HEAD), lambda i: (0, 0)),
    ]
    args = [p, p, k_norm.reshape(1, HEAD)]
    if rope:
        in_specs.append(pl.BlockSpec((3, tr, HEAD), lambda i: (0, i % per_seq, 0)))
        args.append(rope_tab)
    out_specs = [
        pl.BlockSpec((tr, k_w), lambda i: (i, 0)),
        pl.BlockSpec((None, v_w, tr), lambda i: (i // per_seq, 0, i % per_seq)),
    ]
    out_shape = [jax.ShapeDtypeStruct((t, k_w), BF16), jax.ShapeDtypeStruct((n_batch, v_w, n_seq), BF16)]
    if emit_norm:
        out_specs += [pl.BlockSpec((tr, k_w), lambda i: (i, 0)), pl.BlockSpec((tr, v_w), lambda i: (i, 0))]
        out_shape += [jax.ShapeDtypeStruct((t, k_w), F32), jax.ShapeDtypeStruct((t, v_w), F32)]
    return pl.pallas_call(
        functools.partial(_kvprep_kernel, n_kheads=k_w // HEAD, rope=rope, emit_norm=emit_norm),
        grid=(t // tr,),
        in_specs=in_specs,
        out_specs=out_specs,
        out_shape=out_shape,
        compiler_params=_cparams(1),
        name="kv_prep",
    )(*args)


_NT = (((1,), (1,)), ((), ()))


def _key_chunks(k_ref, vt_ref):
    n = k_ref.shape[0]
    step = min(n, ATTN_KEY_CHUNK)
    return [(k_ref[c0:c0 + step, :], vt_ref[:, c0:c0 + step]) for c0 in range(0, n, step)]


def _attend_chunks(q, chunks):
    n = len(chunks)
    n_v = chunks[0][1].shape[0]
    s, m, e, o = ([None] * n for _ in range(4))

    def score(j):
        s[j] = lax.dot_general(chunks[j][0], q, _NT, preferred_element_type=F32)
        m[j] = s[j].max(axis=0, keepdims=True)

    for j in range(min(2, n)):
        score(j)
    for j in range(n):
        e[j] = jnp.exp2(s[j] - m[j]).astype(BF16)
        if j + 2 < n:
            score(j + 2)
        vt = chunks[j][1]
        vt_ones = jnp.concatenate([vt, jnp.ones((BF16_SUBLANES, vt.shape[1]), BF16)], axis=0)
        o[j] = jnp.dot(vt_ones, e[j], preferred_element_type=F32)
    if n == 1:
        acc = o[0]
    else:
        m_all = functools.reduce(jnp.maximum, m)
        acc = None
        for j in range(n):
            part = o[j] * jnp.exp2(m[j] - m_all)
            acc = part if acc is None else acc + part
    return acc[:n_v], acc[n_v:n_v + 1]


def _diff_attn_kernel(*refs, has_ctx, lam_init, tq):
    q_ref, g_ref, kh_ref, vt_ref, qn_ref, lamv_ref, sub_ref = refs[:7]
    pos = 7
    if has_ctx:
        tab_ref, ck_ref, cvt_ref = refs[pos:pos + 3]
        pos += 3
    o_ref = refs[pos]

    lv = lamv_ref[...]
    lam = (jnp.exp(jnp.sum(lv[0:1] * lv[1:2], keepdims=True))
           - jnp.exp(jnp.sum(lv[2:3] * lv[3:4], keepdims=True)) + lam_init)

    zero = jnp.zeros((tq, HEAD), BF16)
    rows = []
    for comp in range(2):
        q = _rms(q_ref[:, comp * HEAD:(comp + 1) * HEAD], qn_ref[...])
        if has_ctx:
            q = _rope(q, tab_ref)
        q = (q * ATTN_SCALE_LOG2).astype(BF16)
        rows.append(jnp.concatenate([q, zero] if comp == 0 else [zero, q], axis=1))
    q_bd = jnp.concatenate(rows, axis=0)
    chunks = _key_chunks(kh_ref, vt_ref) + (_key_chunks(ck_ref, cvt_ref) if has_ctx else [])
    acc, l = _attend_chunks(q_bd, chunks)
    o_t = acc[:, :tq] * (1.0 / l[:, :tq]) - acc[:, tq:] * (lam / l[:, tq:])
    o = _rms(o_t.T, sub_ref[...]) * (1.0 - lam_init)
    o_ref[...] = (o * _silu(g_ref[...])).astype(BF16)


def _diff_attention(p, kh, vt, q_norm, lam_vecs, subln, lam_init, n_batch, n_seq, tq, rope_tab=None, ctx=None):
    has_ctx = ctx is not None
    w2 = 2 * HEAD
    nq = n_seq // tq
    g_col0 = 3 * D_MODEL // w2
    in_specs = [
        pl.BlockSpec((tq, w2), lambda b, h, i: (b * nq + i, h)),
        pl.BlockSpec((tq, w2), lambda b, h, i: (b * nq + i, g_col0 + h)),
        pl.BlockSpec((n_seq, w2), lambda b, h, i: (b, h)),
        pl.BlockSpec((None, w2, n_seq), lambda b, h, i: (b, h, 0)),
        pl.BlockSpec((1, HEAD), lambda b, h, i: (0, 0)),
        pl.BlockSpec((4, HEAD), lambda b, h, i: (0, 0)),
        pl.BlockSpec((1, w2), lambda b, h, i: (0, 0)),
    ]
    args = [p, p, kh, vt, q_norm.reshape(1, HEAD), lam_vecs, subln.reshape(1, w2)]
    if has_ctx:
        ck, cvt = ctx
        n_ctx = ck.shape[1]
        in_specs += [
            pl.BlockSpec((3, tq, HEAD), lambda b, h, i: (0, i, 0)),
            pl.BlockSpec((None, n_ctx, w2), lambda b, h, i: (b, 0, h)),
            pl.BlockSpec((None, w2, n_ctx), lambda b, h, i: (b, h, 0)),
        ]
        args += [rope_tab, ck, cvt]
    return pl.pallas_call(
        functools.partial(_diff_attn_kernel, has_ctx=has_ctx, lam_init=lam_init, tq=tq),
        grid=(n_batch, B_HEADS, nq),
        in_specs=in_specs,
        out_specs=pl.BlockSpec((tq, w2), lambda b, h, i: (b * nq + i, h)),
        out_shape=jax.ShapeDtypeStruct((n_batch * n_seq, D_MODEL), BF16),
        compiler_params=_cparams(3),
        name="diff_attention",
    )(*args)


def _gqa_kernel(*refs, has_ctx, tq):
    q_ref, g_ref, kh_ref, vt_ref, qn_ref = refs[:5]
    pos = 5
    if has_ctx:
        tab_ref, ck_ref, cvt_ref = refs[pos:pos + 3]
        pos += 3
    o_ref = refs[pos]

    qs = []
    for gi in range(C_GROUP):
        q = _rms(q_ref[:, gi * HEAD:(gi + 1) * HEAD], qn_ref[...])
        if has_ctx:
            q = _rope(q, tab_ref)
        qs.append((q * ATTN_SCALE_LOG2).astype(BF16))
    q_all = jnp.concatenate(qs, axis=0)
    chunks = _key_chunks(kh_ref, vt_ref) + (_key_chunks(ck_ref, cvt_ref) if has_ctx else [])
    acc, l = _attend_chunks(q_all, chunks)
    o = (acc * (1.0 / l)).T
    for gi in range(C_GROUP):
        sl = slice(gi * HEAD, (gi + 1) * HEAD)
        o_ref[:, sl] = (o[gi * tq:(gi + 1) * tq] * _silu(g_ref[:, sl])).astype(BF16)


def _gqa_attention(p, kh, vt, q_norm, n_batch, n_seq, tq, rope_tab=None, ctx=None):
    has_ctx = ctx is not None
    wq = C_GROUP * HEAD
    nq = n_seq // tq
    g_col0 = (D_MODEL + 2 * C_KV_HEADS * HEAD) // wq
    in_specs = [
        pl.BlockSpec((tq, wq), lambda b, h, i: (b * nq + i, h)),
        pl.BlockSpec((tq, wq), lambda b, h, i: (b * nq + i, g_col0 + h)),
        pl.BlockSpec((n_seq, HEAD), lambda b, h, i: (b, h)),
        pl.BlockSpec((None, HEAD, n_seq), lambda b, h, i: (b, h, 0)),
        pl.BlockSpec((1, HEAD), lambda b, h, i: (0, 0)),
    ]
    args = [p, p, kh, vt, q_norm.reshape(1, HEAD)]
    if has_ctx:
        ck, cvt = ctx
        n_ctx = ck.shape[1]
        in_specs += [
            pl.BlockSpec((3, tq, HEAD), lambda b, h, i: (0, i, 0)),
            pl.BlockSpec((None, n_ctx, HEAD), lambda b, h, i: (b, 0, h)),
            pl.BlockSpec((None, HEAD, n_ctx), lambda b, h, i: (b, h, 0)),
        ]
        args += [rope_tab, ck, cvt]
    return pl.pallas_call(
        functools.partial(_gqa_kernel, has_ctx=has_ctx, tq=tq),
        grid=(n_batch, C_KV_HEADS, nq),
        in_specs=in_specs,
        out_specs=pl.BlockSpec((tq, wq), lambda b, h, i: (b * nq + i, h)),
        out_shape=jax.ShapeDtypeStruct((n_batch * n_seq, D_MODEL), BF16),
        compiler_params=_cparams(3),
        name="gqa_attention",
    )(*args)


def _forget_gate(z, lb):
    t = jnp.exp2(jnp.abs(z) * -LOG2_E)
    u = 1.0 + t
    r = 1.0 / u
    tr = t * r
    pos = z >= 0.0
    sig = jnp.where(pos, r, tr)
    nsig = jnp.where(pos, tr, r)
    log2_sig = jnp.minimum(z, 0.0) * LOG2_E - jnp.log2(u)
    one_m = 1.0 - lb
    log2_f = jnp.where(lb > 0.0, jnp.log2(lb + one_m * sig), log2_sig)
    return log2_f, one_m * nsig


def _cumsum_rows(tri, x):
    hi = x.astype(BF16)
    mid = (x - hi.astype(F32)).astype(BF16)
    r = jnp.dot(tri, jnp.concatenate([hi, mid], axis=1), preferred_element_type=F32)
    return r[:, :HEAD] + r[:, HEAD:]


def _pair_rows(ref, b, odd):
    off = b if odd else 0
    if b >= 8:
        parts = [ref[pl.ds(i * 2 * b + off, b), :] for i in range(SCAN_HALF // b)]
    else:
        parts = [ref[pl.ds(off + r, SCAN_HALF // b, stride=2 * b), :] for r in range(b)]
    return parts[0] if len(parts) == 1 else jnp.concatenate(parts, axis=0)


def _pair_bcast(ref, b, row):
    if b >= 8:
        parts = [jnp.broadcast_to(ref[pl.ds(i * 2 * b + row, 1), :], (b, HEAD)) for i in range(SCAN_HALF // b)]
    else:
        parts = [ref[pl.ds(row, SCAN_HALF // b, stride=2 * b), :]] * b
    return parts[0] if len(parts) == 1 else jnp.concatenate(parts, axis=0)


def _pair_store(ref, b, odd, val):
    off = b if odd else 0
    if b >= 8:
        for i in range(SCAN_HALF // b):
            ref[pl.ds(i * 2 * b + off, b), :] = val[i * b:(i + 1) * b]
    else:
        n = SCAN_HALF // b
        for r in range(b):
            ref[pl.ds(off + r, n, stride=2 * b), :] = val[r * n:(r + 1) * n]


def _hgrn_kernel(*refs, n_chunks, has_state):
    q_ref, zf_ref, zb_ref, v_ref, g_ref, lb_ref, on_ref = refs[:7]
    s0_ref = refs[7] if has_state else None
    pos = 8
    o_ref = refs[pos]
    pos += 1
    sout_ref = None
    if not has_state:
        sout_ref = refs[pos]
        pos += 1
    tri_ref, mask_ref, sq, sv, skf, scf, sxf, skb, scb, sxb = refs[pos:pos + 10]
    pos += 10
    lvl_out = refs[pos:pos + SCAN_LEVELS]
    pos += SCAN_LEVELS
    if has_state:
        oall, qb_all, klb_all, decb_all = refs[pos:pos + 4]

    c = SCAN_CHUNK
    row = lax.broadcasted_iota(jnp.int32, (c, c), 0)
    col = lax.broadcasted_iota(jnp.int32, (c, c), 1)
    tri_ref[0] = (col <= row).astype(BF16)
    tri_ref[1] = (col >= row).astype(BF16)
    prow = lax.broadcasted_iota(jnp.int32, (SCAN_HALF, SCAN_HALF), 0)
    pcol = lax.broadcasted_iota(jnp.int32, (SCAN_HALF, SCAN_HALF), 1)
    for j in range(SCAN_LEVELS):
        b = 1 << j
        if b >= 8:
            same = (prow // b) == (pcol // b)
        else:
            same = (prow % (SCAN_HALF // b)) == (pcol % (SCAN_HALF // b))
        mask_ref[j] = same.astype(F32)

    lb_f = lb_ref[0:1, :]
    lb_b = lb_ref[1:2, :]

    def intra(r):
        q = _silu(q_ref[r, :])
        v = v_ref[r, :]
        lf, kf = _forget_gate(zf_ref[r, :], lb_f)
        lbk, kb = _forget_gate(zb_ref[r, :], lb_b)
        cf = _cumsum_rows(tri_ref[0], lf)
        cb = _cumsum_rows(tri_ref[1], lbk)
        sq[...] = q
        sv[...] = v
        skf[...] = kf
        scf[...] = cf
        sxf[...] = cf - lf
        skb[...] = kb
        scb[...] = cb
        sxb[...] = cb - lbk
        scores = []
        for j in range(SCAN_LEVELS):
            b = 1 << j
            mask = mask_ref[j]
            qt = _pair_rows(sq, b, True) * jnp.exp2(_pair_rows(scf, b, True) - _pair_bcast(sxf, b, b))
            kt = _pair_rows(skf, b, False) * jnp.exp2(_pair_bcast(scf, b, b - 1) - _pair_rows(scf, b, False))
            z = lax.dot_general(qt.astype(BF16), kt.astype(BF16), _NT, preferred_element_type=F32) * mask
            scores.append((j, True, z.astype(BF16)))
            qt = _pair_rows(sq, b, False) * jnp.exp2(_pair_rows(scb, b, False) - _pair_bcast(sxb, b, b - 1))
            kt = _pair_rows(skb, b, True) * jnp.exp2(_pair_bcast(scb, b, b) - _pair_rows(scb, b, True))
            z = lax.dot_general(qt.astype(BF16), kt.astype(BF16), _NT, preferred_element_type=F32) * mask
            scores.append((j, False, z.astype(BF16)))
        for j, q_odd, z in scores:
            b = 1 << j
            _pair_store(lvl_out[j], b, q_odd, jnp.dot(z, _pair_rows(sv, b, not q_odd).astype(BF16),
                                                      preferred_element_type=F32))
        o = jnp.sum(q * (kf + kb), axis=-1, keepdims=True) * v
        for j in range(SCAN_LEVELS):
            o = o + lvl_out[j][...]
        tot_f = cf[c - 1:c, :]
        tot_b = cb[0:1, :]
        klf = kf * jnp.exp2(tot_f - cf)
        klb = kb * jnp.exp2(tot_b - cb)
        return o, q, v, cf, cb, tot_f, tot_b, klf, klb

    def finish(r, o):
        o = _rms(o, on_ref[...]) * _silu(g_ref[r, :])
        o_ref[r, :] = o.astype(BF16)

    def rows_of(ci):
        return pl.ds(pl.multiple_of(ci * c, c), c)

    if not has_state:
        def seq(ci, carry):
            r = rows_of(ci)
            o, q, v, cf, cb, tot_f, tot_b, klf, klb = intra(r)
            v_b = v.astype(BF16)
            sout_ref[ci, 0] = jnp.dot(klf.T.astype(BF16), v_b, preferred_element_type=F32)
            sout_ref[ci, 1] = jnp.dot(klb.T.astype(BF16), v_b, preferred_element_type=F32)
            finish(r, o)
            return carry

        lax.fori_loop(0, n_chunks, seq, 0)
        return

    def row_to_col(x):
        return jnp.broadcast_to(x, (HEAD, HEAD)).T

    def fwd(ci, s):
        r = rows_of(ci)
        o, q, v, cf, cb, tot_f, tot_b, klf, klb = intra(r)
        oall[r, :] = o + jnp.dot((q * jnp.exp2(cf)).astype(BF16), s.astype(BF16), preferred_element_type=F32)
        qb_all[r, :] = (q * jnp.exp2(cb)).astype(BF16)
        klb_all[r, :] = klb
        decb_all[pl.ds(pl.multiple_of(ci * 8, 8), 1), :] = jnp.exp2(tot_b)
        return row_to_col(jnp.exp2(tot_f)) * s + jnp.dot(klf.T.astype(BF16), v.astype(BF16),
                                                         preferred_element_type=F32)

    def bwd(i, s):
        ci = n_chunks - 1 - i
        r = rows_of(ci)
        finish(r, oall[r, :] + jnp.dot(qb_all[r, :], s.astype(BF16), preferred_element_type=F32))
        dec = decb_all[pl.ds(pl.multiple_of(ci * 8, 8), 1), :]
        return row_to_col(dec) * s + jnp.dot(klb_all[r, :].T.astype(BF16), v_ref[r, :].astype(BF16),
                                             preferred_element_type=F32)

    lax.fori_loop(0, n_chunks, fwd, s0_ref[0])
    lax.fori_loop(0, n_chunks, bwd, s0_ref[1], unroll=4)


def _hgrn_scan(p, lb, o_norm, n_batch, n_seq, state=None, final_states=None, slot=0, seqs_per_step=8):
    has_state = state is not None
    c = SCAN_CHUNK
    if has_state:
        rows, n_steps = n_seq, n_batch
    else:
        assert n_seq == c and n_batch % seqs_per_step == 0
        rows, n_steps = seqs_per_step * c, n_batch // seqs_per_step
    n_chunks = rows // c
    col = lambda s: (lambda b, h: (b, s * A_HEADS + h))
    in_specs = [pl.BlockSpec((rows, HEAD), col(s)) for s in range(5)]
    in_specs += [
        pl.BlockSpec((2, HEAD), lambda b, h: (0, h)),
        pl.BlockSpec((1, HEAD), lambda b, h: (0, 0)),
    ]
    args = [p] * 5 + [lb, o_norm.reshape(1, HEAD)]
    out_specs = [pl.BlockSpec((rows, HEAD), lambda b, h: (b, h))]
    out_shape = [jax.ShapeDtypeStruct((n_batch * n_seq, D_MODEL), BF16)]
    scratch = [pltpu.VMEM((2, c, c), BF16), pltpu.VMEM((SCAN_LEVELS, SCAN_HALF, SCAN_HALF), F32)]
    scratch += [pltpu.VMEM((c, HEAD), F32)] * (8 + SCAN_LEVELS)
    if has_state:
        in_specs.append(pl.BlockSpec((None, 2, None, HEAD, HEAD), lambda b, h: (b, 0, h, 0, 0)))
        args.append(state)
        scratch += [pltpu.VMEM((rows, HEAD), F32), pltpu.VMEM((rows, HEAD), BF16),
                    pltpu.VMEM((rows, HEAD), F32), pltpu.VMEM((n_chunks * 8, HEAD), F32)]
    else:
        in_specs.append(pl.BlockSpec(memory_space=pl.ANY))
        args.append(final_states)
        out_specs.append(pl.BlockSpec((n_chunks, None, 2, None, HEAD, HEAD), lambda b, h: (b, slot, 0, h, 0, 0)))
        out_shape.append(jax.ShapeDtypeStruct(final_states.shape, F32))
    res = pl.pallas_call(
        functools.partial(_hgrn_kernel, n_chunks=n_chunks, has_state=has_state),
        grid=(n_steps, A_HEADS),
        in_specs=in_specs,
        out_specs=out_specs,
        out_shape=out_shape,
        input_output_aliases={} if has_state else {len(args) - 1: 1},
        scratch_shapes=scratch,
        compiler_params=_cparams(2),
        name="hgrn2_scan",
    )(*args)
    return (res[0], None) if has_state else (res[0], res[1])


def _diff_lambda_init(layer):
    return 0.8 - 0.6 * math.exp(-0.3 * layer)


def kernel(x_prompt, x_sample, state_a, cache_b_k, cache_b_v, cache_c_k, cache_c_v, c, c_ctx, norm_w, mod_w, mod_b, a_w_in, a_w_out, a_o_norm, a_lower_bound, b_w_in, b_w_out, b_q_norm, b_k_norm, b_lambda, b_subln, c_w_in, c_w_out, c_q_norm, c_k_norm):
    n_ctx_b, n_ctx_s = x_prompt.shape[:2]
    n_lat_b, n_lat_s = x_sample.shape[:2]
    past = cache_b_k.shape[2]
    tm = tn = 1024

    lb_all = jnp.cumsum(jax.nn.softmax(a_lower_bound.astype(F32), axis=0), axis=0)
    lb_all = lb_all - lb_all[0:1]

    cond = jnp.zeros((COND_ROWS, D_MODEL), F32).at[0].set(c_ctx).at[1:1 + n_lat_b].set(c)
    mods = _mod_rows(cond, mod_w, mod_b)
    rope_tab = _rope_tables(n_lat_s)

    lat_tiles = n_lat_s // tm
    groups = [
        dict(x=x_prompt.reshape(-1, D_MODEL), nb=n_ctx_b, ns=n_ctx_s, row=lambda i: 0, latent=False),
        dict(x=x_sample.reshape(-1, D_MODEL), nb=n_lat_b, ns=n_lat_s, row=lambda i: 1 + i // lat_tiles, latent=True),
    ]
    n_a_layers = a_w_in.shape[0]
    new_a = jnp.zeros((n_ctx_b, n_a_layers, 2, A_HEADS, HEAD, HEAD), F32)
    new_bk, new_bv, new_ck, new_cv = [], [], [], []

    for layer in range(DEPTH):
        kind, j = layer % N_MIXERS, layer // N_MIXERS
        w_in = (a_w_in, b_w_in, c_w_in)[kind][j].astype(BF16)
        w_out = (a_w_out, b_w_out, c_w_out)[kind][j].astype(BF16)
        for grp in groups:
            nb, ns, latent = grp["nb"], grp["ns"], grp["latent"]
            p = _in_proj(grp["x"], mods[layer], norm_w[layer], w_in, grp["row"], tm, tn)
            if kind == 0:
                if latent:
                    o, _ = _hgrn_scan(p, lb_all[layer], a_o_norm[j], nb, ns, state=state_a[:, j])
                else:
                    o, new_a = _hgrn_scan(p, lb_all[layer], a_o_norm[j], nb, ns, final_states=new_a, slot=j)
            elif kind == 1:
                tab = rope_tab if latent else None
                res = _kv_prep(p, D_MODEL, D_MODEL, 2 * D_MODEL, D_MODEL, b_k_norm[j], tab, nb, ns)
                ctx = None
                if latent:
                    ctx = (cache_b_k[:, j].reshape(nb, past, D_MODEL).astype(BF16),
                           jnp.swapaxes(cache_b_v[:, j].reshape(nb, past, D_MODEL), 1, 2).astype(BF16))
                else:
                    new_bk.append(res[2].reshape(nb, ns, B_HEADS, 2, HEAD))
                    new_bv.append(res[3].reshape(nb, ns, B_HEADS, 2 * HEAD))
                o = _diff_attention(p, res[0], res[1], b_q_norm[j], b_lambda[j], b_subln[j],
                                    _diff_lambda_init(layer), nb, ns, 512 if latent else 256, rope_tab=tab, ctx=ctx)
            else:
                kvw = C_KV_HEADS * HEAD
                tab = rope_tab if latent else None
                res = _kv_prep(p, D_MODEL, kvw, D_MODEL + kvw, kvw, c_k_norm[j], tab, nb, ns)
                ctx = None
                if latent:
                    ctx = (cache_c_k[:, j].reshape(nb, past, kvw).astype(BF16),
                           jnp.swapaxes(cache_c_v[:, j].reshape(nb, past, kvw), 1, 2).astype(BF16))
                else:
                    new_ck.append(res[2].reshape(nb, ns, C_KV_HEADS, HEAD))
                    new_cv.append(res[3].reshape(nb, ns, C_KV_HEADS, HEAD))
                o = _gqa_attention(p, res[0], res[1], c_q_norm[j], nb, ns, 256, rope_tab=tab, ctx=ctx)
            grp["x"] = _out_proj(o, w_out, grp["x"], mods[layer], grp["row"], tm, tn)

    y_prompt = groups[0]["x"].reshape(x_prompt.shape)
    y_sample = groups[1]["x"].reshape(x_sample.shape)
    return (y_prompt, y_sample, new_a, jnp.stack(new_bk, axis=1), jnp.stack(new_bv, axis=1),
            jnp.stack(new_ck, axis=1), jnp.stack(new_cv, axis=1))
```

```python
import functools
import math

import jax
import jax.numpy as jnp
from jax import lax
from jax.experimental import pallas as pl
from jax.experimental.pallas import tpu as pltpu

F32 = jnp.float32
BF16 = jnp.bfloat16

D_MODEL = 2048
DEPTH = 4
GRID_W = 64
N_MIXERS = 3
EPS = 1e-6
ROPE_THETA = 10000.0
HEAD = 128
A_HEADS = D_MODEL // HEAD
B_HEADS = D_MODEL // (2 * HEAD)
C_HEADS = D_MODEL // HEAD
C_KV_HEADS = C_HEADS // 4
C_GROUP = C_HEADS // C_KV_HEADS
COND_ROWS = 8

V7X_VMEM_LIMIT_BYTES = 56 * 1024 * 1024
SCAN_CHUNK = 256
SCAN_HALF = SCAN_CHUNK // 2
SCAN_LEVELS = SCAN_CHUNK.bit_length() - 1
LOG2_E = math.log2(math.e)
ATTN_KEY_CHUNK = 1024
ATTN_SCALE_LOG2 = (HEAD ** -0.5) * LOG2_E
BF16_SUBLANES = 16


def _cparams(n_axes):
    return pltpu.CompilerParams(
        dimension_semantics=("arbitrary",) * n_axes,
        vmem_limit_bytes=V7X_VMEM_LIMIT_BYTES,
    )


def _sigmoid(x):
    return 1.0 / (1.0 + jnp.exp(-x))


def _silu(x):
    return x * _sigmoid(x)


def _rms(x, w):
    return x * lax.rsqrt(jnp.mean(x * x, axis=-1, keepdims=True) + EPS) * w


def _mod_kernel(cond_ref, w_ref, b_ref, o_ref):
    s = _silu(cond_ref[...]).astype(BF16)
    o_ref[...] = jnp.dot(s, w_ref[...].astype(BF16), preferred_element_type=F32) + b_ref[...]


def _mod_rows(cond, mod_w, mod_b):
    d3 = 3 * D_MODEL
    tn = 768
    out = pl.pallas_call(
        _mod_kernel,
        grid=(DEPTH, d3 // tn),
        in_specs=[
            pl.BlockSpec((COND_ROWS, D_MODEL), lambda l, j: (0, 0)),
            pl.BlockSpec((None, D_MODEL, tn), lambda l, j: (l, 0, j)),
            pl.BlockSpec((None, 1, tn), lambda l, j: (l, 0, j)),
        ],
        out_specs=pl.BlockSpec((None, COND_ROWS, tn), lambda l, j: (l, 0, j)),
        out_shape=jax.ShapeDtypeStruct((DEPTH, COND_ROWS, d3), F32),
        compiler_params=_cparams(2),
        name="adaln_rows",
    )(cond, mod_w, mod_b.reshape(DEPTH, 1, d3))
    return out.reshape(DEPTH, COND_ROWS, 3, D_MODEL)


def _inproj_kernel(x_ref, mod_ref, nw_ref, w_ref, o_ref, h_ref):
    @pl.when(pl.program_id(1) == 0)
    def _():
        xn = _rms(x_ref[...], nw_ref[...])
        h_ref[...] = (xn * (1.0 + mod_ref[1:2, :]) + mod_ref[0:1, :]).astype(BF16)

    o_ref[...] = jnp.dot(h_ref[...], w_ref[...], preferred_element_type=F32).astype(o_ref.dtype)


def _in_proj(x, mod, norm_w, w, row_of_tile, tm, tn):
    t, n = x.shape[0], w.shape[1]
    return pl.pallas_call(
        _inproj_kernel,
        grid=(t // tm, n // tn),
        in_specs=[
            pl.BlockSpec((tm, D_MODEL), lambda i, j: (i, 0)),
            pl.BlockSpec((None, 3, D_MODEL), lambda i, j: (row_of_tile(i), 0, 0)),
            pl.BlockSpec((1, D_MODEL), lambda i, j: (0, 0)),
            pl.BlockSpec((D_MODEL, tn), lambda i, j: (0, j)),
        ],
        out_specs=pl.BlockSpec((tm, tn), lambda i, j: (i, j)),
        out_shape=jax.ShapeDtypeStruct((t, n), F32),
        scratch_shapes=[pltpu.VMEM((tm, D_MODEL), BF16)],
        compiler_params=_cparams(2),
        name="in_proj",
    )(x, mod, norm_w.reshape(1, D_MODEL), w)


def _outproj_kernel(o_ref, w_ref, x_ref, mod_ref, y_ref):
    acc = jnp.dot(o_ref[...], w_ref[...], preferred_element_type=F32)
    y_ref[...] = x_ref[...] + mod_ref[2:3, :] * acc


def _out_proj(o, w, x, mod, row_of_tile, tm, tn):
    t, kdim = o.shape
    return pl.pallas_call(
        _outproj_kernel,
        grid=(t // tm, D_MODEL // tn),
        in_specs=[
            pl.BlockSpec((tm, kdim), lambda i, j: (i, 0)),
            pl.BlockSpec((kdim, tn), lambda i, j: (0, j)),
            pl.BlockSpec((tm, tn), lambda i, j: (i, j)),
            pl.BlockSpec((None, 3, tn), lambda i, j: (row_of_tile(i), 0, j)),
        ],
        out_specs=pl.BlockSpec((tm, tn), lambda i, j: (i, j)),
        out_shape=jax.ShapeDtypeStruct((t, D_MODEL), F32),
        compiler_params=_cparams(2),
        name="out_proj",
    )(o, w, x, mod)


def _rope_tables(n):
    pos = jnp.arange(n)
    quarter = HEAD // 4
    inv_freq = ROPE_THETA ** (-jnp.arange(quarter, dtype=F32) / quarter)

    def axis_angles(p):
        ang = p.astype(F32)[:, None] * inv_freq[None, :]
        return jnp.concatenate([ang, ang], axis=-1)

    ang = jnp.concatenate([axis_angles(pos // GRID_W), axis_angles(pos % GRID_W)], axis=-1)
    cos, sin = jnp.cos(ang), jnp.sin(ang)
    first = (jnp.arange(HEAD) % (2 * quarter)) < quarter
    return jnp.stack([cos, jnp.where(first, -sin, 0.0), jnp.where(first, 0.0, sin)])


def _rope(x, tab_ref, rows=slice(None)):
    up = pltpu.roll(x, HEAD - HEAD // 4, 1)
    down = pltpu.roll(x, HEAD // 4, 1)
    return x * tab_ref[0, rows, :] + up * tab_ref[1, rows, :] + down * tab_ref[2, rows, :]


def _kvprep_kernel(*refs, n_kheads, rope, emit_norm):
    k_ref, v_ref, kn_ref = refs[:3]
    pos = 3
    tab_ref = None
    if rope:
        tab_ref = refs[pos]
        pos += 1
    kh_ref, vt_ref = refs[pos], refs[pos + 1]
    kn_out, v_out = (refs[pos + 2], refs[pos + 3]) if emit_norm else (None, None)
    for h in range(n_kheads):
        sl = slice(h * HEAD, (h + 1) * HEAD)
        kn = _rms(k_ref[:, sl], kn_ref[...])
        if emit_norm:
            kn_out[:, sl] = kn
        if rope:
            kn = _rope(kn, tab_ref)
        kh_ref[:, sl] = kn.astype(BF16)
    v = v_ref[...]
    if emit_norm:
        v_out[...] = v
    vt_ref[...] = v.T.astype(BF16)


def _kv_prep(p, k_col, k_w, v_col, v_w, k_norm, rope_tab, n_batch, n_seq):
    t = p.shape[0]
    rope = rope_tab is not None
    emit_norm = not rope
    tr = min(512, n_seq)
    per_seq = n_seq // tr
    in_specs = [
        pl.BlockSpec((tr, k_w), lambda i: (i, k_col // k_w)),
        pl.BlockSpec((tr, v_w), lambda i: (i, v_col // v_w)),
        pl.BlockSpec((1, HEAD), lambda i: (0, 0)),
    ]
    args = [p, p, k_norm.reshape(1, HEAD)]
    if rope:
        in_specs.append(pl.BlockSpec((3, tr, HEAD), lambda i: (0, i % per_seq, 0)))
        args.append(rope_tab)
    out_specs = [
        pl.BlockSpec((tr, k_w), lambda i: (i, 0)),
        pl.BlockSpec((None, v_w, tr), lambda i: (i // per_seq, 0, i % per_seq)),
    ]
    out_shape = [jax.ShapeDtypeStruct((t, k_w), BF16), jax.ShapeDtypeStruct((n_batch, v_w, n_seq), BF16)]
    if emit_norm:
        out_specs += [pl.BlockSpec((tr, k_w), lambda i: (i, 0)), pl.BlockSpec((tr, v_w), lambda i: (i, 0))]
        out_shape += [jax.ShapeDtypeStruct((t, k_w), F32), jax.ShapeDtypeStruct((t, v_w), F32)]
    return pl.pallas_call(
        functools.partial(_kvprep_kernel, n_kheads=k_w // HEAD, rope=rope, emit_norm=emit_norm),
        grid=(t // tr,),
        in_specs=in_specs,
        out_specs=out_specs,
        out_shape=out_shape,
        compiler_params=_cparams(1),
        name="kv_prep",
    )(*args)


_NT = (((1,), (1,)), ((), ()))


def _key_chunks(k_ref, vt_ref):
    n = k_ref.shape[0]
    step = min(n, ATTN_KEY_CHUNK)
    return [(k_ref[c0:c0 + step, :], vt_ref[:, c0:c0 + step]) for c0 in range(0, n, step)]


def _attend_chunks(q_blocks, chunks, finish):
    n = len(chunks)
    n_v = chunks[0][1].shape[0]
    vt_ones = [jnp.concatenate([vt, jnp.ones((BF16_SUBLANES, vt.shape[1]), BF16)], axis=0) for _, vt in chunks]
    pairs = [(i, j) for i in range(len(q_blocks)) for j in range(n)]
    s, m, o = {}, {}, {}

    def score(t):
        i, j = pairs[t]
        s[t] = lax.dot_general(chunks[j][0], q_blocks[i], _NT, preferred_element_type=F32)
        m[t] = s[t].max(axis=0, keepdims=True)

    for t in range(min(2, len(pairs))):
        score(t)
    for t, (i, j) in enumerate(pairs):
        e = jnp.exp2(s.pop(t) - m[t]).astype(BF16)
        if t + 2 < len(pairs):
            score(t + 2)
        o[t] = jnp.dot(vt_ones[j], e, preferred_element_type=F32)
        if j == n - 1:
            ts = range(t - n + 1, t + 1)
            if n == 1:
                acc = o[t]
            else:
                m_all = functools.reduce(jnp.maximum, [m[u] for u in ts])
                acc = None
                for u in ts:
                    part = o[u] * jnp.exp2(m[u] - m_all)
                    acc = part if acc is None else acc + part
            finish(i, acc[:n_v], acc[n_v:n_v + 1])


def _diff_attn_kernel(*refs, has_ctx, lam_init, tq):
    q_ref, g_ref, kh_ref, vt_ref, qn_ref, lamv_ref, sub_ref = refs[:7]
    pos = 7
    if has_ctx:
        tab_ref, ck_ref, cvt_ref = refs[pos:pos + 3]
        pos += 3
    o_ref = refs[pos]

    lv = lamv_ref[...]
    lam = (jnp.exp(jnp.sum(lv[0:1] * lv[1:2], keepdims=True))
           - jnp.exp(jnp.sum(lv[2:3] * lv[3:4], keepdims=True)) + lam_init)

    zero = jnp.zeros((tq, HEAD), BF16)
    q_blocks = []
    for i in range(q_ref.shape[0] // tq):
        r = slice(i * tq, (i + 1) * tq)
        rows = []
        for comp in range(2):
            q = _rms(q_ref[r, comp * HEAD:(comp + 1) * HEAD], qn_ref[...])
            if has_ctx:
                q = _rope(q, tab_ref, r)
            q = (q * ATTN_SCALE_LOG2).astype(BF16)
            rows.append(jnp.concatenate([q, zero] if comp == 0 else [zero, q], axis=1))
        q_blocks.append(jnp.concatenate(rows, axis=0))

    def finish(i, acc, l):
        r = slice(i * tq, (i + 1) * tq)
        o_t = acc[:, :tq] * (1.0 / l[:, :tq]) - acc[:, tq:] * (lam / l[:, tq:])
        o = _rms(o_t.T, sub_ref[...]) * (1.0 - lam_init)
        o_ref[r, :] = (o * _silu(g_ref[r, :])).astype(BF16)

    chunks = _key_chunks(kh_ref, vt_ref) + (_key_chunks(ck_ref, cvt_ref) if has_ctx else [])
    _attend_chunks(q_blocks, chunks, finish)


def _diff_attention(p, kh, vt, q_norm, lam_vecs, subln, lam_init, n_batch, n_seq, tq, q_blocks=1, rope_tab=None, ctx=None):
    has_ctx = ctx is not None
    w2 = 2 * HEAD
    rows = tq * q_blocks
    nq = n_seq // rows
    g_col0 = 3 * D_MODEL // w2
    in_specs = [
        pl.BlockSpec((rows, w2), lambda b, h, i: (b * nq + i, h)),
        pl.BlockSpec((rows, w2), lambda b, h, i: (b * nq + i, g_col0 + h)),
        pl.BlockSpec((n_seq, w2), lambda b, h, i: (b, h)),
        pl.BlockSpec((None, w2, n_seq), lambda b, h, i: (b, h, 0)),
        pl.BlockSpec((1, HEAD), lambda b, h, i: (0, 0)),
        pl.BlockSpec((4, HEAD), lambda b, h, i: (0, 0)),
        pl.BlockSpec((1, w2), lambda b, h, i: (0, 0)),
    ]
    args = [p, p, kh, vt, q_norm.reshape(1, HEAD), lam_vecs, subln.reshape(1, w2)]
    if has_ctx:
        ck, cvt = ctx
        n_ctx = ck.shape[1]
        in_specs += [
            pl.BlockSpec((3, rows, HEAD), lambda b, h, i: (0, i, 0)),
            pl.BlockSpec((None, n_ctx, w2), lambda b, h, i: (b, 0, h)),
            pl.BlockSpec((None, w2, n_ctx), lambda b, h, i: (b, h, 0)),
        ]
        args += [rope_tab, ck, cvt]
    return pl.pallas_call(
        functools.partial(_diff_attn_kernel, has_ctx=has_ctx, lam_init=lam_init, tq=tq),
        grid=(n_batch, B_HEADS, nq),
        in_specs=in_specs,
        out_specs=pl.BlockSpec((rows, w2), lambda b, h, i: (b * nq + i, h)),
        out_shape=jax.ShapeDtypeStruct((n_batch * n_seq, D_MODEL), BF16),
        compiler_params=_cparams(3),
        name="diff_attention",
    )(*args)


def _gqa_kernel(*refs, has_ctx, tq):
    q_ref, g_ref, kh_ref, vt_ref, qn_ref = refs[:5]
    pos = 5
    if has_ctx:
        tab_ref, ck_ref, cvt_ref = refs[pos:pos + 3]
        pos += 3
    o_ref = refs[pos]

    q_blocks = []
    for i in range(q_ref.shape[0] // tq):
        r = slice(i * tq, (i + 1) * tq)
        qs = []
        for gi in range(C_GROUP):
            q = _rms(q_ref[r, gi * HEAD:(gi + 1) * HEAD], qn_ref[...])
            if has_ctx:
                q = _rope(q, tab_ref, r)
            qs.append((q * ATTN_SCALE_LOG2).astype(BF16))
        q_blocks.append(jnp.concatenate(qs, axis=0))

    def finish(i, acc, l):
        r = slice(i * tq, (i + 1) * tq)
        o = (acc * (1.0 / l)).T
        for gi in range(C_GROUP):
            sl = slice(gi * HEAD, (gi + 1) * HEAD)
            o_ref[r, sl] = (o[gi * tq:(gi + 1) * tq] * _silu(g_ref[r, sl])).astype(BF16)

    chunks = _key_chunks(kh_ref, vt_ref) + (_key_chunks(ck_ref, cvt_ref) if has_ctx else [])
    _attend_chunks(q_blocks, chunks, finish)


def _gqa_attention(p, kh, vt, q_norm, n_batch, n_seq, tq, q_blocks=1, rope_tab=None, ctx=None):
    has_ctx = ctx is not None
    wq = C_GROUP * HEAD
    rows = tq * q_blocks
    nq = n_seq // rows
    g_col0 = (D_MODEL + 2 * C_KV_HEADS * HEAD) // wq
    in_specs = [
        pl.BlockSpec((rows, wq), lambda b, h, i: (b * nq + i, h)),
        pl.BlockSpec((rows, wq), lambda b, h, i: (b * nq + i, g_col0 + h)),
        pl.BlockSpec((n_seq, HEAD), lambda b, h, i: (b, h)),
        pl.BlockSpec((None, HEAD, n_seq), lambda b, h, i: (b, h, 0)),
        pl.BlockSpec((1, HEAD), lambda b, h, i: (0, 0)),
    ]
    args = [p, p, kh, vt, q_norm.reshape(1, HEAD)]
    if has_ctx:
        ck, cvt = ctx
        n_ctx = ck.shape[1]
        in_specs += [
            pl.BlockSpec((3, rows, HEAD), lambda b, h, i: (0, i, 0)),
            pl.BlockSpec((None, n_ctx, HEAD), lambda b, h, i: (b, 0, h)),
            pl.BlockSpec((None, HEAD, n_ctx), lambda b, h, i: (b, h, 0)),
        ]
        args += [rope_tab, ck, cvt]
    return pl.pallas_call(
        functools.partial(_gqa_kernel, has_ctx=has_ctx, tq=tq),
        grid=(n_batch, C_KV_HEADS, nq),
        in_specs=in_specs,
        out_specs=pl.BlockSpec((rows, wq), lambda b, h, i: (b * nq + i, h)),
        out_shape=jax.ShapeDtypeStruct((n_batch * n_seq, D_MODEL), BF16),
        compiler_params=_cparams(3),
        name="gqa_attention",
    )(*args)


def _forget_gate(z, lb):
    t = jnp.exp2(jnp.abs(z) * -LOG2_E)
    u = 1.0 + t
    r = 1.0 / u
    tr = t * r
    pos = z >= 0.0
    sig = jnp.where(pos, r, tr)
    nsig = jnp.where(pos, tr, r)
    log2_sig = jnp.minimum(z, 0.0) * LOG2_E - jnp.log2(u)
    one_m = 1.0 - lb
    log2_f = jnp.where(lb > 0.0, jnp.log2(lb + one_m * sig), log2_sig)
    return log2_f, one_m * nsig


def _cumsum_rows(tri, x):
    hi = x.astype(BF16)
    mid = (x - hi.astype(F32)).astype(BF16)
    r = jnp.dot(tri, jnp.concatenate([hi, mid], axis=1), preferred_element_type=F32)
    return r[:, :HEAD] + r[:, HEAD:]


def _pair_rows(ref, b, odd):
    off = b if odd else 0
    if b >= 8:
        parts = [ref[pl.ds(i * 2 * b + off, b), :] for i in range(SCAN_HALF // b)]
    else:
        parts = [ref[pl.ds(off + r, SCAN_HALF // b, stride=2 * b), :] for r in range(b)]
    return parts[0] if len(parts) == 1 else jnp.concatenate(parts, axis=0)


def _pair_bcast(ref, b, row):
    if b >= 8:
        parts = [jnp.broadcast_to(ref[pl.ds(i * 2 * b + row, 1), :], (b, HEAD)) for i in range(SCAN_HALF // b)]
    else:
        parts = [ref[pl.ds(row, SCAN_HALF // b, stride=2 * b), :]] * b
    return parts[0] if len(parts) == 1 else jnp.concatenate(parts, axis=0)


def _pair_store(ref, b, odd, val):
    off = b if odd else 0
    if b >= 8:
        for i in range(SCAN_HALF // b):
            ref[pl.ds(i * 2 * b + off, b), :] = val[i * b:(i + 1) * b]
    else:
        n = SCAN_HALF // b
        for r in range(b):
            ref[pl.ds(off + r, n, stride=2 * b), :] = val[r * n:(r + 1) * n]


def _hgrn_kernel(*refs, n_chunks, has_state):
    q_ref, zf_ref, zb_ref, v_ref, g_ref, lb_ref, on_ref = refs[:7]
    s0_ref = refs[7] if has_state else None
    pos = 8
    o_ref = refs[pos]
    pos += 1
    sout_ref = None
    if not has_state:
        sout_ref = refs[pos]
        pos += 1
    tri_ref, mask_ref, sq, sv, skf, scf, sxf, skb, scb, sxb = refs[pos:pos + 10]
    pos += 10
    lvl_out = refs[pos:pos + SCAN_LEVELS]
    pos += SCAN_LEVELS
    if has_state:
        oall, qb_all, klb_all, decb_all = refs[pos:pos + 4]

    c = SCAN_CHUNK
    row = lax.broadcasted_iota(jnp.int32, (c, c), 0)
    col = lax.broadcasted_iota(jnp.int32, (c, c), 1)
    tri_ref[0] = (col <= row).astype(BF16)
    tri_ref[1] = (col >= row).astype(BF16)
    prow = lax.broadcasted_iota(jnp.int32, (SCAN_HALF, SCAN_HALF), 0)
    pcol = lax.broadcasted_iota(jnp.int32, (SCAN_HALF, SCAN_HALF), 1)
    for j in range(SCAN_LEVELS):
        b = 1 << j
        if b >= 8:
            same = (prow // b) == (pcol // b)
        else:
            same = (prow % (SCAN_HALF // b)) == (pcol % (SCAN_HALF // b))
        mask_ref[j] = same.astype(F32)

    lb_f = lb_ref[0:1, :]
    lb_b = lb_ref[1:2, :]

    def intra(r):
        q = _silu(q_ref[r, :])
        v = v_ref[r, :]
        lf, kf = _forget_gate(zf_ref[r, :], lb_f)
        lbk, kb = _forget_gate(zb_ref[r, :], lb_b)
        cf = _cumsum_rows(tri_ref[0], lf)
        cb = _cumsum_rows(tri_ref[1], lbk)
        sq[...] = q
        sv[...] = v
        skf[...] = kf
        scf[...] = cf
        sxf[...] = cf - lf
        skb[...] = kb
        scb[...] = cb
        sxb[...] = cb - lbk
        scores = []
        for j in range(SCAN_LEVELS):
            b = 1 << j
            mask = mask_ref[j]
            qt = _pair_rows(sq, b, True) * jnp.exp2(_pair_rows(scf, b, True) - _pair_bcast(sxf, b, b))
            kt = _pair_rows(skf, b, False) * jnp.exp2(_pair_bcast(scf, b, b - 1) - _pair_rows(scf, b, False))
            z = lax.dot_general(qt.astype(BF16), kt.astype(BF16), _NT, preferred_element_type=F32) * mask
            scores.append((j, True, z.astype(BF16)))
            qt = _pair_rows(sq, b, False) * jnp.exp2(_pair_rows(scb, b, False) - _pair_bcast(sxb, b, b - 1))
            kt = _pair_rows(skb, b, True) * jnp.exp2(_pair_bcast(scb, b, b) - _pair_rows(scb, b, True))
            z = lax.dot_general(qt.astype(BF16), kt.astype(BF16), _NT, preferred_element_type=F32) * mask
            scores.append((j, False, z.astype(BF16)))
        for j, q_odd, z in scores:
            b = 1 << j
            _pair_store(lvl_out[j], b, q_odd, jnp.dot(z, _pair_rows(sv, b, not q_odd).astype(BF16),
                                                      preferred_element_type=F32))
        o = jnp.sum(q * (kf + kb), axis=-1, keepdims=True) * v
        for j in range(SCAN_LEVELS):
            o = o + lvl_out[j][...]
        tot_f = cf[c - 1:c, :]
        tot_b = cb[0:1, :]
        klf = kf * jnp.exp2(tot_f - cf)
        klb = kb * jnp.exp2(tot_b - cb)
        return o, q, v, cf, cb, tot_f, tot_b, klf, klb

    def finish(r, o):
        o = _rms(o, on_ref[...]) * _silu(g_ref[r, :])
        o_ref[r, :] = o.astype(BF16)

    def rows_of(ci):
        return pl.ds(pl.multiple_of(ci * c, c), c)

    if not has_state:
        def seq(ci, carry):
            r = rows_of(ci)
            o, q, v, cf, cb, tot_f, tot_b, klf, klb = intra(r)
            v_b = v.astype(BF16)
            sout_ref[ci, 0] = jnp.dot(klf.T.astype(BF16), v_b, preferred_element_type=F32)
            sout_ref[ci, 1] = jnp.dot(klb.T.astype(BF16), v_b, preferred_element_type=F32)
            finish(r, o)
            return carry

        lax.fori_loop(0, n_chunks, seq, 0)
        return

    def row_to_col(x):
        return jnp.broadcast_to(x, (HEAD, HEAD)).T

    def fwd(ci, s):
        r = rows_of(ci)
        o, q, v, cf, cb, tot_f, tot_b, klf, klb = intra(r)
        oall[r, :] = o + jnp.dot((q * jnp.exp2(cf)).astype(BF16), s.astype(BF16), preferred_element_type=F32)
        qb_all[r, :] = (q * jnp.exp2(cb)).astype(BF16)
        klb_all[r, :] = klb
        decb_all[pl.ds(pl.multiple_of(ci * 8, 8), 1), :] = jnp.exp2(tot_b)
        return row_to_col(jnp.exp2(tot_f)) * s + jnp.dot(klf.T.astype(BF16), v.astype(BF16),
                                                         preferred_element_type=F32)

    def bwd(i, s):
        ci = n_chunks - 1 - i
        r = rows_of(ci)
        finish(r, oall[r, :] + jnp.dot(qb_all[r, :], s.astype(BF16), preferred_element_type=F32))
        dec = decb_all[pl.ds(pl.multiple_of(ci * 8, 8), 1), :]
        return row_to_col(dec) * s + jnp.dot(klb_all[r, :].T.astype(BF16), v_ref[r, :].astype(BF16),
                                             preferred_element_type=F32)

    lax.fori_loop(0, n_chunks, fwd, s0_ref[0])
    lax.fori_loop(0, n_chunks, bwd, s0_ref[1], unroll=4)


def _hgrn_scan(p, lb, o_norm, n_batch, n_seq, state=None, final_states=None, slot=0, seqs_per_step=8):
    has_state = state is not None
    c = SCAN_CHUNK
    if has_state:
        rows, n_steps = n_seq, n_batch
    else:
        assert n_seq == c and n_batch % seqs_per_step == 0
        rows, n_steps = seqs_per_step * c, n_batch // seqs_per_step
    n_chunks = rows // c
    col = lambda s: (lambda b, h: (b, s * A_HEADS + h))
    in_specs = [pl.BlockSpec((rows, HEAD), col(s)) for s in range(5)]
    in_specs += [
        pl.BlockSpec((2, HEAD), lambda b, h: (0, h)),
        pl.BlockSpec((1, HEAD), lambda b, h: (0, 0)),
    ]
    args = [p] * 5 + [lb, o_norm.reshape(1, HEAD)]
    out_specs = [pl.BlockSpec((rows, HEAD), lambda b, h: (b, h))]
    out_shape = [jax.ShapeDtypeStruct((n_batch * n_seq, D_MODEL), BF16)]
    scratch = [pltpu.VMEM((2, c, c), BF16), pltpu.VMEM((SCAN_LEVELS, SCAN_HALF, SCAN_HALF), F32)]
    scratch += [pltpu.VMEM((c, HEAD), F32)] * (8 + SCAN_LEVELS)
    if has_state:
        in_specs.append(pl.BlockSpec((None, 2, None, HEAD, HEAD), lambda b, h: (b, 0, h, 0, 0)))
        args.append(state)
        scratch += [pltpu.VMEM((rows, HEAD), F32), pltpu.VMEM((rows, HEAD), BF16),
                    pltpu.VMEM((rows, HEAD), F32), pltpu.VMEM((n_chunks * 8, HEAD), F32)]
    else:
        in_specs.append(pl.BlockSpec(memory_space=pl.ANY))
        args.append(final_states)
        out_specs.append(pl.BlockSpec((n_chunks, None, 2, None, HEAD, HEAD), lambda b, h: (b, slot, 0, h, 0, 0)))
        out_shape.append(jax.ShapeDtypeStruct(final_states.shape, F32))
    res = pl.pallas_call(
        functools.partial(_hgrn_kernel, n_chunks=n_chunks, has_state=has_state),
        grid=(n_steps, A_HEADS),
        in_specs=in_specs,
        out_specs=out_specs,
        out_shape=out_shape,
        input_output_aliases={} if has_state else {len(args) - 1: 1},
        scratch_shapes=scratch,
        compiler_params=_cparams(2),
        name="hgrn2_scan",
    )(*args)
    return (res[0], None) if has_state else (res[0], res[1])


def _diff_lambda_init(layer):
    return 0.8 - 0.6 * math.exp(-0.3 * layer)


def kernel(x_prompt, x_sample, state_a, cache_b_k, cache_b_v, cache_c_k, cache_c_v, c, c_ctx, norm_w, mod_w, mod_b, a_w_in, a_w_out, a_o_norm, a_lower_bound, b_w_in, b_w_out, b_q_norm, b_k_norm, b_lambda, b_subln, c_w_in, c_w_out, c_q_norm, c_k_norm):
    n_ctx_b, n_ctx_s = x_prompt.shape[:2]
    n_lat_b, n_lat_s = x_sample.shape[:2]
    past = cache_b_k.shape[2]
    tm = tn = 1024

    lb_all = jnp.cumsum(jax.nn.softmax(a_lower_bound.astype(F32), axis=0), axis=0)
    lb_all = lb_all - lb_all[0:1]

    cond = jnp.zeros((COND_ROWS, D_MODEL), F32).at[0].set(c_ctx).at[1:1 + n_lat_b].set(c)
    mods = _mod_rows(cond, mod_w, mod_b)
    rope_tab = _rope_tables(n_lat_s)

    lat_tiles = n_lat_s // tm
    groups = [
        dict(x=x_prompt.reshape(-1, D_MODEL), nb=n_ctx_b, ns=n_ctx_s, row=lambda i: 0, latent=False),
        dict(x=x_sample.reshape(-1, D_MODEL), nb=n_lat_b, ns=n_lat_s, row=lambda i: 1 + i // lat_tiles, latent=True),
    ]
    n_a_layers = a_w_in.shape[0]
    new_a = jnp.zeros((n_ctx_b, n_a_layers, 2, A_HEADS, HEAD, HEAD), F32)
    new_bk, new_bv, new_ck, new_cv = [], [], [], []

    for layer in range(DEPTH):
        kind, j = layer % N_MIXERS, layer // N_MIXERS
        w_in = (a_w_in, b_w_in, c_w_in)[kind][j].astype(BF16)
        w_out = (a_w_out, b_w_out, c_w_out)[kind][j].astype(BF16)
        for grp in groups:
            nb, ns, latent = grp["nb"], grp["ns"], grp["latent"]
            p = _in_proj(grp["x"], mods[layer], norm_w[layer], w_in, grp["row"], tm, tn)
            if kind == 0:
                if latent:
                    o, _ = _hgrn_scan(p, lb_all[layer], a_o_norm[j], nb, ns, state=state_a[:, j])
                else:
                    o, new_a = _hgrn_scan(p, lb_all[layer], a_o_norm[j], nb, ns, final_states=new_a, slot=j)
            elif kind == 1:
                tab = rope_tab if latent else None
                res = _kv_prep(p, D_MODEL, D_MODEL, 2 * D_MODEL, D_MODEL, b_k_norm[j], tab, nb, ns)
                ctx = None
                if latent:
                    ctx = (cache_b_k[:, j].reshape(nb, past, D_MODEL).astype(BF16),
                           jnp.swapaxes(cache_b_v[:, j].reshape(nb, past, D_MODEL), 1, 2).astype(BF16))
                else:
                    new_bk.append(res[2].reshape(nb, ns, B_HEADS, 2, HEAD))
                    new_bv.append(res[3].reshape(nb, ns, B_HEADS, 2 * HEAD))
                o = _diff_attention(p, res[0], res[1], b_q_norm[j], b_lambda[j], b_subln[j],
                                    _diff_lambda_init(layer), nb, ns, 512 if latent else 256,
                                    q_blocks=4 if latent else 1, rope_tab=tab, ctx=ctx)
            else:
                kvw = C_KV_HEADS * HEAD
                tab = rope_tab if latent else None
                res = _kv_prep(p, D_MODEL, kvw, D_MODEL + kvw, kvw, c_k_norm[j], tab, nb, ns)
                ctx = None
                if latent:
                    ctx = (cache_c_k[:, j].reshape(nb, past, kvw).astype(BF16),
                           jnp.swapaxes(cache_c_v[:, j].reshape(nb, past, kvw), 1, 2).astype(BF16))
                else:
                    new_ck.append(res[2].reshape(nb, ns, C_KV_HEADS, HEAD))
                    new_cv.append(res[3].reshape(nb, ns, C_KV_HEADS, HEAD))
                o = _gqa_attention(p, res[0], res[1], c_q_norm[j], nb, ns, 256, q_blocks=4 if latent else 1,
                                   rope_tab=tab, ctx=ctx)
            grp["x"] = _out_proj(o, w_out, grp["x"], mods[layer], grp["row"], tm, tn)

    y_prompt = groups[0]["x"].reshape(x_prompt.shape)
    y_sample = groups[1]["x"].reshape(x_sample.shape)
    return (y_prompt, y_sample, new_a, jnp.stack(new_bk, axis=1), jnp.stack(new_bv, axis=1),
            jnp.stack(new_ck, axis=1), jnp.stack(new_cv, axis=1))
```

```python
import functools
import math

import jax
import jax.numpy as jnp
from jax import lax
from jax.experimental import pallas as pl
from jax.experimental.pallas import tpu as pltpu

F32 = jnp.float32
BF16 = jnp.bfloat16

D_MODEL = 2048
DEPTH = 4
GRID_W = 64
N_MIXERS = 3
EPS = 1e-6
ROPE_THETA = 10000.0
HEAD = 128
A_HEADS = D_MODEL // HEAD
B_HEADS = D_MODEL // (2 * HEAD)
C_HEADS = D_MODEL // HEAD
C_KV_HEADS = C_HEADS // 4
C_GROUP = C_HEADS // C_KV_HEADS
COND_ROWS = 8

V7X_VMEM_LIMIT_BYTES = 56 * 1024 * 1024
SCAN_CHUNK = 256
SCAN_HALF = SCAN_CHUNK // 2
SCAN_LEVELS = SCAN_CHUNK.bit_length() - 1
SCAN_BUFS = 8 + SCAN_LEVELS
SCAN_CHUNKS_PER_ITER = 2
LOG2_E = math.log2(math.e)
ATTN_KEY_CHUNK = 1024
ATTN_SCALE_LOG2 = (HEAD ** -0.5) * LOG2_E
BF16_SUBLANES = 16


def _cparams(n_axes):
    return pltpu.CompilerParams(
        dimension_semantics=("arbitrary",) * n_axes,
        vmem_limit_bytes=V7X_VMEM_LIMIT_BYTES,
    )


def _sigmoid(x):
    return 1.0 / (1.0 + jnp.exp(-x))


def _silu(x):
    return x * _sigmoid(x)


def _rms(x, w):
    return x * lax.rsqrt(jnp.mean(x * x, axis=-1, keepdims=True) + EPS) * w


def _mod_kernel(cond_ref, w_ref, b_ref, o_ref):
    s = _silu(cond_ref[...]).astype(BF16)
    o_ref[...] = jnp.dot(s, w_ref[...].astype(BF16), preferred_element_type=F32) + b_ref[...]


def _mod_rows(cond, mod_w, mod_b):
    d3 = 3 * D_MODEL
    tn = 768
    out = pl.pallas_call(
        _mod_kernel,
        grid=(DEPTH, d3 // tn),
        in_specs=[
            pl.BlockSpec((COND_ROWS, D_MODEL), lambda l, j: (0, 0)),
            pl.BlockSpec((None, D_MODEL, tn), lambda l, j: (l, 0, j)),
            pl.BlockSpec((None, 1, tn), lambda l, j: (l, 0, j)),
        ],
        out_specs=pl.BlockSpec((None, COND_ROWS, tn), lambda l, j: (l, 0, j)),
        out_shape=jax.ShapeDtypeStruct((DEPTH, COND_ROWS, d3), F32),
        compiler_params=_cparams(2),
        name="adaln_rows",
    )(cond, mod_w, mod_b.reshape(DEPTH, 1, d3))
    return out.reshape(DEPTH, COND_ROWS, 3, D_MODEL)


def _inproj_kernel(x_ref, mod_ref, nw_ref, w_ref, o_ref, h_ref):
    @pl.when(pl.program_id(1) == 0)
    def _():
        xn = _rms(x_ref[...], nw_ref[...])
        h_ref[...] = (xn * (1.0 + mod_ref[1:2, :]) + mod_ref[0:1, :]).astype(BF16)

    o_ref[...] = jnp.dot(h_ref[...], w_ref[...], preferred_element_type=F32).astype(o_ref.dtype)


def _in_proj(x, mod, norm_w, w, row_of_tile, tm, tn):
    t, n = x.shape[0], w.shape[1]
    return pl.pallas_call(
        _inproj_kernel,
        grid=(t // tm, n // tn),
        in_specs=[
            pl.BlockSpec((tm, D_MODEL), lambda i, j: (i, 0)),
            pl.BlockSpec((None, 3, D_MODEL), lambda i, j: (row_of_tile(i), 0, 0)),
            pl.BlockSpec((1, D_MODEL), lambda i, j: (0, 0)),
            pl.BlockSpec((D_MODEL, tn), lambda i, j: (0, j)),
        ],
        out_specs=pl.BlockSpec((tm, tn), lambda i, j: (i, j)),
        out_shape=jax.ShapeDtypeStruct((t, n), F32),
        scratch_shapes=[pltpu.VMEM((tm, D_MODEL), BF16)],
        compiler_params=_cparams(2),
        name="in_proj",
    )(x, mod, norm_w.reshape(1, D_MODEL), w)


def _outproj_kernel(o_ref, w_ref, x_ref, mod_ref, y_ref):
    acc = jnp.dot(o_ref[...], w_ref[...], preferred_element_type=F32)
    y_ref[...] = x_ref[...] + mod_ref[2:3, :] * acc


def _out_proj(o, w, x, mod, row_of_tile, tm, tn):
    t, kdim = o.shape
    return pl.pallas_call(
        _outproj_kernel,
        grid=(t // tm, D_MODEL // tn),
        in_specs=[
            pl.BlockSpec((tm, kdim), lambda i, j: (i, 0)),
            pl.BlockSpec((kdim, tn), lambda i, j: (0, j)),
            pl.BlockSpec((tm, tn), lambda i, j: (i, j)),
            pl.BlockSpec((None, 3, tn), lambda i, j: (row_of_tile(i), 0, j)),
        ],
        out_specs=pl.BlockSpec((tm, tn), lambda i, j: (i, j)),
        out_shape=jax.ShapeDtypeStruct((t, D_MODEL), F32),
        compiler_params=_cparams(2),
        name="out_proj",
    )(o, w, x, mod)


def _rope_tables(n):
    pos = jnp.arange(n)
    quarter = HEAD // 4
    inv_freq = ROPE_THETA ** (-jnp.arange(quarter, dtype=F32) / quarter)

    def axis_angles(p):
        ang = p.astype(F32)[:, None] * inv_freq[None, :]
        return jnp.concatenate([ang, ang], axis=-1)

    ang = jnp.concatenate([axis_angles(pos // GRID_W), axis_angles(pos % GRID_W)], axis=-1)
    cos, sin = jnp.cos(ang), jnp.sin(ang)
    first = (jnp.arange(HEAD) % (2 * quarter)) < quarter
    return jnp.stack([cos, jnp.where(first, -sin, 0.0), jnp.where(first, 0.0, sin)])


def _rope(x, tab_ref, rows=slice(None)):
    up = pltpu.roll(x, HEAD - HEAD // 4, 1)
    down = pltpu.roll(x, HEAD // 4, 1)
    return x * tab_ref[0, rows, :] + up * tab_ref[1, rows, :] + down * tab_ref[2, rows, :]


def _kvprep_kernel(*refs, n_kheads, rope, emit_norm):
    k_ref, v_ref, kn_ref = refs[:3]
    pos = 3
    tab_ref = None
    if rope:
        tab_ref = refs[pos]
        pos += 1
    kh_ref, vt_ref = refs[pos], refs[pos + 1]
    kn_out, v_out = (refs[pos + 2], refs[pos + 3]) if emit_norm else (None, None)
    for h in range(n_kheads):
        sl = slice(h * HEAD, (h + 1) * HEAD)
        kn = _rms(k_ref[:, sl], kn_ref[...])
        if emit_norm:
            kn_out[:, sl] = kn
        if rope:
            kn = _rope(kn, tab_ref)
        kh_ref[:, sl] = kn.astype(BF16)
    v = v_ref[...]
    if emit_norm:
        v_out[...] = v
    vt_ref[...] = v.T.astype(BF16)


def _kv_prep(p, k_col, k_w, v_col, v_w, k_norm, rope_tab, n_batch, n_seq):
    t = p.shape[0]
    rope = rope_tab is not None
    emit_norm = not rope
    tr = min(512, n_seq)
    per_seq = n_seq // tr
    in_specs = [
        pl.BlockSpec((tr, k_w), lambda i: (i, k_col // k_w)),
        pl.BlockSpec((tr, v_w), lambda i: (i, v_col // v_w)),
        pl.BlockSpec((1, HEAD), lambda i: (0, 0)),
    ]
    args = [p, p, k_norm.reshape(1, HEAD)]
    if rope:
        in_specs.append(pl.BlockSpec((3, tr, HEAD), lambda i: (0, i % per_seq, 0)))
        args.append(rope_tab)
    out_specs = [
        pl.BlockSpec((tr, k_w), lambda i: (i, 0)),
        pl.BlockSpec((None, v_w, tr), lambda i: (i // per_seq, 0, i % per_seq)),
    ]
    out_shape = [jax.ShapeDtypeStruct((t, k_w), BF16), jax.ShapeDtypeStruct((n_batch, v_w, n_seq), BF16)]
    if emit_norm:
        out_specs += [pl.BlockSpec((tr, k_w), lambda i: (i, 0)), pl.BlockSpec((tr, v_w), lambda i: (i, 0))]
        out_shape += [jax.ShapeDtypeStruct((t, k_w), F32), jax.ShapeDtypeStruct((t, v_w), F32)]
    return pl.pallas_call(
        functools.partial(_kvprep_kernel, n_kheads=k_w // HEAD, rope=rope, emit_norm=emit_norm),
        grid=(t // tr,),
        in_specs=in_specs,
        out_specs=out_specs,
        out_shape=out_shape,
        compiler_params=_cparams(1),
        name="kv_prep",
    )(*args)


_NT = (((1,), (1,)), ((), ()))


def _key_chunks(k_ref, vt_ref):
    n = k_ref.shape[0]
    step = min(n, ATTN_KEY_CHUNK)
    return [(k_ref[c0:c0 + step, :], vt_ref[:, c0:c0 + step]) for c0 in range(0, n, step)]


def _attend_chunks(q_blocks, chunks, finish):
    n = len(chunks)
    n_v = chunks[0][1].shape[0]
    vt_ones = [jnp.concatenate([vt, jnp.ones((BF16_SUBLANES, vt.shape[1]), BF16)], axis=0) for _, vt in chunks]
    pairs = [(i, j) for i in range(len(q_blocks)) for j in range(n)]
    s, m, o = {}, {}, {}

    def score(t):
        i, j = pairs[t]
        s[t] = lax.dot_general(chunks[j][0], q_blocks[i], _NT, preferred_element_type=F32)
        m[t] = s[t].max(axis=0, keepdims=True)

    for t in range(min(2, len(pairs))):
        score(t)
    for t, (i, j) in enumerate(pairs):
        e = jnp.exp2(s.pop(t) - m[t]).astype(BF16)
        if t + 2 < len(pairs):
            score(t + 2)
        o[t] = jnp.dot(vt_ones[j], e, preferred_element_type=F32)
        if j == n - 1:
            ts = range(t - n + 1, t + 1)
            if n == 1:
                acc = o[t]
            else:
                m_all = functools.reduce(jnp.maximum, [m[u] for u in ts])
                acc = None
                for u in ts:
                    part = o[u] * jnp.exp2(m[u] - m_all)
                    acc = part if acc is None else acc + part
            finish(i, acc[:n_v], acc[n_v:n_v + 1])


def _diff_attn_kernel(*refs, has_ctx, lam_init, tq):
    q_ref, g_ref, kh_ref, vt_ref, qn_ref, lamv_ref, sub_ref = refs[:7]
    pos = 7
    if has_ctx:
        tab_ref, ck_ref, cvt_ref = refs[pos:pos + 3]
        pos += 3
    o_ref = refs[pos]

    lv = lamv_ref[...]
    lam = (jnp.exp(jnp.sum(lv[0:1] * lv[1:2], keepdims=True))
           - jnp.exp(jnp.sum(lv[2:3] * lv[3:4], keepdims=True)) + lam_init)

    zero = jnp.zeros((tq, HEAD), BF16)
    q_blocks = []
    for i in range(q_ref.shape[0] // tq):
        r = slice(i * tq, (i + 1) * tq)
        rows = []
        for comp in range(2):
            q = _rms(q_ref[r, comp * HEAD:(comp + 1) * HEAD], qn_ref[...])
            if has_ctx:
                q = _rope(q, tab_ref, r)
            q = (q * ATTN_SCALE_LOG2).astype(BF16)
            rows.append(jnp.concatenate([q, zero] if comp == 0 else [zero, q], axis=1))
        q_blocks.append(jnp.concatenate(rows, axis=0))

    def finish(i, acc, l):
        r = slice(i * tq, (i + 1) * tq)
        o_t = acc[:, :tq] * (1.0 / l[:, :tq]) - acc[:, tq:] * (lam / l[:, tq:])
        o = _rms(o_t.T, sub_ref[...]) * (1.0 - lam_init)
        o_ref[r, :] = (o * _silu(g_ref[r, :])).astype(BF16)

    chunks = _key_chunks(kh_ref, vt_ref) + (_key_chunks(ck_ref, cvt_ref) if has_ctx else [])
    _attend_chunks(q_blocks, chunks, finish)


def _diff_attention(p, kh, vt, q_norm, lam_vecs, subln, lam_init, n_batch, n_seq, tq, q_blocks=1, rope_tab=None, ctx=None):
    has_ctx = ctx is not None
    w2 = 2 * HEAD
    rows = tq * q_blocks
    nq = n_seq // rows
    g_col0 = 3 * D_MODEL // w2
    in_specs = [
        pl.BlockSpec((rows, w2), lambda b, h, i: (b * nq + i, h)),
        pl.BlockSpec((rows, w2), lambda b, h, i: (b * nq + i, g_col0 + h)),
        pl.BlockSpec((n_seq, w2), lambda b, h, i: (b, h)),
        pl.BlockSpec((None, w2, n_seq), lambda b, h, i: (b, h, 0)),
        pl.BlockSpec((1, HEAD), lambda b, h, i: (0, 0)),
        pl.BlockSpec((4, HEAD), lambda b, h, i: (0, 0)),
        pl.BlockSpec((1, w2), lambda b, h, i: (0, 0)),
    ]
    args = [p, p, kh, vt, q_norm.reshape(1, HEAD), lam_vecs, subln.reshape(1, w2)]
    if has_ctx:
        ck, cvt = ctx
        n_ctx = ck.shape[1]
        in_specs += [
            pl.BlockSpec((3, rows, HEAD), lambda b, h, i: (0, i, 0)),
            pl.BlockSpec((None, n_ctx, w2), lambda b, h, i: (b, 0, h)),
            pl.BlockSpec((None, w2, n_ctx), lambda b, h, i: (b, h, 0)),
        ]
        args += [rope_tab, ck, cvt]
    return pl.pallas_call(
        functools.partial(_diff_attn_kernel, has_ctx=has_ctx, lam_init=lam_init, tq=tq),
        grid=(n_batch, B_HEADS, nq),
        in_specs=in_specs,
        out_specs=pl.BlockSpec((rows, w2), lambda b, h, i: (b * nq + i, h)),
        out_shape=jax.ShapeDtypeStruct((n_batch * n_seq, D_MODEL), BF16),
        compiler_params=_cparams(3),
        name="diff_attention",
    )(*args)


def _gqa_kernel(*refs, has_ctx, tq):
    q_ref, g_ref, kh_ref, vt_ref, qn_ref = refs[:5]
    pos = 5
    if has_ctx:
        tab_ref, ck_ref, cvt_ref = refs[pos:pos + 3]
        pos += 3
    o_ref = refs[pos]

    q_blocks = []
    for i in range(q_ref.shape[0] // tq):
        r = slice(i * tq, (i + 1) * tq)
        qs = []
        for gi in range(C_GROUP):
            q = _rms(q_ref[r, gi * HEAD:(gi + 1) * HEAD], qn_ref[...])
            if has_ctx:
                q = _rope(q, tab_ref, r)
            qs.append((q * ATTN_SCALE_LOG2).astype(BF16))
        q_blocks.append(jnp.concatenate(qs, axis=0))

    def finish(i, acc, l):
        r = slice(i * tq, (i + 1) * tq)
        o = (acc * (1.0 / l)).T
        for gi in range(C_GROUP):
            sl = slice(gi * HEAD, (gi + 1) * HEAD)
            o_ref[r, sl] = (o[gi * tq:(gi + 1) * tq] * _silu(g_ref[r, sl])).astype(BF16)

    chunks = _key_chunks(kh_ref, vt_ref) + (_key_chunks(ck_ref, cvt_ref) if has_ctx else [])
    _attend_chunks(q_blocks, chunks, finish)


def _gqa_attention(p, kh, vt, q_norm, n_batch, n_seq, tq, q_blocks=1, rope_tab=None, ctx=None):
    has_ctx = ctx is not None
    wq = C_GROUP * HEAD
    rows = tq * q_blocks
    nq = n_seq // rows
    g_col0 = (D_MODEL + 2 * C_KV_HEADS * HEAD) // wq
    in_specs = [
        pl.BlockSpec((rows, wq), lambda b, h, i: (b * nq + i, h)),
        pl.BlockSpec((rows, wq), lambda b, h, i: (b * nq + i, g_col0 + h)),
        pl.BlockSpec((n_seq, HEAD), lambda b, h, i: (b, h)),
        pl.BlockSpec((None, HEAD, n_seq), lambda b, h, i: (b, h, 0)),
        pl.BlockSpec((1, HEAD), lambda b, h, i: (0, 0)),
    ]
    args = [p, p, kh, vt, q_norm.reshape(1, HEAD)]
    if has_ctx:
        ck, cvt = ctx
        n_ctx = ck.shape[1]
        in_specs += [
            pl.BlockSpec((3, rows, HEAD), lambda b, h, i: (0, i, 0)),
            pl.BlockSpec((None, n_ctx, HEAD), lambda b, h, i: (b, 0, h)),
            pl.BlockSpec((None, HEAD, n_ctx), lambda b, h, i: (b, h, 0)),
        ]
        args += [rope_tab, ck, cvt]
    return pl.pallas_call(
        functools.partial(_gqa_kernel, has_ctx=has_ctx, tq=tq),
        grid=(n_batch, C_KV_HEADS, nq),
        in_specs=in_specs,
        out_specs=pl.BlockSpec((rows, wq), lambda b, h, i: (b * nq + i, h)),
        out_shape=jax.ShapeDtypeStruct((n_batch * n_seq, D_MODEL), BF16),
        compiler_params=_cparams(3),
        name="gqa_attention",
    )(*args)


def _forget_gate(z, lb):
    t = jnp.exp2(jnp.abs(z) * -LOG2_E)
    u = 1.0 + t
    r = 1.0 / u
    tr = t * r
    pos = z >= 0.0
    sig = jnp.where(pos, r, tr)
    nsig = jnp.where(pos, tr, r)
    log2_sig = jnp.minimum(z, 0.0) * LOG2_E - jnp.log2(u)
    one_m = 1.0 - lb
    log2_f = jnp.where(lb > 0.0, jnp.log2(lb + one_m * sig), log2_sig)
    return log2_f, one_m * nsig


def _cumsum_rows(tri, x):
    hi = x.astype(BF16)
    mid = (x - hi.astype(F32)).astype(BF16)
    r = jnp.dot(tri, jnp.concatenate([hi, mid], axis=1), preferred_element_type=F32)
    return r[:, :HEAD] + r[:, HEAD:]


def _pair_rows(ref, b, odd):
    off = b if odd else 0
    if b >= 8:
        parts = [ref[pl.ds(i * 2 * b + off, b), :] for i in range(SCAN_HALF // b)]
    else:
        parts = [ref[pl.ds(off + r, SCAN_HALF // b, stride=2 * b), :] for r in range(b)]
    return parts[0] if len(parts) == 1 else jnp.concatenate(parts, axis=0)


def _pair_bcast(ref, b, row):
    if b >= 8:
        parts = [jnp.broadcast_to(ref[pl.ds(i * 2 * b + row, 1), :], (b, HEAD)) for i in range(SCAN_HALF // b)]
    else:
        parts = [ref[pl.ds(row, SCAN_HALF // b, stride=2 * b), :]] * b
    return parts[0] if len(parts) == 1 else jnp.concatenate(parts, axis=0)


def _pair_store(ref, b, odd, val):
    off = b if odd else 0
    if b >= 8:
        for i in range(SCAN_HALF // b):
            ref[pl.ds(i * 2 * b + off, b), :] = val[i * b:(i + 1) * b]
    else:
        n = SCAN_HALF // b
        for r in range(b):
            ref[pl.ds(off + r, n, stride=2 * b), :] = val[r * n:(r + 1) * n]


def _hgrn_kernel(*refs, n_chunks, has_state):
    q_ref, zf_ref, zb_ref, v_ref, g_ref, lb_ref, on_ref = refs[:7]
    s0_ref = refs[7] if has_state else None
    pos = 8
    o_ref = refs[pos]
    pos += 1
    sout_ref = None
    if not has_state:
        sout_ref = refs[pos]
        pos += 1
    tri_ref, mask_ref = refs[pos:pos + 2]
    pos += 2
    buf_sets = [refs[pos + k * SCAN_BUFS:pos + (k + 1) * SCAN_BUFS] for k in range(SCAN_CHUNKS_PER_ITER)]
    pos += SCAN_CHUNKS_PER_ITER * SCAN_BUFS
    if has_state:
        oall, qb_all, klb_all, decb_all = refs[pos:pos + 4]

    c = SCAN_CHUNK
    row = lax.broadcasted_iota(jnp.int32, (c, c), 0)
    col = lax.broadcasted_iota(jnp.int32, (c, c), 1)
    tri_ref[0] = (col <= row).astype(BF16)
    tri_ref[1] = (col >= row).astype(BF16)
    prow = lax.broadcasted_iota(jnp.int32, (SCAN_HALF, SCAN_HALF), 0)
    pcol = lax.broadcasted_iota(jnp.int32, (SCAN_HALF, SCAN_HALF), 1)
    for j in range(SCAN_LEVELS):
        b = 1 << j
        if b >= 8:
            same = (prow // b) == (pcol // b)
        else:
            same = (prow % (SCAN_HALF // b)) == (pcol % (SCAN_HALF // b))
        mask_ref[j] = same.astype(F32)

    lb_f = lb_ref[0:1, :]
    lb_b = lb_ref[1:2, :]

    def gates(r, bufs):
        sq, sv, skf, scf, sxf, skb, scb, sxb = bufs[:8]
        q = _silu(q_ref[r, :])
        v = v_ref[r, :]
        lf, kf = _forget_gate(zf_ref[r, :], lb_f)
        lbk, kb = _forget_gate(zb_ref[r, :], lb_b)
        cf = _cumsum_rows(tri_ref[0], lf)
        cb = _cumsum_rows(tri_ref[1], lbk)
        sq[...] = q
        sv[...] = v
        skf[...] = kf
        scf[...] = cf
        sxf[...] = cf - lf
        skb[...] = kb
        scb[...] = cb
        sxb[...] = cb - lbk
        return q, v, kf, kb, cf, cb

    def pair_scores(bufs):
        sq, sv, skf, scf, sxf, skb, scb, sxb = bufs[:8]
        scores = []
        for j in range(SCAN_LEVELS):
            b = 1 << j
            mask = mask_ref[j]
            qt = _pair_rows(sq, b, True) * jnp.exp2(_pair_rows(scf, b, True) - _pair_bcast(sxf, b, b))
            kt = _pair_rows(skf, b, False) * jnp.exp2(_pair_bcast(scf, b, b - 1) - _pair_rows(scf, b, False))
            z = lax.dot_general(qt.astype(BF16), kt.astype(BF16), _NT, preferred_element_type=F32) * mask
            scores.append((j, True, z.astype(BF16)))
            qt = _pair_rows(sq, b, False) * jnp.exp2(_pair_rows(scb, b, False) - _pair_bcast(sxb, b, b - 1))
            kt = _pair_rows(skb, b, True) * jnp.exp2(_pair_bcast(scb, b, b) - _pair_rows(scb, b, True))
            z = lax.dot_general(qt.astype(BF16), kt.astype(BF16), _NT, preferred_element_type=F32) * mask
            scores.append((j, False, z.astype(BF16)))
        return scores

    def pair_values(bufs, scores):
        sv, lvl_out = bufs[1], bufs[8:]
        for j, q_odd, z in scores:
            b = 1 << j
            _pair_store(lvl_out[j], b, q_odd, jnp.dot(z, _pair_rows(sv, b, not q_odd).astype(BF16),
                                                      preferred_element_type=F32))

    def chunk_result(bufs, g):
        q, v, kf, kb, cf, cb = g
        o = jnp.sum(q * (kf + kb), axis=-1, keepdims=True) * v
        for lvl in bufs[8:]:
            o = o + lvl[...]
        tot_f = cf[c - 1:c, :]
        tot_b = cb[0:1, :]
        klf = kf * jnp.exp2(tot_f - cf)
        klb = kb * jnp.exp2(tot_b - cb)
        return o, q, v, cf, cb, tot_f, tot_b, klf, klb

    def intra(rs):
        g = [gates(r, bufs) for r, bufs in zip(rs, buf_sets)]
        scores = [pair_scores(bufs) for bufs in buf_sets]
        for bufs, sc in zip(buf_sets, scores):
            pair_values(bufs, sc)
        return [chunk_result(bufs, gk) for bufs, gk in zip(buf_sets, g)]

    def finish(r, o):
        o = _rms(o, on_ref[...]) * _silu(g_ref[r, :])
        o_ref[r, :] = o.astype(BF16)

    def rows_of(ci):
        return pl.ds(pl.multiple_of(ci * c, c), c)

    per_iter = SCAN_CHUNKS_PER_ITER
    n_iters = n_chunks // per_iter

    if not has_state:
        def seqs(it, carry):
            cis = [it * per_iter + k for k in range(per_iter)]
            for ci, res in zip(cis, intra([rows_of(ci) for ci in cis])):
                o, q, v, cf, cb, tot_f, tot_b, klf, klb = res
                v_b = v.astype(BF16)
                sout_ref[ci, 0] = jnp.dot(klf.T.astype(BF16), v_b, preferred_element_type=F32)
                sout_ref[ci, 1] = jnp.dot(klb.T.astype(BF16), v_b, preferred_element_type=F32)
                finish(rows_of(ci), o)
            return carry

        lax.fori_loop(0, n_iters, seqs, 0)
        return

    def row_to_col(x):
        return jnp.broadcast_to(x, (HEAD, HEAD)).T

    def fwd(it, s):
        cis = [it * per_iter + k for k in range(per_iter)]
        for ci, res in zip(cis, intra([rows_of(ci) for ci in cis])):
            o, q, v, cf, cb, tot_f, tot_b, klf, klb = res
            r = rows_of(ci)
            oall[r, :] = o + jnp.dot((q * jnp.exp2(cf)).astype(BF16), s.astype(BF16),
                                     preferred_element_type=F32)
            qb_all[r, :] = (q * jnp.exp2(cb)).astype(BF16)
            klb_all[r, :] = klb
            decb_all[pl.ds(pl.multiple_of(ci * 8, 8), 1), :] = jnp.exp2(tot_b)
            s = row_to_col(jnp.exp2(tot_f)) * s + jnp.dot(klf.T.astype(BF16), v.astype(BF16),
                                                          preferred_element_type=F32)
        return s

    def bwd(i, s):
        ci = n_chunks - 1 - i
        r = rows_of(ci)
        finish(r, oall[r, :] + jnp.dot(qb_all[r, :], s.astype(BF16), preferred_element_type=F32))
        dec = decb_all[pl.ds(pl.multiple_of(ci * 8, 8), 1), :]
        return row_to_col(dec) * s + jnp.dot(klb_all[r, :].T.astype(BF16), v_ref[r, :].astype(BF16),
                                             preferred_element_type=F32)

    lax.fori_loop(0, n_iters, fwd, s0_ref[0])
    lax.fori_loop(0, n_chunks, bwd, s0_ref[1], unroll=4)


def _hgrn_scan(p, lb, o_norm, n_batch, n_seq, state=None, final_states=None, slot=0, seqs_per_step=8):
    has_state = state is not None
    c = SCAN_CHUNK
    if has_state:
        rows, n_steps = n_seq, n_batch
    else:
        assert n_seq == c and n_batch % seqs_per_step == 0
        rows, n_steps = seqs_per_step * c, n_batch // seqs_per_step
    n_chunks = rows // c
    col = lambda s: (lambda b, h: (b, s * A_HEADS + h))
    in_specs = [pl.BlockSpec((rows, HEAD), col(s)) for s in range(5)]
    in_specs += [
        pl.BlockSpec((2, HEAD), lambda b, h: (0, h)),
        pl.BlockSpec((1, HEAD), lambda b, h: (0, 0)),
    ]
    args = [p] * 5 + [lb, o_norm.reshape(1, HEAD)]
    out_specs = [pl.BlockSpec((rows, HEAD), lambda b, h: (b, h))]
    out_shape = [jax.ShapeDtypeStruct((n_batch * n_seq, D_MODEL), BF16)]
    scratch = [pltpu.VMEM((2, c, c), BF16), pltpu.VMEM((SCAN_LEVELS, SCAN_HALF, SCAN_HALF), F32)]
    assert n_chunks % SCAN_CHUNKS_PER_ITER == 0
    scratch += [pltpu.VMEM((c, HEAD), F32)] * (SCAN_CHUNKS_PER_ITER * SCAN_BUFS)
    if has_state:
        in_specs.append(pl.BlockSpec((None, 2, None, HEAD, HEAD), lambda b, h: (b, 0, h, 0, 0)))
        args.append(state)
        scratch += [pltpu.VMEM((rows, HEAD), F32), pltpu.VMEM((rows, HEAD), BF16),
                    pltpu.VMEM((rows, HEAD), F32), pltpu.VMEM((n_chunks * 8, HEAD), F32)]
    else:
        in_specs.append(pl.BlockSpec(memory_space=pl.ANY))
        args.append(final_states)
        out_specs.append(pl.BlockSpec((n_chunks, None, 2, None, HEAD, HEAD), lambda b, h: (b, slot, 0, h, 0, 0)))
        out_shape.append(jax.ShapeDtypeStruct(final_states.shape, F32))
    res = pl.pallas_call(
        functools.partial(_hgrn_kernel, n_chunks=n_chunks, has_state=has_state),
        grid=(n_steps, A_HEADS),
        in_specs=in_specs,
        out_specs=out_specs,
        out_shape=out_shape,
        input_output_aliases={} if has_state else {len(args) - 1: 1},
        scratch_shapes=scratch,
        compiler_params=_cparams(2),
        name="hgrn2_scan",
    )(*args)
    return (res[0], None) if has_state else (res[0], res[1])


def _diff_lambda_init(layer):
    return 0.8 - 0.6 * math.exp(-0.3 * layer)


def kernel(x_prompt, x_sample, state_a, cache_b_k, cache_b_v, cache_c_k, cache_c_v, c, c_ctx, norm_w, mod_w, mod_b, a_w_in, a_w_out, a_o_norm, a_lower_bound, b_w_in, b_w_out, b_q_norm, b_k_norm, b_lambda, b_subln, c_w_in, c_w_out, c_q_norm, c_k_norm):
    n_ctx_b, n_ctx_s = x_prompt.shape[:2]
    n_lat_b, n_lat_s = x_sample.shape[:2]
    past = cache_b_k.shape[2]
    tm = tn = 1024

    lb_all = jnp.cumsum(jax.nn.softmax(a_lower_bound.astype(F32), axis=0), axis=0)
    lb_all = lb_all - lb_all[0:1]

    cond = jnp.zeros((COND_ROWS, D_MODEL), F32).at[0].set(c_ctx).at[1:1 + n_lat_b].set(c)
    mods = _mod_rows(cond, mod_w, mod_b)
    rope_tab = _rope_tables(n_lat_s)

    lat_tiles = n_lat_s // tm
    groups = [
        dict(x=x_prompt.reshape(-1, D_MODEL), nb=n_ctx_b, ns=n_ctx_s, row=lambda i: 0, latent=False),
        dict(x=x_sample.reshape(-1, D_MODEL), nb=n_lat_b, ns=n_lat_s, row=lambda i: 1 + i // lat_tiles, latent=True),
    ]
    n_a_layers = a_w_in.shape[0]
    new_a = jnp.zeros((n_ctx_b, n_a_layers, 2, A_HEADS, HEAD, HEAD), F32)
    new_bk, new_bv, new_ck, new_cv = [], [], [], []

    for layer in range(DEPTH):
        kind, j = layer % N_MIXERS, layer // N_MIXERS
        w_in = (a_w_in, b_w_in, c_w_in)[kind][j].astype(BF16)
        w_out = (a_w_out, b_w_out, c_w_out)[kind][j].astype(BF16)
        for grp in groups:
            nb, ns, latent = grp["nb"], grp["ns"], grp["latent"]
            p = _in_proj(grp["x"], mods[layer], norm_w[layer], w_in, grp["row"], tm, tn)
            if kind == 0:
                if latent:
                    o, _ = _hgrn_scan(p, lb_all[layer], a_o_norm[j], nb, ns, state=state_a[:, j])
                else:
                    o, new_a = _hgrn_scan(p, lb_all[layer], a_o_norm[j], nb, ns, final_states=new_a, slot=j)
            elif kind == 1:
                tab = rope_tab if latent else None
                res = _kv_prep(p, D_MODEL, D_MODEL, 2 * D_MODEL, D_MODEL, b_k_norm[j], tab, nb, ns)
                ctx = None
                if latent:
                    ctx = (cache_b_k[:, j].reshape(nb, past, D_MODEL).astype(BF16),
                           jnp.swapaxes(cache_b_v[:, j].reshape(nb, past, D_MODEL), 1, 2).astype(BF16))
                else:
                    new_bk.append(res[2].reshape(nb, ns, B_HEADS, 2, HEAD))
                    new_bv.append(res[3].reshape(nb, ns, B_HEADS, 2 * HEAD))
                o = _diff_attention(p, res[0], res[1], b_q_norm[j], b_lambda[j], b_subln[j],
                                    _diff_lambda_init(layer), nb, ns, 512 if latent else 256,
                                    q_blocks=4 if latent else 1, rope_tab=tab, ctx=ctx)
            else:
                kvw = C_KV_HEADS * HEAD
                tab = rope_tab if latent else None
                res = _kv_prep(p, D_MODEL, kvw, D_MODEL + kvw, kvw, c_k_norm[j], tab, nb, ns)
                ctx = None
                if latent:
                    ctx = (cache_c_k[:, j].reshape(nb, past, kvw).astype(BF16),
                           jnp.swapaxes(cache_c_v[:, j].reshape(nb, past, kvw), 1, 2).astype(BF16))
                else:
                    new_ck.append(res[2].reshape(nb, ns, C_KV_HEADS, HEAD))
                    new_cv.append(res[3].reshape(nb, ns, C_KV_HEADS, HEAD))
                o = _gqa_attention(p, res[0], res[1], c_q_norm[j], nb, ns, 256, q_blocks=4 if latent else 1,
                                   rope_tab=tab, ctx=ctx)
            grp["x"] = _out_proj(o, w_out, grp["x"], mods[layer], grp["row"], tm, tn)

    y_prompt = groups[0]["x"].reshape(x_prompt.shape)
    y_sample = groups[1]["x"].reshape(x_sample.shape)
    return (y_prompt, y_sample, new_a, jnp.stack(new_bk, axis=1), jnp.stack(new_bv, axis=1),
            jnp.stack(new_ck, axis=1), jnp.stack(new_cv, axis=1))
```

```python
import functools
import math

import jax
import jax.numpy as jnp
from jax import lax
from jax.experimental import pallas as pl
from jax.experimental.pallas import tpu as pltpu

F32 = jnp.float32
BF16 = jnp.bfloat16

D_MODEL = 2048
DEPTH = 4
GRID_W = 64
N_MIXERS = 3
EPS = 1e-6
ROPE_THETA = 10000.0
HEAD = 128
A_HEADS = D_MODEL // HEAD
B_HEADS = D_MODEL // (2 * HEAD)
C_HEADS = D_MODEL // HEAD
C_KV_HEADS = C_HEADS // 4
C_GROUP = C_HEADS // C_KV_HEADS
COND_ROWS = 8

V7X_VMEM_LIMIT_BYTES = 56 * 1024 * 1024
SCAN_CHUNK = 256
SCAN_HALF = SCAN_CHUNK // 2
SCAN_LEVELS = SCAN_CHUNK.bit_length() - 1
SCAN_BUFS = 8 + SCAN_LEVELS
SCAN_CHUNKS_PER_ITER = 2
LOG2_E = math.log2(math.e)
ATTN_KEY_CHUNK = 1024
ATTN_SCALE_LOG2 = (HEAD ** -0.5) * LOG2_E
BF16_SUBLANES = 16


def _cparams(n_axes):
    return pltpu.CompilerParams(
        dimension_semantics=("arbitrary",) * n_axes,
        vmem_limit_bytes=V7X_VMEM_LIMIT_BYTES,
    )


def _sigmoid(x):
    return 1.0 / (1.0 + jnp.exp(-x))


def _silu(x):
    return x * _sigmoid(x)


def _rms(x, w):
    return x * lax.rsqrt(jnp.mean(x * x, axis=-1, keepdims=True) + EPS) * w


def _mod_kernel(cond_ref, w_ref, b_ref, o_ref):
    s = _silu(cond_ref[...]).astype(BF16)
    o_ref[...] = jnp.dot(s, w_ref[...].astype(BF16), preferred_element_type=F32) + b_ref[...]


def _mod_rows(cond, mod_w, mod_b):
    d3 = 3 * D_MODEL
    tn = 768
    out = pl.pallas_call(
        _mod_kernel,
        grid=(DEPTH, d3 // tn),
        in_specs=[
            pl.BlockSpec((COND_ROWS, D_MODEL), lambda l, j: (0, 0)),
            pl.BlockSpec((None, D_MODEL, tn), lambda l, j: (l, 0, j)),
            pl.BlockSpec((None, 1, tn), lambda l, j: (l, 0, j)),
        ],
        out_specs=pl.BlockSpec((None, COND_ROWS, tn), lambda l, j: (l, 0, j)),
        out_shape=jax.ShapeDtypeStruct((DEPTH, COND_ROWS, d3), F32),
        compiler_params=_cparams(2),
        name="adaln_rows",
    )(cond, mod_w, mod_b.reshape(DEPTH, 1, d3))
    return out.reshape(DEPTH, COND_ROWS, 3, D_MODEL)


def _inproj_kernel(x_ref, mod_ref, nw_ref, w_ref, o_ref, h_ref):
    @pl.when(pl.program_id(1) == 0)
    def _():
        xn = _rms(x_ref[...], nw_ref[...])
        h_ref[...] = (xn * (1.0 + mod_ref[1:2, :]) + mod_ref[0:1, :]).astype(BF16)

    o_ref[...] = jnp.dot(h_ref[...], w_ref[...], preferred_element_type=F32).astype(o_ref.dtype)


def _in_proj(x, mod, norm_w, w, row_of_tile, tm, tn):
    t, n = x.shape[0], w.shape[1]
    return pl.pallas_call(
        _inproj_kernel,
        grid=(t // tm, n // tn),
        in_specs=[
            pl.BlockSpec((tm, D_MODEL), lambda i, j: (i, 0)),
            pl.BlockSpec((None, 3, D_MODEL), lambda i, j: (row_of_tile(i), 0, 0)),
            pl.BlockSpec((1, D_MODEL), lambda i, j: (0, 0)),
            pl.BlockSpec((D_MODEL, tn), lambda i, j: (0, j)),
        ],
        out_specs=pl.BlockSpec((tm, tn), lambda i, j: (i, j)),
        out_shape=jax.ShapeDtypeStruct((t, n), F32),
        scratch_shapes=[pltpu.VMEM((tm, D_MODEL), BF16)],
        compiler_params=_cparams(2),
        name="in_proj",
    )(x, mod, norm_w.reshape(1, D_MODEL), w)


def _outproj_kernel(o_ref, w_ref, x_ref, mod_ref, y_ref):
    acc = jnp.dot(o_ref[...], w_ref[...], preferred_element_type=F32)
    y_ref[...] = x_ref[...] + mod_ref[2:3, :] * acc


def _out_proj(o, w, x, mod, row_of_tile, tm, tn):
    t, kdim = o.shape
    return pl.pallas_call(
        _outproj_kernel,
        grid=(t // tm, D_MODEL // tn),
        in_specs=[
            pl.BlockSpec((tm, kdim), lambda i, j: (i, 0)),
            pl.BlockSpec((kdim, tn), lambda i, j: (0, j)),
            pl.BlockSpec((tm, tn), lambda i, j: (i, j)),
            pl.BlockSpec((None, 3, tn), lambda i, j: (row_of_tile(i), 0, j)),
        ],
        out_specs=pl.BlockSpec((tm, tn), lambda i, j: (i, j)),
        out_shape=jax.ShapeDtypeStruct((t, D_MODEL), F32),
        compiler_params=_cparams(2),
        name="out_proj",
    )(o, w, x, mod)


def _rope_tables(n):
    pos = jnp.arange(n)
    quarter = HEAD // 4
    inv_freq = ROPE_THETA ** (-jnp.arange(quarter, dtype=F32) / quarter)

    def axis_angles(p):
        ang = p.astype(F32)[:, None] * inv_freq[None, :]
        return jnp.concatenate([ang, ang], axis=-1)

    ang = jnp.concatenate([axis_angles(pos // GRID_W), axis_angles(pos % GRID_W)], axis=-1)
    cos, sin = jnp.cos(ang), jnp.sin(ang)
    first = (jnp.arange(HEAD) % (2 * quarter)) < quarter
    return jnp.stack([cos, jnp.where(first, -sin, 0.0), jnp.where(first, 0.0, sin)])


def _rope(x, tab_ref, rows=slice(None)):
    up = pltpu.roll(x, HEAD - HEAD // 4, 1)
    down = pltpu.roll(x, HEAD // 4, 1)
    return x * tab_ref[0, rows, :] + up * tab_ref[1, rows, :] + down * tab_ref[2, rows, :]


def _kvprep_kernel(*refs, n_kheads, rope, emit_norm):
    k_ref, v_ref, kn_ref = refs[:3]
    pos = 3
    tab_ref = None
    if rope:
        tab_ref = refs[pos]
        pos += 1
    kh_ref, vt_ref = refs[pos], refs[pos + 1]
    kn_out, v_out = (refs[pos + 2], refs[pos + 3]) if emit_norm else (None, None)
    for h in range(n_kheads):
        sl = slice(h * HEAD, (h + 1) * HEAD)
        kn = _rms(k_ref[:, sl], kn_ref[...])
        if emit_norm:
            kn_out[:, sl] = kn
        if rope:
            kn = _rope(kn, tab_ref)
        kh_ref[:, sl] = kn.astype(BF16)
    v = v_ref[...]
    if emit_norm:
        v_out[...] = v
    vt_ref[...] = v.T.astype(BF16)


def _kv_prep(p, k_col, k_w, v_col, v_w, k_norm, rope_tab, n_batch, n_seq):
    t = p.shape[0]
    rope = rope_tab is not None
    emit_norm = not rope
    tr = min(512, n_seq)
    per_seq = n_seq // tr
    in_specs = [
        pl.BlockSpec((tr, k_w), lambda i: (i, k_col // k_w)),
        pl.BlockSpec((tr, v_w), lambda i: (i, v_col // v_w)),
        pl.BlockSpec((1, HEAD), lambda i: (0, 0)),
    ]
    args = [p, p, k_norm.reshape(1, HEAD)]
    if rope:
        in_specs.append(pl.BlockSpec((3, tr, HEAD), lambda i: (0, i % per_seq, 0)))
        args.append(rope_tab)
    out_specs = [
        pl.BlockSpec((tr, k_w), lambda i: (i, 0)),
        pl.BlockSpec((None, v_w, tr), lambda i: (i // per_seq, 0, i % per_seq)),
    ]
    out_shape = [jax.ShapeDtypeStruct((t, k_w), BF16), jax.ShapeDtypeStruct((n_batch, v_w, n_seq), BF16)]
    if emit_norm:
        out_specs += [pl.BlockSpec((tr, k_w), lambda i: (i, 0)), pl.BlockSpec((tr, v_w), lambda i: (i, 0))]
        out_shape += [jax.ShapeDtypeStruct((t, k_w), F32), jax.ShapeDtypeStruct((t, v_w), F32)]
    return pl.pallas_call(
        functools.partial(_kvprep_kernel, n_kheads=k_w // HEAD, rope=rope, emit_norm=emit_norm),
        grid=(t // tr,),
        in_specs=in_specs,
        out_specs=out_specs,
        out_shape=out_shape,
        compiler_params=_cparams(1),
        name="kv_prep",
    )(*args)


_NT = (((1,), (1,)), ((), ()))


def _key_chunks(k_ref, vt_ref, head, k_w, v_w):
    n = k_ref.shape[0]
    step = min(n, ATTN_KEY_CHUNK)
    ones = jnp.ones((BF16_SUBLANES, step), BF16)
    return [(k_ref[c0:c0 + step, head * k_w:(head + 1) * k_w],
             jnp.concatenate([vt_ref[head * v_w:(head + 1) * v_w, c0:c0 + step], ones], axis=0))
            for c0 in range(0, n, step)]


def _attend_chunks(problems, finish):
    pairs = [(i, j) for i, (_, chunks) in enumerate(problems) for j in range(len(chunks))]
    s, m, o = {}, {}, {}

    def score(t):
        i, j = pairs[t]
        q, chunks = problems[i]
        s[t] = lax.dot_general(chunks[j][0], q, _NT, preferred_element_type=F32)
        m[t] = s[t].max(axis=0, keepdims=True)

    for t in range(min(2, len(pairs))):
        score(t)
    for t, (i, j) in enumerate(pairs):
        e = jnp.exp2(s.pop(t) - m[t]).astype(BF16)
        if t + 2 < len(pairs):
            score(t + 2)
        chunks = problems[i][1]
        n = len(chunks)
        n_v = chunks[j][1].shape[0] - BF16_SUBLANES
        o[t] = jnp.dot(chunks[j][1], e, preferred_element_type=F32)
        if j == n - 1:
            ts = range(t - n + 1, t + 1)
            if n == 1:
                acc = o[t]
            else:
                m_all = functools.reduce(jnp.maximum, [m[u] for u in ts])
                acc = None
                for u in ts:
                    part = o[u] * jnp.exp2(m[u] - m_all)
                    acc = part if acc is None else acc + part
            finish(i, acc[:n_v], acc[n_v:n_v + 1])


def _diff_attn_kernel(*refs, has_ctx, lam_init, tq, heads):
    q_ref, g_ref, kh_ref, vt_ref, qn_ref, lamv_ref, sub_ref = refs[:7]
    w2 = 2 * HEAD
    pos = 7
    if has_ctx:
        tab_ref, ck_ref, cvt_ref = refs[pos:pos + 3]
        pos += 3
    o_ref = refs[pos]

    lv = lamv_ref[...]
    lam = (jnp.exp(jnp.sum(lv[0:1] * lv[1:2], keepdims=True))
           - jnp.exp(jnp.sum(lv[2:3] * lv[3:4], keepdims=True)) + lam_init)

    zero = jnp.zeros((tq, HEAD), BF16)
    n_blk = q_ref.shape[0] // tq
    problems = []
    for h in range(heads):
        chunks = _key_chunks(kh_ref, vt_ref, h, w2, w2) + (_key_chunks(ck_ref, cvt_ref, h, w2, w2) if has_ctx else [])
        for i in range(n_blk):
            r = slice(i * tq, (i + 1) * tq)
            rows = []
            for comp in range(2):
                q = _rms(q_ref[r, h * w2 + comp * HEAD:h * w2 + (comp + 1) * HEAD], qn_ref[...])
                if has_ctx:
                    q = _rope(q, tab_ref, r)
                q = (q * ATTN_SCALE_LOG2).astype(BF16)
                rows.append(jnp.concatenate([q, zero] if comp == 0 else [zero, q], axis=1))
            problems.append((jnp.concatenate(rows, axis=0), chunks))

    def finish(idx, acc, l):
        h, i = divmod(idx, n_blk)
        r = slice(i * tq, (i + 1) * tq)
        cols = slice(h * w2, (h + 1) * w2)
        o_t = acc[:, :tq] * (1.0 / l[:, :tq]) - acc[:, tq:] * (lam / l[:, tq:])
        o = _rms(o_t.T, sub_ref[...]) * (1.0 - lam_init)
        o_ref[r, cols] = (o * _silu(g_ref[r, cols])).astype(BF16)

    _attend_chunks(problems, finish)


def _diff_attention(p, kh, vt, q_norm, lam_vecs, subln, lam_init, n_batch, n_seq, tq, q_blocks=1, heads=1,
                    rope_tab=None, ctx=None):
    has_ctx = ctx is not None
    w2 = 2 * HEAD
    wc = heads * w2
    rows = tq * q_blocks
    nq = n_seq // rows
    assert (3 * D_MODEL) % wc == 0 and B_HEADS % heads == 0
    g_col0 = 3 * D_MODEL // wc
    in_specs = [
        pl.BlockSpec((rows, wc), lambda b, h, i: (b * nq + i, h)),
        pl.BlockSpec((rows, wc), lambda b, h, i: (b * nq + i, g_col0 + h)),
        pl.BlockSpec((n_seq, wc), lambda b, h, i: (b, h)),
        pl.BlockSpec((None, wc, n_seq), lambda b, h, i: (b, h, 0)),
        pl.BlockSpec((1, HEAD), lambda b, h, i: (0, 0)),
        pl.BlockSpec((4, HEAD), lambda b, h, i: (0, 0)),
        pl.BlockSpec((1, w2), lambda b, h, i: (0, 0)),
    ]
    args = [p, p, kh, vt, q_norm.reshape(1, HEAD), lam_vecs, subln.reshape(1, w2)]
    if has_ctx:
        ck, cvt = ctx
        n_ctx = ck.shape[1]
        in_specs += [
            pl.BlockSpec((3, rows, HEAD), lambda b, h, i: (0, i, 0)),
            pl.BlockSpec((None, n_ctx, wc), lambda b, h, i: (b, 0, h)),
            pl.BlockSpec((None, wc, n_ctx), lambda b, h, i: (b, h, 0)),
        ]
        args += [rope_tab, ck, cvt]
    return pl.pallas_call(
        functools.partial(_diff_attn_kernel, has_ctx=has_ctx, lam_init=lam_init, tq=tq, heads=heads),
        grid=(n_batch, B_HEADS // heads, nq),
        in_specs=in_specs,
        out_specs=pl.BlockSpec((rows, wc), lambda b, h, i: (b * nq + i, h)),
        out_shape=jax.ShapeDtypeStruct((n_batch * n_seq, D_MODEL), BF16),
        compiler_params=_cparams(3),
        name="diff_attention",
    )(*args)


def _gqa_kernel(*refs, has_ctx, tq, heads):
    q_ref, g_ref, kh_ref, vt_ref, qn_ref = refs[:5]
    wq = C_GROUP * HEAD
    pos = 5
    if has_ctx:
        tab_ref, ck_ref, cvt_ref = refs[pos:pos + 3]
        pos += 3
    o_ref = refs[pos]

    n_blk = q_ref.shape[0] // tq
    problems = []
    for h in range(heads):
        chunks = (_key_chunks(kh_ref, vt_ref, h, HEAD, HEAD)
                  + (_key_chunks(ck_ref, cvt_ref, h, HEAD, HEAD) if has_ctx else []))
        for i in range(n_blk):
            r = slice(i * tq, (i + 1) * tq)
            qs = []
            for gi in range(C_GROUP):
                c0 = h * wq + gi * HEAD
                q = _rms(q_ref[r, c0:c0 + HEAD], qn_ref[...])
                if has_ctx:
                    q = _rope(q, tab_ref, r)
                qs.append((q * ATTN_SCALE_LOG2).astype(BF16))
            problems.append((jnp.concatenate(qs, axis=0), chunks))

    def finish(idx, acc, l):
        h, i = divmod(idx, n_blk)
        r = slice(i * tq, (i + 1) * tq)
        o = (acc * (1.0 / l)).T
        for gi in range(C_GROUP):
            sl = slice(h * wq + gi * HEAD, h * wq + (gi + 1) * HEAD)
            o_ref[r, sl] = (o[gi * tq:(gi + 1) * tq] * _silu(g_ref[r, sl])).astype(BF16)

    _attend_chunks(problems, finish)


def _gqa_attention(p, kh, vt, q_norm, n_batch, n_seq, tq, q_blocks=1, heads=1, rope_tab=None, ctx=None):
    has_ctx = ctx is not None
    wq = heads * C_GROUP * HEAD
    wk = heads * HEAD
    rows = tq * q_blocks
    nq = n_seq // rows
    g_off = D_MODEL + 2 * C_KV_HEADS * HEAD
    assert g_off % wq == 0 and C_KV_HEADS % heads == 0
    g_col0 = g_off // wq
    in_specs = [
        pl.BlockSpec((rows, wq), lambda b, h, i: (b * nq + i, h)),
        pl.BlockSpec((rows, wq), lambda b, h, i: (b * nq + i, g_col0 + h)),
        pl.BlockSpec((n_seq, wk), lambda b, h, i: (b, h)),
        pl.BlockSpec((None, wk, n_seq), lambda b, h, i: (b, h, 0)),
        pl.BlockSpec((1, HEAD), lambda b, h, i: (0, 0)),
    ]
    args = [p, p, kh, vt, q_norm.reshape(1, HEAD)]
    if has_ctx:
        ck, cvt = ctx
        n_ctx = ck.shape[1]
        in_specs += [
            pl.BlockSpec((3, rows, HEAD), lambda b, h, i: (0, i, 0)),
            pl.BlockSpec((None, n_ctx, wk), lambda b, h, i: (b, 0, h)),
            pl.BlockSpec((None, wk, n_ctx), lambda b, h, i: (b, h, 0)),
        ]
        args += [rope_tab, ck, cvt]
    return pl.pallas_call(
        functools.partial(_gqa_kernel, has_ctx=has_ctx, tq=tq, heads=heads),
        grid=(n_batch, C_KV_HEADS // heads, nq),
        in_specs=in_specs,
        out_specs=pl.BlockSpec((rows, wq), lambda b, h, i: (b * nq + i, h)),
        out_shape=jax.ShapeDtypeStruct((n_batch * n_seq, D_MODEL), BF16),
        compiler_params=_cparams(3),
        name="gqa_attention",
    )(*args)


def _forget_gate(z, lb):
    t = jnp.exp2(jnp.abs(z) * -LOG2_E)
    u = 1.0 + t
    r = 1.0 / u
    tr = t * r
    pos = z >= 0.0
    sig = jnp.where(pos, r, tr)
    nsig = jnp.where(pos, tr, r)
    log2_sig = jnp.minimum(z, 0.0) * LOG2_E - jnp.log2(u)
    one_m = 1.0 - lb
    log2_f = jnp.where(lb > 0.0, jnp.log2(lb + one_m * sig), log2_sig)
    return log2_f, one_m * nsig


def _cumsum_rows(tri, x):
    hi = x.astype(BF16)
    mid = (x - hi.astype(F32)).astype(BF16)
    r = jnp.dot(tri, jnp.concatenate([hi, mid], axis=1), preferred_element_type=F32)
    return r[:, :HEAD] + r[:, HEAD:]


def _pair_rows(ref, b, odd):
    off = b if odd else 0
    if b >= 8:
        parts = [ref[pl.ds(i * 2 * b + off, b), :] for i in range(SCAN_HALF // b)]
    else:
        parts = [ref[pl.ds(off + r, SCAN_HALF // b, stride=2 * b), :] for r in range(b)]
    return parts[0] if len(parts) == 1 else jnp.concatenate(parts, axis=0)


def _pair_bcast(ref, b, row):
    if b >= 8:
        parts = [jnp.broadcast_to(ref[pl.ds(i * 2 * b + row, 1), :], (b, HEAD)) for i in range(SCAN_HALF // b)]
    else:
        parts = [ref[pl.ds(row, SCAN_HALF // b, stride=2 * b), :]] * b
    return parts[0] if len(parts) == 1 else jnp.concatenate(parts, axis=0)


def _pair_store(ref, b, odd, val):
    off = b if odd else 0
    if b >= 8:
        for i in range(SCAN_HALF // b):
            ref[pl.ds(i * 2 * b + off, b), :] = val[i * b:(i + 1) * b]
    else:
        n = SCAN_HALF // b
        for r in range(b):
            ref[pl.ds(off + r, n, stride=2 * b), :] = val[r * n:(r + 1) * n]


def _hgrn_kernel(*refs, n_chunks, has_state):
    q_ref, zf_ref, zb_ref, v_ref, g_ref, lb_ref, on_ref = refs[:7]
    s0_ref = refs[7] if has_state else None
    pos = 8
    o_ref = refs[pos]
    pos += 1
    sout_ref = None
    if not has_state:
        sout_ref = refs[pos]
        pos += 1
    tri_ref, mask_ref = refs[pos:pos + 2]
    pos += 2
    buf_sets = [refs[pos + k * SCAN_BUFS:pos + (k + 1) * SCAN_BUFS] for k in range(SCAN_CHUNKS_PER_ITER)]
    pos += SCAN_CHUNKS_PER_ITER * SCAN_BUFS
    if has_state:
        oall, qb_all, klb_all, decb_all = refs[pos:pos + 4]

    c = SCAN_CHUNK
    row = lax.broadcasted_iota(jnp.int32, (c, c), 0)
    col = lax.broadcasted_iota(jnp.int32, (c, c), 1)
    tri_ref[0] = (col <= row).astype(BF16)
    tri_ref[1] = (col >= row).astype(BF16)
    prow = lax.broadcasted_iota(jnp.int32, (SCAN_HALF, SCAN_HALF), 0)
    pcol = lax.broadcasted_iota(jnp.int32, (SCAN_HALF, SCAN_HALF), 1)
    for j in range(SCAN_LEVELS):
        b = 1 << j
        if b >= 8:
            same = (prow // b) == (pcol // b)
        else:
            same = (prow % (SCAN_HALF // b)) == (pcol % (SCAN_HALF // b))
        mask_ref[j] = same.astype(F32)

    lb_f = lb_ref[0:1, :]
    lb_b = lb_ref[1:2, :]

    def gates(r, bufs):
        sq, sv, skf, scf, sxf, skb, scb, sxb = bufs[:8]
        q = _silu(q_ref[r, :])
        v = v_ref[r, :]
        lf, kf = _forget_gate(zf_ref[r, :], lb_f)
        lbk, kb = _forget_gate(zb_ref[r, :], lb_b)
        cf = _cumsum_rows(tri_ref[0], lf)
        cb = _cumsum_rows(tri_ref[1], lbk)
        sq[...] = q
        sv[...] = v
        skf[...] = kf
        scf[...] = cf
        sxf[...] = cf - lf
        skb[...] = kb
        scb[...] = cb
        sxb[...] = cb - lbk
        return q, v, kf, kb, cf, cb

    def pair_scores(bufs):
        sq, sv, skf, scf, sxf, skb, scb, sxb = bufs[:8]
        scores = []
        for j in range(SCAN_LEVELS):
            b = 1 << j
            mask = mask_ref[j]
            qt = _pair_rows(sq, b, True) * jnp.exp2(_pair_rows(scf, b, True) - _pair_bcast(sxf, b, b))
            kt = _pair_rows(skf, b, False) * jnp.exp2(_pair_bcast(scf, b, b - 1) - _pair_rows(scf, b, False))
            z = lax.dot_general(qt.astype(BF16), kt.astype(BF16), _NT, preferred_element_type=F32) * mask
            scores.append((j, True, z.astype(BF16)))
            qt = _pair_rows(sq, b, False) * jnp.exp2(_pair_rows(scb, b, False) - _pair_bcast(sxb, b, b - 1))
            kt = _pair_rows(skb, b, True) * jnp.exp2(_pair_bcast(scb, b, b) - _pair_rows(scb, b, True))
            z = lax.dot_general(qt.astype(BF16), kt.astype(BF16), _NT, preferred_element_type=F32) * mask
            scores.append((j, False, z.astype(BF16)))
        return scores

    def pair_values(bufs, scores):
        sv, lvl_out = bufs[1], bufs[8:]
        for j, q_odd, z in scores:
            b = 1 << j
            _pair_store(lvl_out[j], b, q_odd, jnp.dot(z, _pair_rows(sv, b, not q_odd).astype(BF16),
                                                      preferred_element_type=F32))

    def chunk_result(bufs, g):
        q, v, kf, kb, cf, cb = g
        o = jnp.sum(q * (kf + kb), axis=-1, keepdims=True) * v
        for lvl in bufs[8:]:
            o = o + lvl[...]
        tot_f = cf[c - 1:c, :]
        tot_b = cb[0:1, :]
        klf = kf * jnp.exp2(tot_f - cf)
        klb = kb * jnp.exp2(tot_b - cb)
        return o, q, v, cf, cb, tot_f, tot_b, klf, klb

    def intra(rs):
        g = [gates(r, bufs) for r, bufs in zip(rs, buf_sets)]
        scores = [pair_scores(bufs) for bufs in buf_sets]
        for bufs, sc in zip(buf_sets, scores):
            pair_values(bufs, sc)
        return [chunk_result(bufs, gk) for bufs, gk in zip(buf_sets, g)]

    def finish(r, o):
        o = _rms(o, on_ref[...]) * _silu(g_ref[r, :])
        o_ref[r, :] = o.astype(BF16)

    def rows_of(ci):
        return pl.ds(pl.multiple_of(ci * c, c), c)

    per_iter = SCAN_CHUNKS_PER_ITER
    n_iters = n_chunks // per_iter

    if not has_state:
        def seqs(it, carry):
            cis = [it * per_iter + k for k in range(per_iter)]
            for ci, res in zip(cis, intra([rows_of(ci) for ci in cis])):
                o, q, v, cf, cb, tot_f, tot_b, klf, klb = res
                v_b = v.astype(BF16)
                sout_ref[ci, 0] = jnp.dot(klf.T.astype(BF16), v_b, preferred_element_type=F32)
                sout_ref[ci, 1] = jnp.dot(klb.T.astype(BF16), v_b, preferred_element_type=F32)
                finish(rows_of(ci), o)
            return carry

        lax.fori_loop(0, n_iters, seqs, 0)
        return

    def row_to_col(x):
        return jnp.broadcast_to(x, (HEAD, HEAD)).T

    def fwd(it, s):
        cis = [it * per_iter + k for k in range(per_iter)]
        for ci, res in zip(cis, intra([rows_of(ci) for ci in cis])):
            o, q, v, cf, cb, tot_f, tot_b, klf, klb = res
            r = rows_of(ci)
            oall[r, :] = o + jnp.dot((q * jnp.exp2(cf)).astype(BF16), s.astype(BF16),
                                     preferred_element_type=F32)
            qb_all[r, :] = (q * jnp.exp2(cb)).astype(BF16)
            klb_all[r, :] = klb
            decb_all[pl.ds(pl.multiple_of(ci * 8, 8), 1), :] = jnp.exp2(tot_b)
            s = row_to_col(jnp.exp2(tot_f)) * s + jnp.dot(klf.T.astype(BF16), v.astype(BF16),
                                                          preferred_element_type=F32)
        return s

    def bwd(i, s):
        ci = n_chunks - 1 - i
        r = rows_of(ci)
        finish(r, oall[r, :] + jnp.dot(qb_all[r, :], s.astype(BF16), preferred_element_type=F32))
        dec = decb_all[pl.ds(pl.multiple_of(ci * 8, 8), 1), :]
        return row_to_col(dec) * s + jnp.dot(klb_all[r, :].T.astype(BF16), v_ref[r, :].astype(BF16),
                                             preferred_element_type=F32)

    lax.fori_loop(0, n_iters, fwd, s0_ref[0])
    lax.fori_loop(0, n_chunks, bwd, s0_ref[1], unroll=4)


def _hgrn_scan(p, lb, o_norm, n_batch, n_seq, state=None, final_states=None, slot=0, seqs_per_step=8):
    has_state = state is not None
    c = SCAN_CHUNK
    if has_state:
        rows, n_steps = n_seq, n_batch
    else:
        assert n_seq == c and n_batch % seqs_per_step == 0
        rows, n_steps = seqs_per_step * c, n_batch // seqs_per_step
    n_chunks = rows // c
    col = lambda s: (lambda b, h: (b, s * A_HEADS + h))
    in_specs = [pl.BlockSpec((rows, HEAD), col(s)) for s in range(5)]
    in_specs += [
        pl.BlockSpec((2, HEAD), lambda b, h: (0, h)),
        pl.BlockSpec((1, HEAD), lambda b, h: (0, 0)),
    ]
    args = [p] * 5 + [lb, o_norm.reshape(1, HEAD)]
    out_specs = [pl.BlockSpec((rows, HEAD), lambda b, h: (b, h))]
    out_shape = [jax.ShapeDtypeStruct((n_batch * n_seq, D_MODEL), BF16)]
    scratch = [pltpu.VMEM((2, c, c), BF16), pltpu.VMEM((SCAN_LEVELS, SCAN_HALF, SCAN_HALF), F32)]
    assert n_chunks % SCAN_CHUNKS_PER_ITER == 0
    scratch += [pltpu.VMEM((c, HEAD), F32)] * (SCAN_CHUNKS_PER_ITER * SCAN_BUFS)
    if has_state:
        in_specs.append(pl.BlockSpec((None, 2, None, HEAD, HEAD), lambda b, h: (b, 0, h, 0, 0)))
        args.append(state)
        scratch += [pltpu.VMEM((rows, HEAD), F32), pltpu.VMEM((rows, HEAD), BF16),
                    pltpu.VMEM((rows, HEAD), F32), pltpu.VMEM((n_chunks * 8, HEAD), F32)]
    else:
        in_specs.append(pl.BlockSpec(memory_space=pl.ANY))
        args.append(final_states)
        out_specs.append(pl.BlockSpec((n_chunks, None, 2, None, HEAD, HEAD), lambda b, h: (b, slot, 0, h, 0, 0)))
        out_shape.append(jax.ShapeDtypeStruct(final_states.shape, F32))
    res = pl.pallas_call(
        functools.partial(_hgrn_kernel, n_chunks=n_chunks, has_state=has_state),
        grid=(n_steps, A_HEADS),
        in_specs=in_specs,
        out_specs=out_specs,
        out_shape=out_shape,
        input_output_aliases={} if has_state else {len(args) - 1: 1},
        scratch_shapes=scratch,
        compiler_params=_cparams(2),
        name="hgrn2_scan",
    )(*args)
    return (res[0], None) if has_state else (res[0], res[1])


def _diff_lambda_init(layer):
    return 0.8 - 0.6 * math.exp(-0.3 * layer)


def kernel(x_prompt, x_sample, state_a, cache_b_k, cache_b_v, cache_c_k, cache_c_v, c, c_ctx, norm_w, mod_w, mod_b, a_w_in, a_w_out, a_o_norm, a_lower_bound, b_w_in, b_w_out, b_q_norm, b_k_norm, b_lambda, b_subln, c_w_in, c_w_out, c_q_norm, c_k_norm):
    n_ctx_b, n_ctx_s = x_prompt.shape[:2]
    n_lat_b, n_lat_s = x_sample.shape[:2]
    past = cache_b_k.shape[2]
    tm = tn = 1024

    lb_all = jnp.cumsum(jax.nn.softmax(a_lower_bound.astype(F32), axis=0), axis=0)
    lb_all = lb_all - lb_all[0:1]

    cond = jnp.zeros((COND_ROWS, D_MODEL), F32).at[0].set(c_ctx).at[1:1 + n_lat_b].set(c)
    mods = _mod_rows(cond, mod_w, mod_b)
    rope_tab = _rope_tables(n_lat_s)

    lat_tiles = n_lat_s // tm
    groups = [
        dict(x=x_prompt.reshape(-1, D_MODEL), nb=n_ctx_b, ns=n_ctx_s, row=lambda i: 0, latent=False),
        dict(x=x_sample.reshape(-1, D_MODEL), nb=n_lat_b, ns=n_lat_s, row=lambda i: 1 + i // lat_tiles, latent=True),
    ]
    n_a_layers = a_w_in.shape[0]
    new_a = jnp.zeros((n_ctx_b, n_a_layers, 2, A_HEADS, HEAD, HEAD), F32)
    new_bk, new_bv, new_ck, new_cv = [], [], [], []

    for layer in range(DEPTH):
        kind, j = layer % N_MIXERS, layer // N_MIXERS
        w_in = (a_w_in, b_w_in, c_w_in)[kind][j].astype(BF16)
        w_out = (a_w_out, b_w_out, c_w_out)[kind][j].astype(BF16)
        for grp in groups:
            nb, ns, latent = grp["nb"], grp["ns"], grp["latent"]
            p = _in_proj(grp["x"], mods[layer], norm_w[layer], w_in, grp["row"], tm, tn)
            if kind == 0:
                if latent:
                    o, _ = _hgrn_scan(p, lb_all[layer], a_o_norm[j], nb, ns, state=state_a[:, j])
                else:
                    o, new_a = _hgrn_scan(p, lb_all[layer], a_o_norm[j], nb, ns, final_states=new_a, slot=j)
            elif kind == 1:
                tab = rope_tab if latent else None
                res = _kv_prep(p, D_MODEL, D_MODEL, 2 * D_MODEL, D_MODEL, b_k_norm[j], tab, nb, ns)
                ctx = None
                if latent:
                    ctx = (cache_b_k[:, j].reshape(nb, past, D_MODEL).astype(BF16),
                           jnp.swapaxes(cache_b_v[:, j].reshape(nb, past, D_MODEL), 1, 2).astype(BF16))
                else:
                    new_bk.append(res[2].reshape(nb, ns, B_HEADS, 2, HEAD))
                    new_bv.append(res[3].reshape(nb, ns, B_HEADS, 2 * HEAD))
                o = _diff_attention(p, res[0], res[1], b_q_norm[j], b_lambda[j], b_subln[j],
                                    _diff_lambda_init(layer), nb, ns, 512 if latent else 256,
                                    q_blocks=4 if latent else 1, heads=1 if latent else 4, rope_tab=tab, ctx=ctx)
            else:
                kvw = C_KV_HEADS * HEAD
                tab = rope_tab if latent else None
                res = _kv_prep(p, D_MODEL, kvw, D_MODEL + kvw, kvw, c_k_norm[j], tab, nb, ns)
                ctx = None
                if latent:
                    ctx = (cache_c_k[:, j].reshape(nb, past, kvw).astype(BF16),
                           jnp.swapaxes(cache_c_v[:, j].reshape(nb, past, kvw), 1, 2).astype(BF16))
                else:
                    new_ck.append(res[2].reshape(nb, ns, C_KV_HEADS, HEAD))
                    new_cv.append(res[3].reshape(nb, ns, C_KV_HEADS, HEAD))
                o = _gqa_attention(p, res[0], res[1], c_q_norm[j], nb, ns, 256, q_blocks=4 if latent else 1, heads=1 if latent else 2,
                                   rope_tab=tab, ctx=ctx)
            grp["x"] = _out_proj(o, w_out, grp["x"], mods[layer], grp["row"], tm, tn)

    y_prompt = groups[0]["x"].reshape(x_prompt.shape)
    y_sample = groups[1]["x"].reshape(x_sample.shape)
    return (y_prompt, y_sample, new_a, jnp.stack(new_bk, axis=1), jnp.stack(new_bv, axis=1),
            jnp.stack(new_ck, axis=1), jnp.stack(new_cv, axis=1))
```

```python
import functools
import math

import jax
import jax.numpy as jnp
from jax import lax
from jax.experimental import pallas as pl
from jax.experimental.pallas import tpu as pltpu

F32 = jnp.float32
BF16 = jnp.bfloat16

D_MODEL = 2048
DEPTH = 4
GRID_W = 64
N_MIXERS = 3
EPS = 1e-6
ROPE_THETA = 10000.0
HEAD = 128
A_HEADS = D_MODEL // HEAD
B_HEADS = D_MODEL // (2 * HEAD)
C_HEADS = D_MODEL // HEAD
C_KV_HEADS = C_HEADS // 4
C_GROUP = C_HEADS // C_KV_HEADS
COND_ROWS = 8

V7X_VMEM_LIMIT_BYTES = 56 * 1024 * 1024
SCAN_CHUNK = 256
SCAN_HALF = SCAN_CHUNK // 2
SCAN_LEVELS = SCAN_CHUNK.bit_length() - 1
SCAN_BUFS = 8 + SCAN_LEVELS
SCAN_CHUNKS_PER_ITER = 4
LOG2_E = math.log2(math.e)
ATTN_KEY_CHUNK = 1024
ATTN_SCALE_LOG2 = (HEAD ** -0.5) * LOG2_E
BF16_SUBLANES = 16


def _cparams(n_axes):
    return pltpu.CompilerParams(
        dimension_semantics=("arbitrary",) * n_axes,
        vmem_limit_bytes=V7X_VMEM_LIMIT_BYTES,
    )


def _sigmoid(x):
    return 1.0 / (1.0 + jnp.exp(-x))


def _silu(x):
    return x * _sigmoid(x)


def _rms(x, w):
    return x * lax.rsqrt(jnp.mean(x * x, axis=-1, keepdims=True) + EPS) * w


def _mod_kernel(cond_ref, w_ref, b_ref, o_ref):
    s = _silu(cond_ref[...]).astype(BF16)
    o_ref[...] = jnp.dot(s, w_ref[...].astype(BF16), preferred_element_type=F32) + b_ref[...]


def _mod_rows(cond, mod_w, mod_b):
    d3 = 3 * D_MODEL
    tn = 768
    out = pl.pallas_call(
        _mod_kernel,
        grid=(DEPTH, d3 // tn),
        in_specs=[
            pl.BlockSpec((COND_ROWS, D_MODEL), lambda l, j: (0, 0)),
            pl.BlockSpec((None, D_MODEL, tn), lambda l, j: (l, 0, j)),
            pl.BlockSpec((None, 1, tn), lambda l, j: (l, 0, j)),
        ],
        out_specs=pl.BlockSpec((None, COND_ROWS, tn), lambda l, j: (l, 0, j)),
        out_shape=jax.ShapeDtypeStruct((DEPTH, COND_ROWS, d3), F32),
        compiler_params=_cparams(2),
        name="adaln_rows",
    )(cond, mod_w, mod_b.reshape(DEPTH, 1, d3))
    return out.reshape(DEPTH, COND_ROWS, 3, D_MODEL)


def _inproj_kernel(x_ref, mod_ref, nw_ref, w_ref, o_ref, h_ref):
    @pl.when(pl.program_id(1) == 0)
    def _():
        xn = _rms(x_ref[...], nw_ref[...])
        h_ref[...] = (xn * (1.0 + mod_ref[1:2, :]) + mod_ref[0:1, :]).astype(BF16)

    o_ref[...] = jnp.dot(h_ref[...], w_ref[...], preferred_element_type=F32).astype(o_ref.dtype)


def _in_proj(x, mod, norm_w, w, layer, row_of_tile, tm, tn):
    t, n = x.shape[0], w.shape[2]
    return pl.pallas_call(
        _inproj_kernel,
        grid=(t // tm, n // tn),
        in_specs=[
            pl.BlockSpec((tm, D_MODEL), lambda i, j: (i, 0)),
            pl.BlockSpec((None, 3, D_MODEL), lambda i, j: (row_of_tile(i), 0, 0)),
            pl.BlockSpec((1, D_MODEL), lambda i, j: (0, 0)),
            pl.BlockSpec((None, D_MODEL, tn), lambda i, j: (layer, 0, j)),
        ],
        out_specs=pl.BlockSpec((tm, tn), lambda i, j: (i, j)),
        out_shape=jax.ShapeDtypeStruct((t, n), F32),
        scratch_shapes=[pltpu.VMEM((tm, D_MODEL), BF16)],
        compiler_params=_cparams(2),
        name="in_proj",
    )(x, mod, norm_w.reshape(1, D_MODEL), w)


def _outproj_kernel(o_ref, w_ref, x_ref, mod_ref, y_ref):
    acc = jnp.dot(o_ref[...], w_ref[...], preferred_element_type=F32)
    y_ref[...] = x_ref[...] + mod_ref[2:3, :] * acc


def _out_proj(o, w, layer, x, mod, row_of_tile, tm, tn):
    t, kdim = o.shape
    return pl.pallas_call(
        _outproj_kernel,
        grid=(t // tm, D_MODEL // tn),
        in_specs=[
            pl.BlockSpec((tm, kdim), lambda i, j: (i, 0)),
            pl.BlockSpec((None, kdim, tn), lambda i, j: (layer, 0, j)),
            pl.BlockSpec((tm, tn), lambda i, j: (i, j)),
            pl.BlockSpec((None, 3, tn), lambda i, j: (row_of_tile(i), 0, j)),
        ],
        out_specs=pl.BlockSpec((tm, tn), lambda i, j: (i, j)),
        out_shape=jax.ShapeDtypeStruct((t, D_MODEL), F32),
        compiler_params=_cparams(2),
        name="out_proj",
    )(o, w, x, mod)


def _rope_tables(n):
    pos = jnp.arange(n)
    quarter = HEAD // 4
    inv_freq = ROPE_THETA ** (-jnp.arange(quarter, dtype=F32) / quarter)

    def axis_angles(p):
        ang = p.astype(F32)[:, None] * inv_freq[None, :]
        return jnp.concatenate([ang, ang], axis=-1)

    ang = jnp.concatenate([axis_angles(pos // GRID_W), axis_angles(pos % GRID_W)], axis=-1)
    cos, sin = jnp.cos(ang), jnp.sin(ang)
    first = (jnp.arange(HEAD) % (2 * quarter)) < quarter
    return jnp.stack([cos, jnp.where(first, -sin, 0.0), jnp.where(first, 0.0, sin)])


def _rope(x, tab_ref, rows=slice(None)):
    up = pltpu.roll(x, HEAD - HEAD // 4, 1)
    down = pltpu.roll(x, HEAD // 4, 1)
    return x * tab_ref[0, rows, :] + up * tab_ref[1, rows, :] + down * tab_ref[2, rows, :]


def _kvprep_kernel(*refs, n_kheads, rope, emit_norm):
    k_ref, v_ref, kn_ref = refs[:3]
    pos = 3
    tab_ref = None
    if rope:
        tab_ref = refs[pos]
        pos += 1
    kh_ref, vt_ref = refs[pos], refs[pos + 1]
    kn_out, v_out = (refs[pos + 2], refs[pos + 3]) if emit_norm else (None, None)
    for h in range(n_kheads):
        sl = slice(h * HEAD, (h + 1) * HEAD)
        kn = _rms(k_ref[:, sl], kn_ref[...])
        if emit_norm:
            kn_out[:, sl] = kn
        if rope:
            kn = _rope(kn, tab_ref)
        kh_ref[:, sl] = kn.astype(BF16)
    v = v_ref[...]
    if emit_norm:
        v_out[...] = v
    vt_ref[...] = v.T.astype(BF16)


def _kv_prep(p, k_col, k_w, v_col, v_w, k_norm, rope_tab, n_batch, n_seq):
    t = p.shape[0]
    rope = rope_tab is not None
    emit_norm = not rope
    tr = min(512, n_seq)
    per_seq = n_seq // tr
    in_specs = [
        pl.BlockSpec((tr, k_w), lambda i: (i, k_col // k_w)),
        pl.BlockSpec((tr, v_w), lambda i: (i, v_col // v_w)),
        pl.BlockSpec((1, HEAD), lambda i: (0, 0)),
    ]
    args = [p, p, k_norm.reshape(1, HEAD)]
    if rope:
        in_specs.append(pl.BlockSpec((3, tr, HEAD), lambda i: (0, i % per_seq, 0)))
        args.append(rope_tab)
    out_specs = [
        pl.BlockSpec((tr, k_w), lambda i: (i, 0)),
        pl.BlockSpec((None, v_w, tr), lambda i: (i // per_seq, 0, i % per_seq)),
    ]
    out_shape = [jax.ShapeDtypeStruct((t, k_w), BF16), jax.ShapeDtypeStruct((n_batch, v_w, n_seq), BF16)]
    if emit_norm:
        out_specs += [pl.BlockSpec((tr, k_w), lambda i: (i, 0)), pl.BlockSpec((tr, v_w), lambda i: (i, 0))]
        out_shape += [jax.ShapeDtypeStruct((t, k_w), F32), jax.ShapeDtypeStruct((t, v_w), F32)]
    return pl.pallas_call(
        functools.partial(_kvprep_kernel, n_kheads=k_w // HEAD, rope=rope, emit_norm=emit_norm),
        grid=(t // tr,),
        in_specs=in_specs,
        out_specs=out_specs,
        out_shape=out_shape,
        compiler_params=_cparams(1),
        name="kv_prep",
    )(*args)


_NT = (((1,), (1,)), ((), ()))


def _key_chunks(k_ref, vt_ref, head, k_w, v_w):
    n = k_ref.shape[0]
    step = min(n, ATTN_KEY_CHUNK)
    ones = jnp.ones((BF16_SUBLANES, step), BF16)
    return [(k_ref[c0:c0 + step, head * k_w:(head + 1) * k_w],
             jnp.concatenate([vt_ref[head * v_w:(head + 1) * v_w, c0:c0 + step], ones], axis=0))
            for c0 in range(0, n, step)]


def _attend_chunks(problems, finish):
    pairs = [(i, j) for i, (_, chunks) in enumerate(problems) for j in range(len(chunks))]
    s, m, o = {}, {}, {}

    def score(t):
        i, j = pairs[t]
        q, chunks = problems[i]
        s[t] = lax.dot_general(chunks[j][0], q, _NT, preferred_element_type=F32)
        m[t] = s[t].max(axis=0, keepdims=True)

    for t in range(min(2, len(pairs))):
        score(t)
    for t, (i, j) in enumerate(pairs):
        e = jnp.exp2(s.pop(t) - m[t]).astype(BF16)
        if t + 2 < len(pairs):
            score(t + 2)
        chunks = problems[i][1]
        n = len(chunks)
        n_v = chunks[j][1].shape[0] - BF16_SUBLANES
        o[t] = jnp.dot(chunks[j][1], e, preferred_element_type=F32)
        if j == n - 1:
            ts = range(t - n + 1, t + 1)
            if n == 1:
                acc = o[t]
            else:
                m_all = functools.reduce(jnp.maximum, [m[u] for u in ts])
                acc = None
                for u in ts:
                    part = o[u] * jnp.exp2(m[u] - m_all)
                    acc = part if acc is None else acc + part
            finish(i, acc[:n_v], acc[n_v:n_v + 1])


def _diff_attn_kernel(*refs, has_ctx, lam_init, tq, heads):
    q_ref, g_ref, kh_ref, vt_ref, qn_ref, lamv_ref, sub_ref = refs[:7]
    w2 = 2 * HEAD
    pos = 7
    if has_ctx:
        tab_ref, ck_ref, cvt_ref = refs[pos:pos + 3]
        pos += 3
    o_ref = refs[pos]

    lv = lamv_ref[...]
    lam = (jnp.exp(jnp.sum(lv[0:1] * lv[1:2], keepdims=True))
           - jnp.exp(jnp.sum(lv[2:3] * lv[3:4], keepdims=True)) + lam_init)

    zero = jnp.zeros((tq, HEAD), BF16)
    n_blk = q_ref.shape[0] // tq
    problems = []
    for h in range(heads):
        chunks = _key_chunks(kh_ref, vt_ref, h, w2, w2) + (_key_chunks(ck_ref, cvt_ref, h, w2, w2) if has_ctx else [])
        for i in range(n_blk):
            r = slice(i * tq, (i + 1) * tq)
            rows = []
            for comp in range(2):
                q = _rms(q_ref[r, h * w2 + comp * HEAD:h * w2 + (comp + 1) * HEAD], qn_ref[...])
                if has_ctx:
                    q = _rope(q, tab_ref, r)
                q = (q * ATTN_SCALE_LOG2).astype(BF16)
                rows.append(jnp.concatenate([q, zero] if comp == 0 else [zero, q], axis=1))
            problems.append((jnp.concatenate(rows, axis=0), chunks))

    def finish(idx, acc, l):
        h, i = divmod(idx, n_blk)
        r = slice(i * tq, (i + 1) * tq)
        cols = slice(h * w2, (h + 1) * w2)
        o_t = acc[:, :tq] * (1.0 / l[:, :tq]) - acc[:, tq:] * (lam / l[:, tq:])
        o = _rms(o_t.T, sub_ref[...]) * (1.0 - lam_init)
        o_ref[r, cols] = (o * _silu(g_ref[r, cols])).astype(BF16)

    _attend_chunks(problems, finish)


def _diff_attention(p, kh, vt, q_norm, lam_vecs, subln, lam_init, n_batch, n_seq, tq, q_blocks=1, heads=1,
                    rope_tab=None, ctx=None):
    has_ctx = ctx is not None
    w2 = 2 * HEAD
    wc = heads * w2
    rows = tq * q_blocks
    nq = n_seq // rows
    assert (3 * D_MODEL) % wc == 0 and B_HEADS % heads == 0
    g_col0 = 3 * D_MODEL // wc
    in_specs = [
        pl.BlockSpec((rows, wc), lambda b, h, i: (b * nq + i, h)),
        pl.BlockSpec((rows, wc), lambda b, h, i: (b * nq + i, g_col0 + h)),
        pl.BlockSpec((n_seq, wc), lambda b, h, i: (b, h)),
        pl.BlockSpec((None, wc, n_seq), lambda b, h, i: (b, h, 0)),
        pl.BlockSpec((1, HEAD), lambda b, h, i: (0, 0)),
        pl.BlockSpec((4, HEAD), lambda b, h, i: (0, 0)),
        pl.BlockSpec((1, w2), lambda b, h, i: (0, 0)),
    ]
    args = [p, p, kh, vt, q_norm.reshape(1, HEAD), lam_vecs, subln.reshape(1, w2)]
    if has_ctx:
        ck, cvt = ctx
        n_ctx = ck.shape[1]
        in_specs += [
            pl.BlockSpec((3, rows, HEAD), lambda b, h, i: (0, i, 0)),
            pl.BlockSpec((None, n_ctx, wc), lambda b, h, i: (b, 0, h)),
            pl.BlockSpec((None, wc, n_ctx), lambda b, h, i: (b, h, 0)),
        ]
        args += [rope_tab, ck, cvt]
    return pl.pallas_call(
        functools.partial(_diff_attn_kernel, has_ctx=has_ctx, lam_init=lam_init, tq=tq, heads=heads),
        grid=(n_batch, B_HEADS // heads, nq),
        in_specs=in_specs,
        out_specs=pl.BlockSpec((rows, wc), lambda b, h, i: (b * nq + i, h)),
        out_shape=jax.ShapeDtypeStruct((n_batch * n_seq, D_MODEL), BF16),
        compiler_params=_cparams(3),
        name="diff_attention",
    )(*args)


def _gqa_kernel(*refs, has_ctx, tq, heads):
    q_ref, g_ref, kh_ref, vt_ref, qn_ref = refs[:5]
    wq = C_GROUP * HEAD
    pos = 5
    if has_ctx:
        tab_ref, ck_ref, cvt_ref = refs[pos:pos + 3]
        pos += 3
    o_ref = refs[pos]

    n_blk = q_ref.shape[0] // tq
    problems = []
    for h in range(heads):
        chunks = (_key_chunks(kh_ref, vt_ref, h, HEAD, HEAD)
                  + (_key_chunks(ck_ref, cvt_ref, h, HEAD, HEAD) if has_ctx else []))
        for i in range(n_blk):
            r = slice(i * tq, (i + 1) * tq)
            qs = []
            for gi in range(C_GROUP):
                c0 = h * wq + gi * HEAD
                q = _rms(q_ref[r, c0:c0 + HEAD], qn_ref[...])
                if has_ctx:
                    q = _rope(q, tab_ref, r)
                qs.append((q * ATTN_SCALE_LOG2).astype(BF16))
            problems.append((jnp.concatenate(qs, axis=0), chunks))

    def finish(idx, acc, l):
        h, i = divmod(idx, n_blk)
        r = slice(i * tq, (i + 1) * tq)
        o = (acc * (1.0 / l)).T
        for gi in range(C_GROUP):
            sl = slice(h * wq + gi * HEAD, h * wq + (gi + 1) * HEAD)
            o_ref[r, sl] = (o[gi * tq:(gi + 1) * tq] * _silu(g_ref[r, sl])).astype(BF16)

    _attend_chunks(problems, finish)


def _gqa_attention(p, kh, vt, q_norm, n_batch, n_seq, tq, q_blocks=1, heads=1, rope_tab=None, ctx=None):
    has_ctx = ctx is not None
    wq = heads * C_GROUP * HEAD
    wk = heads * HEAD
    rows = tq * q_blocks
    nq = n_seq // rows
    g_off = D_MODEL + 2 * C_KV_HEADS * HEAD
    assert g_off % wq == 0 and C_KV_HEADS % heads == 0
    g_col0 = g_off // wq
    in_specs = [
        pl.BlockSpec((rows, wq), lambda b, h, i: (b * nq + i, h)),
        pl.BlockSpec((rows, wq), lambda b, h, i: (b * nq + i, g_col0 + h)),
        pl.BlockSpec((n_seq, wk), lambda b, h, i: (b, h)),
        pl.BlockSpec((None, wk, n_seq), lambda b, h, i: (b, h, 0)),
        pl.BlockSpec((1, HEAD), lambda b, h, i: (0, 0)),
    ]
    args = [p, p, kh, vt, q_norm.reshape(1, HEAD)]
    if has_ctx:
        ck, cvt = ctx
        n_ctx = ck.shape[1]
        in_specs += [
            pl.BlockSpec((3, rows, HEAD), lambda b, h, i: (0, i, 0)),
            pl.BlockSpec((None, n_ctx, wk), lambda b, h, i: (b, 0, h)),
            pl.BlockSpec((None, wk, n_ctx), lambda b, h, i: (b, h, 0)),
        ]
        args += [rope_tab, ck, cvt]
    return pl.pallas_call(
        functools.partial(_gqa_kernel, has_ctx=has_ctx, tq=tq, heads=heads),
        grid=(n_batch, C_KV_HEADS // heads, nq),
        in_specs=in_specs,
        out_specs=pl.BlockSpec((rows, wq), lambda b, h, i: (b * nq + i, h)),
        out_shape=jax.ShapeDtypeStruct((n_batch * n_seq, D_MODEL), BF16),
        compiler_params=_cparams(3),
        name="gqa_attention",
    )(*args)


def _forget_gate(z, lb):
    t = jnp.exp2(jnp.abs(z) * -LOG2_E)
    u = 1.0 + t
    r = 1.0 / u
    tr = t * r
    pos = z >= 0.0
    sig = jnp.where(pos, r, tr)
    nsig = jnp.where(pos, tr, r)
    log2_sig = jnp.minimum(z, 0.0) * LOG2_E - jnp.log2(u)
    one_m = 1.0 - lb
    log2_f = jnp.where(lb > 0.0, jnp.log2(lb + one_m * sig), log2_sig)
    return log2_f, one_m * nsig


def _cumsum_rows(tri, x):
    hi = x.astype(BF16)
    mid = (x - hi.astype(F32)).astype(BF16)
    r = jnp.dot(tri, jnp.concatenate([hi, mid], axis=1), preferred_element_type=F32)
    return r[:, :HEAD] + r[:, HEAD:]


def _pair_rows(ref, b, odd):
    off = b if odd else 0
    if b >= 8:
        parts = [ref[pl.ds(i * 2 * b + off, b), :] for i in range(SCAN_HALF // b)]
    else:
        parts = [ref[pl.ds(off + r, SCAN_HALF // b, stride=2 * b), :] for r in range(b)]
    return parts[0] if len(parts) == 1 else jnp.concatenate(parts, axis=0)


def _pair_bcast(ref, b, row):
    if b >= 8:
        parts = [jnp.broadcast_to(ref[pl.ds(i * 2 * b + row, 1), :], (b, HEAD)) for i in range(SCAN_HALF // b)]
    else:
        parts = [ref[pl.ds(row, SCAN_HALF // b, stride=2 * b), :]] * b
    return parts[0] if len(parts) == 1 else jnp.concatenate(parts, axis=0)


def _pair_store(ref, b, odd, val):
    off = b if odd else 0
    if b >= 8:
        for i in range(SCAN_HALF // b):
            ref[pl.ds(i * 2 * b + off, b), :] = val[i * b:(i + 1) * b]
    else:
        n = SCAN_HALF // b
        for r in range(b):
            ref[pl.ds(off + r, n, stride=2 * b), :] = val[r * n:(r + 1) * n]


def _hgrn_kernel(*refs, n_chunks, has_state):
    q_ref, zf_ref, zb_ref, v_ref, g_ref, lb_ref, on_ref = refs[:7]
    s0_ref = refs[7] if has_state else None
    pos = 8
    o_ref = refs[pos]
    pos += 1
    sout_ref = None
    if not has_state:
        sout_ref = refs[pos]
        pos += 1
    tri_ref, mask_ref = refs[pos:pos + 2]
    pos += 2
    buf_sets = [refs[pos + k * SCAN_BUFS:pos + (k + 1) * SCAN_BUFS] for k in range(SCAN_CHUNKS_PER_ITER)]
    pos += SCAN_CHUNKS_PER_ITER * SCAN_BUFS
    if has_state:
        oall, qb_all, klb_all, decb_all = refs[pos:pos + 4]

    c = SCAN_CHUNK
    row = lax.broadcasted_iota(jnp.int32, (c, c), 0)
    col = lax.broadcasted_iota(jnp.int32, (c, c), 1)
    tri_ref[0] = (col <= row).astype(BF16)
    tri_ref[1] = (col >= row).astype(BF16)
    prow = lax.broadcasted_iota(jnp.int32, (SCAN_HALF, SCAN_HALF), 0)
    pcol = lax.broadcasted_iota(jnp.int32, (SCAN_HALF, SCAN_HALF), 1)
    for j in range(SCAN_LEVELS):
        b = 1 << j
        if b >= 8:
            same = (prow // b) == (pcol // b)
        else:
            same = (prow % (SCAN_HALF // b)) == (pcol % (SCAN_HALF // b))
        mask_ref[j] = same.astype(F32)

    lb_f = lb_ref[0:1, :]
    lb_b = lb_ref[1:2, :]

    def gates(r, bufs):
        sq, sv, skf, scf, sxf, skb, scb, sxb = bufs[:8]
        q = _silu(q_ref[r, :])
        v = v_ref[r, :]
        lf, kf = _forget_gate(zf_ref[r, :], lb_f)
        lbk, kb = _forget_gate(zb_ref[r, :], lb_b)
        cf = _cumsum_rows(tri_ref[0], lf)
        cb = _cumsum_rows(tri_ref[1], lbk)
        sq[...] = q
        sv[...] = v
        skf[...] = kf
        scf[...] = cf
        sxf[...] = cf - lf
        skb[...] = kb
        scb[...] = cb
        sxb[...] = cb - lbk
        return q, v, kf, kb, cf, cb

    def pair_scores(bufs):
        sq, sv, skf, scf, sxf, skb, scb, sxb = bufs[:8]
        scores = []
        for j in range(SCAN_LEVELS):
            b = 1 << j
            mask = mask_ref[j]
            qt = _pair_rows(sq, b, True) * jnp.exp2(_pair_rows(scf, b, True) - _pair_bcast(sxf, b, b))
            kt = _pair_rows(skf, b, False) * jnp.exp2(_pair_bcast(scf, b, b - 1) - _pair_rows(scf, b, False))
            z = lax.dot_general(qt.astype(BF16), kt.astype(BF16), _NT, preferred_element_type=F32) * mask
            scores.append((j, True, z.astype(BF16)))
            qt = _pair_rows(sq, b, False) * jnp.exp2(_pair_rows(scb, b, False) - _pair_bcast(sxb, b, b - 1))
            kt = _pair_rows(skb, b, True) * jnp.exp2(_pair_bcast(scb, b, b) - _pair_rows(scb, b, True))
            z = lax.dot_general(qt.astype(BF16), kt.astype(BF16), _NT, preferred_element_type=F32) * mask
            scores.append((j, False, z.astype(BF16)))
        return scores

    def pair_values(bufs, scores):
        sv, lvl_out = bufs[1], bufs[8:]
        for j, q_odd, z in scores:
            b = 1 << j
            _pair_store(lvl_out[j], b, q_odd, jnp.dot(z, _pair_rows(sv, b, not q_odd).astype(BF16),
                                                      preferred_element_type=F32))

    def chunk_result(bufs, g):
        q, v, kf, kb, cf, cb = g
        o = jnp.sum(q * (kf + kb), axis=-1, keepdims=True) * v
        for lvl in bufs[8:]:
            o = o + lvl[...]
        tot_f = cf[c - 1:c, :]
        tot_b = cb[0:1, :]
        klf = kf * jnp.exp2(tot_f - cf)
        klb = kb * jnp.exp2(tot_b - cb)
        return o, q, v, cf, cb, tot_f, tot_b, klf, klb

    def intra(rs):
        g = [gates(r, bufs) for r, bufs in zip(rs, buf_sets)]
        scores = [pair_scores(bufs) for bufs in buf_sets]
        for bufs, sc in zip(buf_sets, scores):
            pair_values(bufs, sc)
        return [chunk_result(bufs, gk) for bufs, gk in zip(buf_sets, g)]

    def finish(r, o):
        o = _rms(o, on_ref[...]) * _silu(g_ref[r, :])
        o_ref[r, :] = o.astype(BF16)

    def rows_of(ci):
        return pl.ds(pl.multiple_of(ci * c, c), c)

    per_iter = SCAN_CHUNKS_PER_ITER
    n_iters = n_chunks // per_iter

    if not has_state:
        def seqs(it, carry):
            cis = [it * per_iter + k for k in range(per_iter)]
            for ci, res in zip(cis, intra([rows_of(ci) for ci in cis])):
                o, q, v, cf, cb, tot_f, tot_b, klf, klb = res
                v_b = v.astype(BF16)
                sout_ref[ci, 0] = jnp.dot(klf.T.astype(BF16), v_b, preferred_element_type=F32)
                sout_ref[ci, 1] = jnp.dot(klb.T.astype(BF16), v_b, preferred_element_type=F32)
                finish(rows_of(ci), o)
            return carry

        lax.fori_loop(0, n_iters, seqs, 0)
        return

    def row_to_col(x):
        return jnp.broadcast_to(x, (HEAD, HEAD)).T

    def fwd(it, s):
        cis = [it * per_iter + k for k in range(per_iter)]
        for ci, res in zip(cis, intra([rows_of(ci) for ci in cis])):
            o, q, v, cf, cb, tot_f, tot_b, klf, klb = res
            r = rows_of(ci)
            oall[r, :] = o + jnp.dot((q * jnp.exp2(cf)).astype(BF16), s.astype(BF16),
                                     preferred_element_type=F32)
            qb_all[r, :] = (q * jnp.exp2(cb)).astype(BF16)
            klb_all[r, :] = klb
            decb_all[pl.ds(pl.multiple_of(ci * 8, 8), 1), :] = jnp.exp2(tot_b)
            s = row_to_col(jnp.exp2(tot_f)) * s + jnp.dot(klf.T.astype(BF16), v.astype(BF16),
                                                          preferred_element_type=F32)
        return s

    def bwd(i, s):
        ci = n_chunks - 1 - i
        r = rows_of(ci)
        finish(r, oall[r, :] + jnp.dot(qb_all[r, :], s.astype(BF16), preferred_element_type=F32))
        dec = decb_all[pl.ds(pl.multiple_of(ci * 8, 8), 1), :]
        return row_to_col(dec) * s + jnp.dot(klb_all[r, :].T.astype(BF16), v_ref[r, :].astype(BF16),
                                             preferred_element_type=F32)

    lax.fori_loop(0, n_iters, fwd, s0_ref[0])
    lax.fori_loop(0, n_chunks, bwd, s0_ref[1], unroll=4)


def _hgrn_scan(p, lb, o_norm, n_batch, n_seq, state=None, final_states=None, slot=0, seqs_per_step=8):
    has_state = state is not None
    c = SCAN_CHUNK
    if has_state:
        rows, n_steps = n_seq, n_batch
    else:
        assert n_seq == c and n_batch % seqs_per_step == 0
        rows, n_steps = seqs_per_step * c, n_batch // seqs_per_step
    n_chunks = rows // c
    col = lambda s: (lambda b, h: (b, s * A_HEADS + h))
    in_specs = [pl.BlockSpec((rows, HEAD), col(s)) for s in range(5)]
    in_specs += [
        pl.BlockSpec((2, HEAD), lambda b, h: (0, h)),
        pl.BlockSpec((1, HEAD), lambda b, h: (0, 0)),
    ]
    args = [p] * 5 + [lb, o_norm.reshape(1, HEAD)]
    out_specs = [pl.BlockSpec((rows, HEAD), lambda b, h: (b, h))]
    out_shape = [jax.ShapeDtypeStruct((n_batch * n_seq, D_MODEL), BF16)]
    scratch = [pltpu.VMEM((2, c, c), BF16), pltpu.VMEM((SCAN_LEVELS, SCAN_HALF, SCAN_HALF), F32)]
    assert n_chunks % SCAN_CHUNKS_PER_ITER == 0
    scratch += [pltpu.VMEM((c, HEAD), F32)] * (SCAN_CHUNKS_PER_ITER * SCAN_BUFS)
    if has_state:
        in_specs.append(pl.BlockSpec((None, 2, None, HEAD, HEAD), lambda b, h: (b, 0, h, 0, 0)))
        args.append(state)
        scratch += [pltpu.VMEM((rows, HEAD), F32), pltpu.VMEM((rows, HEAD), BF16),
                    pltpu.VMEM((rows, HEAD), F32), pltpu.VMEM((n_chunks * 8, HEAD), F32)]
    else:
        in_specs.append(pl.BlockSpec(memory_space=pl.ANY))
        args.append(final_states)
        out_specs.append(pl.BlockSpec((n_chunks, None, 2, None, HEAD, HEAD), lambda b, h: (b, slot, 0, h, 0, 0)))
        out_shape.append(jax.ShapeDtypeStruct(final_states.shape, F32))
    res = pl.pallas_call(
        functools.partial(_hgrn_kernel, n_chunks=n_chunks, has_state=has_state),
        grid=(n_steps, A_HEADS),
        in_specs=in_specs,
        out_specs=out_specs,
        out_shape=out_shape,
        input_output_aliases={} if has_state else {len(args) - 1: 1},
        scratch_shapes=scratch,
        compiler_params=_cparams(2),
        name="hgrn2_scan",
    )(*args)
    return (res[0], None) if has_state else (res[0], res[1])


def _diff_lambda_init(layer):
    return 0.8 - 0.6 * math.exp(-0.3 * layer)


def kernel(x_prompt, x_sample, state_a, cache_b_k, cache_b_v, cache_c_k, cache_c_v, c, c_ctx, norm_w, mod_w, mod_b, a_w_in, a_w_out, a_o_norm, a_lower_bound, b_w_in, b_w_out, b_q_norm, b_k_norm, b_lambda, b_subln, c_w_in, c_w_out, c_q_norm, c_k_norm):
    n_ctx_b, n_ctx_s = x_prompt.shape[:2]
    n_lat_b, n_lat_s = x_sample.shape[:2]
    past = cache_b_k.shape[2]
    in_tm, in_tn = 1024, 1024
    out_tm, out_tn = 512, D_MODEL

    lb_all = jnp.cumsum(jax.nn.softmax(a_lower_bound.astype(F32), axis=0), axis=0)
    lb_all = lb_all - lb_all[0:1]

    cond = jnp.zeros((COND_ROWS, D_MODEL), F32).at[0].set(c_ctx).at[1:1 + n_lat_b].set(c)
    mods = _mod_rows(cond, mod_w, mod_b)
    rope_tab = _rope_tables(n_lat_s)

    groups = [
        dict(x=x_prompt.reshape(-1, D_MODEL), nb=n_ctx_b, ns=n_ctx_s, latent=False,
             row=lambda tm: (lambda i: 0)),
        dict(x=x_sample.reshape(-1, D_MODEL), nb=n_lat_b, ns=n_lat_s, latent=True,
             row=lambda tm: (lambda i: 1 + i // (n_lat_s // tm))),
    ]
    w_in_all = [w.astype(BF16) for w in (a_w_in, b_w_in, c_w_in)]
    w_out_all = [w.astype(BF16) for w in (a_w_out, b_w_out, c_w_out)]
    n_a_layers = a_w_in.shape[0]
    new_a = jnp.zeros((n_ctx_b, n_a_layers, 2, A_HEADS, HEAD, HEAD), F32)
    new_bk, new_bv, new_ck, new_cv = [], [], [], []

    for layer in range(DEPTH):
        kind, j = layer % N_MIXERS, layer // N_MIXERS
        for grp in groups:
            nb, ns, latent = grp["nb"], grp["ns"], grp["latent"]
            p = _in_proj(grp["x"], mods[layer], norm_w[layer], w_in_all[kind], j, grp["row"](in_tm), in_tm, in_tn)
            if kind == 0:
                if latent:
                    o, _ = _hgrn_scan(p, lb_all[layer], a_o_norm[j], nb, ns, state=state_a[:, j])
                else:
                    o, new_a = _hgrn_scan(p, lb_all[layer], a_o_norm[j], nb, ns, final_states=new_a, slot=j)
            elif kind == 1:
                tab = rope_tab if latent else None
                res = _kv_prep(p, D_MODEL, D_MODEL, 2 * D_MODEL, D_MODEL, b_k_norm[j], tab, nb, ns)
                ctx = None
                if latent:
                    ctx = (cache_b_k[:, j].reshape(nb, past, D_MODEL).astype(BF16),
                           jnp.swapaxes(cache_b_v[:, j].reshape(nb, past, D_MODEL), 1, 2).astype(BF16))
                else:
                    new_bk.append(res[2].reshape(nb, ns, B_HEADS, 2, HEAD))
                    new_bv.append(res[3].reshape(nb, ns, B_HEADS, 2 * HEAD))
                o = _diff_attention(p, res[0], res[1], b_q_norm[j], b_lambda[j], b_subln[j],
                                    _diff_lambda_init(layer), nb, ns, 512 if latent else 256,
                                    q_blocks=4 if latent else 1, heads=1 if latent else 4, rope_tab=tab, ctx=ctx)
            else:
                kvw = C_KV_HEADS * HEAD
                tab = rope_tab if latent else None
                res = _kv_prep(p, D_MODEL, kvw, D_MODEL + kvw, kvw, c_k_norm[j], tab, nb, ns)
                ctx = None
                if latent:
                    ctx = (cache_c_k[:, j].reshape(nb, past, kvw).astype(BF16),
                           jnp.swapaxes(cache_c_v[:, j].reshape(nb, past, kvw), 1, 2).astype(BF16))
                else:
                    new_ck.append(res[2].reshape(nb, ns, C_KV_HEADS, HEAD))
                    new_cv.append(res[3].reshape(nb, ns, C_KV_HEADS, HEAD))
                o = _gqa_attention(p, res[0], res[1], c_q_norm[j], nb, ns, 256, q_blocks=4 if latent else 1, heads=1 if latent else 2,
                                   rope_tab=tab, ctx=ctx)
            grp["x"] = _out_proj(o, w_out_all[kind], j, grp["x"], mods[layer], grp["row"](out_tm), out_tm, out_tn)

    y_prompt = groups[0]["x"].reshape(x_prompt.shape)
    y_sample = groups[1]["x"].reshape(x_sample.shape)
    return (y_prompt, y_sample, new_a, jnp.stack(new_bk, axis=1), jnp.stack(new_bv, axis=1),
            jnp.stack(new_ck, axis=1), jnp.stack(new_cv, axis=1))
```

```python
import functools
import math

import jax
import jax.numpy as jnp
from jax import lax
from jax.experimental import pallas as pl
from jax.experimental.pallas import tpu as pltpu

F32 = jnp.float32
BF16 = jnp.bfloat16

D_MODEL = 2048
DEPTH = 4
GRID_W = 64
N_MIXERS = 3
EPS = 1e-6
ROPE_THETA = 10000.0
HEAD = 128
A_HEADS = D_MODEL // HEAD
B_HEADS = D_MODEL // (2 * HEAD)
C_HEADS = D_MODEL // HEAD
C_KV_HEADS = C_HEADS // 4
C_GROUP = C_HEADS // C_KV_HEADS
COND_ROWS = 8

V7X_VMEM_LIMIT_BYTES = 56 * 1024 * 1024
SCAN_CHUNK = 256
SCAN_HALF = SCAN_CHUNK // 2
SCAN_LEVELS = SCAN_CHUNK.bit_length() - 1
SCAN_IN_BUFS = 6
SCAN_BUFS = SCAN_IN_BUFS + SCAN_LEVELS
SCAN_CHUNKS_PER_ITER = 4
SCAN_GATE_ROWS = 64
LOG2_E = math.log2(math.e)
ATTN_KEY_CHUNK = 1024
ATTN_SCALE_LOG2 = (HEAD ** -0.5) * LOG2_E
BF16_SUBLANES = 16


def _cparams(n_axes):
    return pltpu.CompilerParams(
        dimension_semantics=("arbitrary",) * n_axes,
        vmem_limit_bytes=V7X_VMEM_LIMIT_BYTES,
    )


def _sigmoid(x):
    return 1.0 / (1.0 + jnp.exp(-x))


def _silu(x):
    return x * _sigmoid(x)


def _rms(x, w):
    return x * lax.rsqrt(jnp.mean(x * x, axis=-1, keepdims=True) + EPS) * w


def _mod_kernel(cond_ref, w_ref, b_ref, o_ref):
    s = _silu(cond_ref[...]).astype(BF16)
    o_ref[...] = jnp.dot(s, w_ref[...].astype(BF16), preferred_element_type=F32) + b_ref[...]


def _mod_rows(cond, mod_w, mod_b):
    d3 = 3 * D_MODEL
    tn = 768
    out = pl.pallas_call(
        _mod_kernel,
        grid=(DEPTH, d3 // tn),
        in_specs=[
            pl.BlockSpec((COND_ROWS, D_MODEL), lambda l, j: (0, 0)),
            pl.BlockSpec((None, D_MODEL, tn), lambda l, j: (l, 0, j)),
            pl.BlockSpec((None, 1, tn), lambda l, j: (l, 0, j)),
        ],
        out_specs=pl.BlockSpec((None, COND_ROWS, tn), lambda l, j: (l, 0, j)),
        out_shape=jax.ShapeDtypeStruct((DEPTH, COND_ROWS, d3), F32),
        compiler_params=_cparams(2),
        name="adaln_rows",
    )(cond, mod_w, mod_b.reshape(DEPTH, 1, d3))
    return out.reshape(DEPTH, COND_ROWS, 3, D_MODEL)


def _inproj_kernel(x_ref, mod_ref, nw_ref, w_ref, o_ref, h_ref):
    @pl.when(pl.program_id(1) == 0)
    def _():
        gain = nw_ref[...] * (1.0 + mod_ref[1:2, :])
        shift = mod_ref[0:1, :]

        def rows(ci, carry):
            r = pl.ds(pl.multiple_of(ci * BF16_SUBLANES, BF16_SUBLANES), BF16_SUBLANES)
            x = x_ref[r, :]
            inv = lax.rsqrt(jnp.mean(x * x, axis=-1, keepdims=True) + EPS)
            h_ref[r, :] = (x * inv * gain + shift).astype(BF16)
            return carry

        lax.fori_loop(0, x_ref.shape[0] // BF16_SUBLANES, rows, 0, unroll=4)

    o_ref[...] = jnp.dot(h_ref[...], w_ref[...], preferred_element_type=F32).astype(o_ref.dtype)


def _in_proj(x, mod, norm_w, w, layer, row_of_tile, tm, tn):
    t, n = x.shape[0], w.shape[2]
    return pl.pallas_call(
        _inproj_kernel,
        grid=(t // tm, n // tn),
        in_specs=[
            pl.BlockSpec((tm, D_MODEL), lambda i, j: (i, 0)),
            pl.BlockSpec((None, 3, D_MODEL), lambda i, j: (row_of_tile(i), 0, 0)),
            pl.BlockSpec((1, D_MODEL), lambda i, j: (0, 0)),
            pl.BlockSpec((None, D_MODEL, tn), lambda i, j: (layer, 0, j)),
        ],
        out_specs=pl.BlockSpec((tm, tn), lambda i, j: (i, j)),
        out_shape=jax.ShapeDtypeStruct((t, n), F32),
        scratch_shapes=[pltpu.VMEM((tm, D_MODEL), BF16)],
        compiler_params=_cparams(2),
        name="in_proj",
    )(x, mod, norm_w.reshape(1, D_MODEL), w)


def _outproj_kernel(o_ref, w_ref, x_ref, mod_ref, y_ref):
    acc = jnp.dot(o_ref[...], w_ref[...], preferred_element_type=F32)
    y_ref[...] = x_ref[...] + mod_ref[2:3, :] * acc


def _out_proj(o, w, layer, x, mod, row_of_tile, tm, tn):
    t, kdim = o.shape
    return pl.pallas_call(
        _outproj_kernel,
        grid=(t // tm, D_MODEL // tn),
        in_specs=[
            pl.BlockSpec((tm, kdim), lambda i, j: (i, 0)),
            pl.BlockSpec((None, kdim, tn), lambda i, j: (layer, 0, j)),
            pl.BlockSpec((tm, tn), lambda i, j: (i, j)),
            pl.BlockSpec((None, 3, tn), lambda i, j: (row_of_tile(i), 0, j)),
        ],
        out_specs=pl.BlockSpec((tm, tn), lambda i, j: (i, j)),
        out_shape=jax.ShapeDtypeStruct((t, D_MODEL), F32),
        compiler_params=_cparams(2),
        name="out_proj",
    )(o, w, x, mod)


def _rope_tables(n):
    pos = jnp.arange(n)
    quarter = HEAD // 4
    inv_freq = ROPE_THETA ** (-jnp.arange(quarter, dtype=F32) / quarter)

    def axis_angles(p):
        ang = p.astype(F32)[:, None] * inv_freq[None, :]
        return jnp.concatenate([ang, ang], axis=-1)

    ang = jnp.concatenate([axis_angles(pos // GRID_W), axis_angles(pos % GRID_W)], axis=-1)
    cos, sin = jnp.cos(ang), jnp.sin(ang)
    first = (jnp.arange(HEAD) % (2 * quarter)) < quarter
    return jnp.stack([cos, jnp.where(first, -sin, 0.0), jnp.where(first, 0.0, sin)])


def _rope(x, tab_ref, rows=slice(None)):
    up = pltpu.roll(x, HEAD - HEAD // 4, 1)
    down = pltpu.roll(x, HEAD // 4, 1)
    return x * tab_ref[0, rows, :] + up * tab_ref[1, rows, :] + down * tab_ref[2, rows, :]


def _kvprep_kernel(*refs, n_kheads, rope, emit_norm):
    k_ref, v_ref, kn_ref = refs[:3]
    pos = 3
    tab_ref = None
    if rope:
        tab_ref = refs[pos]
        pos += 1
    kh_ref, vt_ref = refs[pos], refs[pos + 1]
    kn_out, v_out = (refs[pos + 2], refs[pos + 3]) if emit_norm else (None, None)
    for h in range(n_kheads):
        sl = slice(h * HEAD, (h + 1) * HEAD)
        kn = _rms(k_ref[:, sl], kn_ref[...])
        if emit_norm:
            kn_out[:, sl] = kn
        if rope:
            kn = _rope(kn, tab_ref)
        kh_ref[:, sl] = kn.astype(BF16)
    v = v_ref[...]
    if emit_norm:
        v_out[...] = v
    vt_ref[...] = v.T.astype(BF16)


def _kv_prep(p, k_col, k_w, v_col, v_w, k_norm, rope_tab, n_batch, n_seq):
    t = p.shape[0]
    rope = rope_tab is not None
    emit_norm = not rope
    tr = min(512, n_seq)
    per_seq = n_seq // tr
    in_specs = [
        pl.BlockSpec((tr, k_w), lambda i: (i, k_col // k_w)),
        pl.BlockSpec((tr, v_w), lambda i: (i, v_col // v_w)),
        pl.BlockSpec((1, HEAD), lambda i: (0, 0)),
    ]
    args = [p, p, k_norm.reshape(1, HEAD)]
    if rope:
        in_specs.append(pl.BlockSpec((3, tr, HEAD), lambda i: (0, i % per_seq, 0)))
        args.append(rope_tab)
    out_specs = [
        pl.BlockSpec((tr, k_w), lambda i: (i, 0)),
        pl.BlockSpec((None, v_w, tr), lambda i: (i // per_seq, 0, i % per_seq)),
    ]
    out_shape = [jax.ShapeDtypeStruct((t, k_w), BF16), jax.ShapeDtypeStruct((n_batch, v_w, n_seq), BF16)]
    if emit_norm:
        out_specs += [pl.BlockSpec((tr, k_w), lambda i: (i, 0)), pl.BlockSpec((tr, v_w), lambda i: (i, 0))]
        out_shape += [jax.ShapeDtypeStruct((t, k_w), F32), jax.ShapeDtypeStruct((t, v_w), F32)]
    return pl.pallas_call(
        functools.partial(_kvprep_kernel, n_kheads=k_w // HEAD, rope=rope, emit_norm=emit_norm),
        grid=(t // tr,),
        in_specs=in_specs,
        out_specs=out_specs,
        out_shape=out_shape,
        compiler_params=_cparams(1),
        name="kv_prep",
    )(*args)


_NT = (((1,), (1,)), ((), ()))


def _key_chunks(k_ref, vt_ref, head, k_w, v_w):
    n = k_ref.shape[0]
    step = min(n, ATTN_KEY_CHUNK)
    ones = jnp.ones((BF16_SUBLANES, step), BF16)
    return [(k_ref[c0:c0 + step, head * k_w:(head + 1) * k_w],
             jnp.concatenate([vt_ref[head * v_w:(head + 1) * v_w, c0:c0 + step], ones], axis=0))
            for c0 in range(0, n, step)]


def _attend_chunks(problems, finish):
    pairs = [(i, j) for i, (_, chunks) in enumerate(problems) for j in range(len(chunks))]
    s, m, o = {}, {}, {}

    def score(t):
        i, j = pairs[t]
        q, chunks = problems[i]
        s[t] = lax.dot_general(chunks[j][0], q, _NT, preferred_element_type=F32)
        m[t] = s[t].max(axis=0, keepdims=True)

    for t in range(min(2, len(pairs))):
        score(t)
    for t, (i, j) in enumerate(pairs):
        e = jnp.exp2(s.pop(t) - m[t]).astype(BF16)
        if t + 2 < len(pairs):
            score(t + 2)
        chunks = problems[i][1]
        n = len(chunks)
        n_v = chunks[j][1].shape[0] - BF16_SUBLANES
        o[t] = jnp.dot(chunks[j][1], e, preferred_element_type=F32)
        if j == n - 1:
            ts = range(t - n + 1, t + 1)
            if n == 1:
                acc = o[t]
            else:
                m_all = functools.reduce(jnp.maximum, [m[u] for u in ts])
                acc = None
                for u in ts:
                    part = o[u] * jnp.exp2(m[u] - m_all)
                    acc = part if acc is None else acc + part
            finish(i, acc[:n_v], acc[n_v:n_v + 1])


def _diff_attn_kernel(*refs, has_ctx, lam_init, tq, heads):
    q_ref, g_ref, kh_ref, vt_ref, qn_ref, lamv_ref, sub_ref = refs[:7]
    w2 = 2 * HEAD
    pos = 7
    if has_ctx:
        tab_ref, ck_ref, cvt_ref = refs[pos:pos + 3]
        pos += 3
    o_ref = refs[pos]

    lv = lamv_ref[...]
    lam = (jnp.exp(jnp.sum(lv[0:1] * lv[1:2], keepdims=True))
           - jnp.exp(jnp.sum(lv[2:3] * lv[3:4], keepdims=True)) + lam_init)

    zero = jnp.zeros((tq, HEAD), BF16)
    n_blk = q_ref.shape[0] // tq
    problems = []
    for h in range(heads):
        chunks = _key_chunks(kh_ref, vt_ref, h, w2, w2) + (_key_chunks(ck_ref, cvt_ref, h, w2, w2) if has_ctx else [])
        for i in range(n_blk):
            r = slice(i * tq, (i + 1) * tq)
            rows = []
            for comp in range(2):
                q = _rms(q_ref[r, h * w2 + comp * HEAD:h * w2 + (comp + 1) * HEAD], qn_ref[...])
                if has_ctx:
                    q = _rope(q, tab_ref, r)
                q = (q * ATTN_SCALE_LOG2).astype(BF16)
                rows.append(jnp.concatenate([q, zero] if comp == 0 else [zero, q], axis=1))
            problems.append((jnp.concatenate(rows, axis=0), chunks))

    def finish(idx, acc, l):
        h, i = divmod(idx, n_blk)
        r = slice(i * tq, (i + 1) * tq)
        cols = slice(h * w2, (h + 1) * w2)
        o_t = acc[:, :tq] * (1.0 / l[:, :tq]) - acc[:, tq:] * (lam / l[:, tq:])
        o = _rms(o_t.T, sub_ref[...]) * (1.0 - lam_init)
        o_ref[r, cols] = (o * _silu(g_ref[r, cols])).astype(BF16)

    _attend_chunks(problems, finish)


def _diff_attention(p, kh, vt, q_norm, lam_vecs, subln, lam_init, n_batch, n_seq, tq, q_blocks=1, heads=1,
                    rope_tab=None, ctx=None):
    has_ctx = ctx is not None
    w2 = 2 * HEAD
    wc = heads * w2
    rows = tq * q_blocks
    nq = n_seq // rows
    assert (3 * D_MODEL) % wc == 0 and B_HEADS % heads == 0
    g_col0 = 3 * D_MODEL // wc
    in_specs = [
        pl.BlockSpec((rows, wc), lambda b, h, i: (b * nq + i, h)),
        pl.BlockSpec((rows, wc), lambda b, h, i: (b * nq + i, g_col0 + h)),
        pl.BlockSpec((n_seq, wc), lambda b, h, i: (b, h)),
        pl.BlockSpec((None, wc, n_seq), lambda b, h, i: (b, h, 0)),
        pl.BlockSpec((1, HEAD), lambda b, h, i: (0, 0)),
        pl.BlockSpec((4, HEAD), lambda b, h, i: (0, 0)),
        pl.BlockSpec((1, w2), lambda b, h, i: (0, 0)),
    ]
    args = [p, p, kh, vt, q_norm.reshape(1, HEAD), lam_vecs, subln.reshape(1, w2)]
    if has_ctx:
        ck, cvt = ctx
        n_ctx = ck.shape[1]
        in_specs += [
            pl.BlockSpec((3, rows, HEAD), lambda b, h, i: (0, i, 0)),
            pl.BlockSpec((None, n_ctx, wc), lambda b, h, i: (b, 0, h)),
            pl.BlockSpec((None, wc, n_ctx), lambda b, h, i: (b, h, 0)),
        ]
        args += [rope_tab, ck, cvt]
    return pl.pallas_call(
        functools.partial(_diff_attn_kernel, has_ctx=has_ctx, lam_init=lam_init, tq=tq, heads=heads),
        grid=(n_batch, B_HEADS // heads, nq),
        in_specs=in_specs,
        out_specs=pl.BlockSpec((rows, wc), lambda b, h, i: (b * nq + i, h)),
        out_shape=jax.ShapeDtypeStruct((n_batch * n_seq, D_MODEL), BF16),
        compiler_params=_cparams(3),
        name="diff_attention",
    )(*args)


def _gqa_kernel(*refs, has_ctx, tq, heads):
    q_ref, g_ref, kh_ref, vt_ref, qn_ref = refs[:5]
    wq = C_GROUP * HEAD
    pos = 5
    if has_ctx:
        tab_ref, ck_ref, cvt_ref = refs[pos:pos + 3]
        pos += 3
    o_ref = refs[pos]

    n_blk = q_ref.shape[0] // tq
    problems = []
    for h in range(heads):
        chunks = (_key_chunks(kh_ref, vt_ref, h, HEAD, HEAD)
                  + (_key_chunks(ck_ref, cvt_ref, h, HEAD, HEAD) if has_ctx else []))
        for i in range(n_blk):
            r = slice(i * tq, (i + 1) * tq)
            qs = []
            for gi in range(C_GROUP):
                c0 = h * wq + gi * HEAD
                q = _rms(q_ref[r, c0:c0 + HEAD], qn_ref[...])
                if has_ctx:
                    q = _rope(q, tab_ref, r)
                qs.append((q * ATTN_SCALE_LOG2).astype(BF16))
            problems.append((jnp.concatenate(qs, axis=0), chunks))

    def finish(idx, acc, l):
        h, i = divmod(idx, n_blk)
        r = slice(i * tq, (i + 1) * tq)
        o = (acc * (1.0 / l)).T
        for gi in range(C_GROUP):
            sl = slice(h * wq + gi * HEAD, h * wq + (gi + 1) * HEAD)
            o_ref[r, sl] = (o[gi * tq:(gi + 1) * tq] * _silu(g_ref[r, sl])).astype(BF16)

    _attend_chunks(problems, finish)


def _gqa_attention(p, kh, vt, q_norm, n_batch, n_seq, tq, q_blocks=1, heads=1, rope_tab=None, ctx=None):
    has_ctx = ctx is not None
    wq = heads * C_GROUP * HEAD
    wk = heads * HEAD
    rows = tq * q_blocks
    nq = n_seq // rows
    g_off = D_MODEL + 2 * C_KV_HEADS * HEAD
    assert g_off % wq == 0 and C_KV_HEADS % heads == 0
    g_col0 = g_off // wq
    in_specs = [
        pl.BlockSpec((rows, wq), lambda b, h, i: (b * nq + i, h)),
        pl.BlockSpec((rows, wq), lambda b, h, i: (b * nq + i, g_col0 + h)),
        pl.BlockSpec((n_seq, wk), lambda b, h, i: (b, h)),
        pl.BlockSpec((None, wk, n_seq), lambda b, h, i: (b, h, 0)),
        pl.BlockSpec((1, HEAD), lambda b, h, i: (0, 0)),
    ]
    args = [p, p, kh, vt, q_norm.reshape(1, HEAD)]
    if has_ctx:
        ck, cvt = ctx
        n_ctx = ck.shape[1]
        in_specs += [
            pl.BlockSpec((3, rows, HEAD), lambda b, h, i: (0, i, 0)),
            pl.BlockSpec((None, n_ctx, wk), lambda b, h, i: (b, 0, h)),
            pl.BlockSpec((None, wk, n_ctx), lambda b, h, i: (b, h, 0)),
        ]
        args += [rope_tab, ck, cvt]
    return pl.pallas_call(
        functools.partial(_gqa_kernel, has_ctx=has_ctx, tq=tq, heads=heads),
        grid=(n_batch, C_KV_HEADS // heads, nq),
        in_specs=in_specs,
        out_specs=pl.BlockSpec((rows, wq), lambda b, h, i: (b * nq + i, h)),
        out_shape=jax.ShapeDtypeStruct((n_batch * n_seq, D_MODEL), BF16),
        compiler_params=_cparams(3),
        name="gqa_attention",
    )(*args)


def _forget_gate(z, lb):
    t = jnp.exp2(jnp.abs(z) * -LOG2_E)
    u = 1.0 + t
    r = 1.0 / u
    tr = t * r
    pos = z >= 0.0
    sig = jnp.where(pos, r, tr)
    nsig = jnp.where(pos, tr, r)
    log2_sig = jnp.minimum(z, 0.0) * LOG2_E - jnp.log2(u)
    one_m = 1.0 - lb
    log2_f = jnp.where(lb > 0.0, jnp.log2(lb + one_m * sig), log2_sig)
    return log2_f, one_m * nsig


def _cumsum_rows(tri, x):
    hi = x.astype(BF16)
    mid = (x - hi.astype(F32)).astype(BF16)
    r = jnp.dot(tri, jnp.concatenate([hi, mid], axis=1), preferred_element_type=F32)
    return r[:, :HEAD] + r[:, HEAD:]


def _pair_rows(ref, b, odd):
    off = b if odd else 0
    if b >= 8:
        parts = [ref[pl.ds(i * 2 * b + off, b), :] for i in range(SCAN_HALF // b)]
    else:
        parts = [ref[pl.ds(off + r, SCAN_HALF // b, stride=2 * b), :] for r in range(b)]
    return parts[0] if len(parts) == 1 else jnp.concatenate(parts, axis=0)


def _pair_bcast(ref, b, row):
    if b >= 8:
        parts = [jnp.broadcast_to(ref[pl.ds(i * 2 * b + row, 1), :], (b, HEAD)) for i in range(SCAN_HALF // b)]
    else:
        parts = [ref[pl.ds(row, SCAN_HALF // b, stride=2 * b), :]] * b
    return parts[0] if len(parts) == 1 else jnp.concatenate(parts, axis=0)


def _pair_store(ref, b, odd, val):
    off = b if odd else 0
    if b >= 8:
        for i in range(SCAN_HALF // b):
            ref[pl.ds(i * 2 * b + off, b), :] = val[i * b:(i + 1) * b]
    else:
        n = SCAN_HALF // b
        for r in range(b):
            ref[pl.ds(off + r, n, stride=2 * b), :] = val[r * n:(r + 1) * n]


def _hgrn_kernel(*refs, n_chunks, has_state):
    q_ref, zf_ref, zb_ref, v_ref, g_ref, lb_ref, on_ref = refs[:7]
    s0_ref = refs[7] if has_state else None
    pos = 8
    o_ref = refs[pos]
    pos += 1
    sout_ref = None
    if not has_state:
        sout_ref = refs[pos]
        pos += 1
    tri_ref, mask_ref = refs[pos:pos + 2]
    pos += 2
    buf_sets = [refs[pos + k * SCAN_BUFS:pos + (k + 1) * SCAN_BUFS] for k in range(SCAN_CHUNKS_PER_ITER)]
    pos += SCAN_CHUNKS_PER_ITER * SCAN_BUFS
    if has_state:
        oall, qb_all, klb_all, decb_all = refs[pos:pos + 4]

    c = SCAN_CHUNK
    row = lax.broadcasted_iota(jnp.int32, (c, c), 0)
    col = lax.broadcasted_iota(jnp.int32, (c, c), 1)
    tri_ref[0] = (col <= row).astype(BF16)
    tri_ref[1] = (col >= row).astype(BF16)
    prow = lax.broadcasted_iota(jnp.int32, (SCAN_HALF, SCAN_HALF), 0)
    pcol = lax.broadcasted_iota(jnp.int32, (SCAN_HALF, SCAN_HALF), 1)
    for j in range(SCAN_LEVELS):
        b = 1 << j
        if b >= 8:
            same = (prow // b) == (pcol // b)
        else:
            same = (prow % (SCAN_HALF // b)) == (pcol % (SCAN_HALF // b))
        mask_ref[j] = same.astype(F32).astype(BF16)

    lb_f = lb_ref[0:1, :]
    lb_b = lb_ref[1:2, :]

    def gates(ci, bufs):
        sq, sv, skf, scf, skb, scb = bufs[:SCAN_IN_BUFS]
        for r0 in range(0, c, SCAN_GATE_ROWS):
            src = pl.ds(pl.multiple_of(ci * c + r0, SCAN_GATE_ROWS), SCAN_GATE_ROWS)
            dst = slice(r0, r0 + SCAN_GATE_ROWS)
            sq[dst, :] = _silu(q_ref[src, :])
            sv[dst, :] = v_ref[src, :]
            scf[dst, :], skf[dst, :] = _forget_gate(zf_ref[src, :], lb_f)
            scb[dst, :], skb[dst, :] = _forget_gate(zb_ref[src, :], lb_b)
        scf[...] = _cumsum_rows(tri_ref[0], scf[...])
        scb[...] = _cumsum_rows(tri_ref[1], scb[...])

    def pair_scores(bufs):
        sq, sv, skf, scf, skb, scb = bufs[:SCAN_IN_BUFS]

        def masked(z, j):
            z = z.astype(BF16)
            return z if j == SCAN_LEVELS - 1 else z * mask_ref[j]

        scores = []
        for j in range(SCAN_LEVELS):
            b = 1 << j
            edge = _pair_bcast(scf, b, b - 1)
            qt = _pair_rows(sq, b, True) * jnp.exp2(_pair_rows(scf, b, True) - edge)
            kt = _pair_rows(skf, b, False) * jnp.exp2(edge - _pair_rows(scf, b, False))
            z = lax.dot_general(qt.astype(BF16), kt.astype(BF16), _NT, preferred_element_type=F32)
            scores.append((j, True, masked(z, j)))
            edge = _pair_bcast(scb, b, b)
            qt = _pair_rows(sq, b, False) * jnp.exp2(_pair_rows(scb, b, False) - edge)
            kt = _pair_rows(skb, b, True) * jnp.exp2(edge - _pair_rows(scb, b, True))
            z = lax.dot_general(qt.astype(BF16), kt.astype(BF16), _NT, preferred_element_type=F32)
            scores.append((j, False, masked(z, j)))
        return scores

    def pair_values(bufs, scores):
        sv, lvl_out = bufs[1], bufs[SCAN_IN_BUFS:]
        for j, q_odd, z in scores:
            b = 1 << j
            _pair_store(lvl_out[j], b, q_odd, jnp.dot(z, _pair_rows(sv, b, not q_odd).astype(BF16),
                                                      preferred_element_type=F32))

    def chunk_result(bufs):
        q, v, kf, cf, kb, cb = (buf[...] for buf in bufs[:SCAN_IN_BUFS])
        o = jnp.sum(q * (kf + kb), axis=-1, keepdims=True) * v
        for lvl in bufs[SCAN_IN_BUFS:]:
            o = o + lvl[...]
        tot_f = cf[c - 1:c, :]
        tot_b = cb[0:1, :]
        klf = kf * jnp.exp2(tot_f - cf)
        klb = kb * jnp.exp2(tot_b - cb)
        return o, q, v, cf, cb, tot_f, tot_b, klf, klb

    def intra(cis):
        for ci, bufs in zip(cis, buf_sets):
            gates(ci, bufs)
        scores = [pair_scores(bufs) for bufs in buf_sets]
        for bufs, sc in zip(buf_sets, scores):
            pair_values(bufs, sc)
        return [chunk_result(bufs) for bufs in buf_sets]

    def finish(r, o):
        o = _rms(o, on_ref[...]) * _silu(g_ref[r, :])
        o_ref[r, :] = o.astype(BF16)

    def rows_of(ci):
        return pl.ds(pl.multiple_of(ci * c, c), c)

    per_iter = SCAN_CHUNKS_PER_ITER
    n_iters = n_chunks // per_iter

    if not has_state:
        def seqs(it, carry):
            cis = [it * per_iter + k for k in range(per_iter)]
            for ci, res in zip(cis, intra(cis)):
                o, q, v, cf, cb, tot_f, tot_b, klf, klb = res
                v_b = v.astype(BF16)
                sout_ref[ci, 0] = jnp.dot(klf.T.astype(BF16), v_b, preferred_element_type=F32)
                sout_ref[ci, 1] = jnp.dot(klb.T.astype(BF16), v_b, preferred_element_type=F32)
                finish(rows_of(ci), o)
            return carry

        lax.fori_loop(0, n_iters, seqs, 0)
        return

    def row_to_col(x):
        return jnp.broadcast_to(x, (HEAD, HEAD)).T

    def fwd(it, s):
        cis = [it * per_iter + k for k in range(per_iter)]
        for ci, res in zip(cis, intra(cis)):
            o, q, v, cf, cb, tot_f, tot_b, klf, klb = res
            r = rows_of(ci)
            oall[r, :] = o + jnp.dot((q * jnp.exp2(cf)).astype(BF16), s.astype(BF16),
                                     preferred_element_type=F32)
            qb_all[r, :] = (q * jnp.exp2(cb)).astype(BF16)
            klb_all[r, :] = klb
            decb_all[pl.ds(pl.multiple_of(ci * 8, 8), 1), :] = jnp.exp2(tot_b)
            s = row_to_col(jnp.exp2(tot_f)) * s + jnp.dot(klf.T.astype(BF16), v.astype(BF16),
                                                          preferred_element_type=F32)
        return s

    def bwd(i, s):
        ci = n_chunks - 1 - i
        r = rows_of(ci)
        finish(r, oall[r, :] + jnp.dot(qb_all[r, :], s.astype(BF16), preferred_element_type=F32))
        dec = decb_all[pl.ds(pl.multiple_of(ci * 8, 8), 1), :]
        return row_to_col(dec) * s + jnp.dot(klb_all[r, :].T.astype(BF16), v_ref[r, :].astype(BF16),
                                             preferred_element_type=F32)

    lax.fori_loop(0, n_iters, fwd, s0_ref[0])
    lax.fori_loop(0, n_chunks, bwd, s0_ref[1], unroll=4)


def _hgrn_scan(p, lb, o_norm, n_batch, n_seq, state=None, final_states=None, slot=0, seqs_per_step=8):
    has_state = state is not None
    c = SCAN_CHUNK
    if has_state:
        rows, n_steps = n_seq, n_batch
    else:
        assert n_seq == c and n_batch % seqs_per_step == 0
        rows, n_steps = seqs_per_step * c, n_batch // seqs_per_step
    n_chunks = rows // c
    col = lambda s: (lambda b, h: (b, s * A_HEADS + h))
    in_specs = [pl.BlockSpec((rows, HEAD), col(s)) for s in range(5)]
    in_specs += [
        pl.BlockSpec((2, HEAD), lambda b, h: (0, h)),
        pl.BlockSpec((1, HEAD), lambda b, h: (0, 0)),
    ]
    args = [p] * 5 + [lb, o_norm.reshape(1, HEAD)]
    out_specs = [pl.BlockSpec((rows, HEAD), lambda b, h: (b, h))]
    out_shape = [jax.ShapeDtypeStruct((n_batch * n_seq, D_MODEL), BF16)]
    scratch = [pltpu.VMEM((2, c, c), BF16), pltpu.VMEM((SCAN_LEVELS, SCAN_HALF, SCAN_HALF), BF16)]
    assert n_chunks % SCAN_CHUNKS_PER_ITER == 0
    scratch += [pltpu.VMEM((c, HEAD), F32)] * (SCAN_CHUNKS_PER_ITER * SCAN_BUFS)
    if has_state:
        in_specs.append(pl.BlockSpec((None, 2, None, HEAD, HEAD), lambda b, h: (b, 0, h, 0, 0)))
        args.append(state)
        scratch += [pltpu.VMEM((rows, HEAD), F32), pltpu.VMEM((rows, HEAD), BF16),
                    pltpu.VMEM((rows, HEAD), F32), pltpu.VMEM((n_chunks * 8, HEAD), F32)]
    else:
        in_specs.append(pl.BlockSpec(memory_space=pl.ANY))
        args.append(final_states)
        out_specs.append(pl.BlockSpec((n_chunks, None, 2, None, HEAD, HEAD), lambda b, h: (b, slot, 0, h, 0, 0)))
        out_shape.append(jax.ShapeDtypeStruct(final_states.shape, F32))
    res = pl.pallas_call(
        functools.partial(_hgrn_kernel, n_chunks=n_chunks, has_state=has_state),
        grid=(n_steps, A_HEADS),
        in_specs=in_specs,
        out_specs=out_specs,
        out_shape=out_shape,
        input_output_aliases={} if has_state else {len(args) - 1: 1},
        scratch_shapes=scratch,
        compiler_params=_cparams(2),
        name="hgrn2_scan",
    )(*args)
    return (res[0], None) if has_state else (res[0], res[1])


def _diff_lambda_init(layer):
    return 0.8 - 0.6 * math.exp(-0.3 * layer)


def kernel(x_prompt, x_sample, state_a, cache_b_k, cache_b_v, cache_c_k, cache_c_v, c, c_ctx, norm_w, mod_w, mod_b, a_w_in, a_w_out, a_o_norm, a_lower_bound, b_w_in, b_w_out, b_q_norm, b_k_norm, b_lambda, b_subln, c_w_in, c_w_out, c_q_norm, c_k_norm):
    n_ctx_b, n_ctx_s = x_prompt.shape[:2]
    n_lat_b, n_lat_s = x_sample.shape[:2]
    past = cache_b_k.shape[2]
    in_tm, in_tn = 1024, 1024
    out_tm, out_tn = 512, D_MODEL

    lb_all = jnp.cumsum(jax.nn.softmax(a_lower_bound.astype(F32), axis=0), axis=0)
    lb_all = lb_all - lb_all[0:1]

    cond = jnp.zeros((COND_ROWS, D_MODEL), F32).at[0].set(c_ctx).at[1:1 + n_lat_b].set(c)
    mods = _mod_rows(cond, mod_w, mod_b)
    rope_tab = _rope_tables(n_lat_s)

    groups = [
        dict(x=x_prompt.reshape(-1, D_MODEL), nb=n_ctx_b, ns=n_ctx_s, latent=False,
             row=lambda tm: (lambda i: 0)),
        dict(x=x_sample.reshape(-1, D_MODEL), nb=n_lat_b, ns=n_lat_s, latent=True,
             row=lambda tm: (lambda i: 1 + i // (n_lat_s // tm))),
    ]
    w_in_all = [w.astype(BF16) for w in (a_w_in, b_w_in, c_w_in)]
    w_out_all = [w.astype(BF16) for w in (a_w_out, b_w_out, c_w_out)]
    n_a_layers = a_w_in.shape[0]
    new_a = jnp.zeros((n_ctx_b, n_a_layers, 2, A_HEADS, HEAD, HEAD), F32)
    new_bk, new_bv, new_ck, new_cv = [], [], [], []

    for layer in range(DEPTH):
        kind, j = layer % N_MIXERS, layer // N_MIXERS
        for grp in groups:
            nb, ns, latent = grp["nb"], grp["ns"], grp["latent"]
            p = _in_proj(grp["x"], mods[layer], norm_w[layer], w_in_all[kind], j, grp["row"](in_tm), in_tm, in_tn)
            if kind == 0:
                if latent:
                    o, _ = _hgrn_scan(p, lb_all[layer], a_o_norm[j], nb, ns, state=state_a[:, j])
                else:
                    o, new_a = _hgrn_scan(p, lb_all[layer], a_o_norm[j], nb, ns, final_states=new_a, slot=j)
            elif kind == 1:
                tab = rope_tab if latent else None
                res = _kv_prep(p, D_MODEL, D_MODEL, 2 * D_MODEL, D_MODEL, b_k_norm[j], tab, nb, ns)
                ctx = None
                if latent:
                    ctx = (cache_b_k[:, j].reshape(nb, past, D_MODEL).astype(BF16),
                           jnp.swapaxes(cache_b_v[:, j].reshape(nb, past, D_MODEL), 1, 2).astype(BF16))
                else:
                    new_bk.append(res[2].reshape(nb, ns, B_HEADS, 2, HEAD))
                    new_bv.append(res[3].reshape(nb, ns, B_HEADS, 2 * HEAD))
                o = _diff_attention(p, res[0], res[1], b_q_norm[j], b_lambda[j], b_subln[j],
                                    _diff_lambda_init(layer), nb, ns, 512 if latent else 256,
                                    q_blocks=4 if latent else 1, heads=1 if latent else B_HEADS, rope_tab=tab, ctx=ctx)
            else:
                kvw = C_KV_HEADS * HEAD
                tab = rope_tab if latent else None
                res = _kv_prep(p, D_MODEL, kvw, D_MODEL + kvw, kvw, c_k_norm[j], tab, nb, ns)
                ctx = None
                if latent:
                    ctx = (cache_c_k[:, j].reshape(nb, past, kvw).astype(BF16),
                           jnp.swapaxes(cache_c_v[:, j].reshape(nb, past, kvw), 1, 2).astype(BF16))
                else:
                    new_ck.append(res[2].reshape(nb, ns, C_KV_HEADS, HEAD))
                    new_cv.append(res[3].reshape(nb, ns, C_KV_HEADS, HEAD))
                o = _gqa_attention(p, res[0], res[1], c_q_norm[j], nb, ns, 256, q_blocks=4 if latent else 1, heads=1 if latent else 2,
                                   rope_tab=tab, ctx=ctx)
            grp["x"] = _out_proj(o, w_out_all[kind], j, grp["x"], mods[layer], grp["row"](out_tm), out_tm, out_tn)

    y_prompt = groups[0]["x"].reshape(x_prompt.shape)
    y_sample = groups[1]["x"].reshape(x_sample.shape)
    return (y_prompt, y_sample, new_a, jnp.stack(new_bk, axis=1), jnp.stack(new_bv, axis=1),
            jnp.stack(new_ck, axis=1), jnp.stack(new_cv, axis=1))
```

```python
import functools
import math

import jax
import jax.numpy as jnp
from jax import lax
from jax.experimental import pallas as pl
from jax.experimental.pallas import tpu as pltpu

F32 = jnp.float32
BF16 = jnp.bfloat16

D_MODEL = 2048
DEPTH = 4
GRID_W = 64
N_MIXERS = 3
EPS = 1e-6
ROPE_THETA = 10000.0
HEAD = 128
A_HEADS = D_MODEL // HEAD
B_HEADS = D_MODEL // (2 * HEAD)
C_HEADS = D_MODEL // HEAD
C_KV_HEADS = C_HEADS // 4
C_GROUP = C_HEADS // C_KV_HEADS
COND_ROWS = 8

V7X_VMEM_LIMIT_BYTES = 56 * 1024 * 1024
SCAN_CHUNK = 256
SCAN_HALF = SCAN_CHUNK // 2
SCAN_LEVELS = SCAN_CHUNK.bit_length() - 1
SCAN_IN_BUFS = 6
SCAN_BUFS = SCAN_IN_BUFS + SCAN_LEVELS
SCAN_CHUNKS_PER_ITER = 4
SCAN_GATE_ROWS = 64
SCAN_SCORE_LOOKAHEAD = 8
LOG2_E = math.log2(math.e)
ATTN_KEY_CHUNK = 1024
ATTN_SCALE_LOG2 = (HEAD ** -0.5) * LOG2_E
BF16_SUBLANES = 16


def _cparams(n_axes):
    return pltpu.CompilerParams(
        dimension_semantics=("arbitrary",) * n_axes,
        vmem_limit_bytes=V7X_VMEM_LIMIT_BYTES,
    )


def _sigmoid(x):
    return 1.0 / (1.0 + jnp.exp(-x))


def _silu(x):
    return x * _sigmoid(x)


def _rms(x, w):
    return x * lax.rsqrt(jnp.mean(x * x, axis=-1, keepdims=True) + EPS) * w


def _mod_kernel(cond_ref, w_ref, b_ref, o_ref):
    s = _silu(cond_ref[...]).astype(BF16)
    o_ref[...] = jnp.dot(s, w_ref[...].astype(BF16), preferred_element_type=F32) + b_ref[...]


def _mod_rows(cond, mod_w, mod_b):
    d3 = 3 * D_MODEL
    tn = 768
    out = pl.pallas_call(
        _mod_kernel,
        grid=(DEPTH, d3 // tn),
        in_specs=[
            pl.BlockSpec((COND_ROWS, D_MODEL), lambda l, j: (0, 0)),
            pl.BlockSpec((None, D_MODEL, tn), lambda l, j: (l, 0, j)),
            pl.BlockSpec((None, 1, tn), lambda l, j: (l, 0, j)),
        ],
        out_specs=pl.BlockSpec((None, COND_ROWS, tn), lambda l, j: (l, 0, j)),
        out_shape=jax.ShapeDtypeStruct((DEPTH, COND_ROWS, d3), F32),
        compiler_params=_cparams(2),
        name="adaln_rows",
    )(cond, mod_w, mod_b.reshape(DEPTH, 1, d3))
    return out.reshape(DEPTH, COND_ROWS, 3, D_MODEL)


def _inproj_kernel(x_ref, mod_ref, nw_ref, w_ref, o_ref, h_ref):
    @pl.when(pl.program_id(1) == 0)
    def _():
        gain = nw_ref[...] * (1.0 + mod_ref[1:2, :])
        shift = mod_ref[0:1, :]

        def rows(ci, carry):
            r = pl.ds(pl.multiple_of(ci * BF16_SUBLANES, BF16_SUBLANES), BF16_SUBLANES)
            x = x_ref[r, :]
            inv = lax.rsqrt(jnp.mean(x * x, axis=-1, keepdims=True) + EPS)
            h_ref[r, :] = (x * inv * gain + shift).astype(BF16)
            return carry

        lax.fori_loop(0, x_ref.shape[0] // BF16_SUBLANES, rows, 0, unroll=4)

    o_ref[...] = jnp.dot(h_ref[...], w_ref[...], preferred_element_type=F32).astype(o_ref.dtype)


def _in_proj(x, mod, norm_w, w, layer, row_of_tile, tm, tn):
    t, n = x.shape[0], w.shape[2]
    return pl.pallas_call(
        _inproj_kernel,
        grid=(t // tm, n // tn),
        in_specs=[
            pl.BlockSpec((tm, D_MODEL), lambda i, j: (i, 0)),
            pl.BlockSpec((None, 3, D_MODEL), lambda i, j: (row_of_tile(i), 0, 0)),
            pl.BlockSpec((1, D_MODEL), lambda i, j: (0, 0)),
            pl.BlockSpec((None, D_MODEL, tn), lambda i, j: (layer, 0, j)),
        ],
        out_specs=pl.BlockSpec((tm, tn), lambda i, j: (i, j)),
        out_shape=jax.ShapeDtypeStruct((t, n), F32),
        scratch_shapes=[pltpu.VMEM((tm, D_MODEL), BF16)],
        compiler_params=_cparams(2),
        name="in_proj",
    )(x, mod, norm_w.reshape(1, D_MODEL), w)


def _outproj_kernel(o_ref, w_ref, x_ref, mod_ref, y_ref):
    acc = jnp.dot(o_ref[...], w_ref[...], preferred_element_type=F32)
    y_ref[...] = x_ref[...] + mod_ref[2:3, :] * acc


def _out_proj(o, w, layer, x, mod, row_of_tile, tm, tn):
    t, kdim = o.shape
    return pl.pallas_call(
        _outproj_kernel,
        grid=(t // tm, D_MODEL // tn),
        in_specs=[
            pl.BlockSpec((tm, kdim), lambda i, j: (i, 0)),
            pl.BlockSpec((None, kdim, tn), lambda i, j: (layer, 0, j)),
            pl.BlockSpec((tm, tn), lambda i, j: (i, j)),
            pl.BlockSpec((None, 3, tn), lambda i, j: (row_of_tile(i), 0, j)),
        ],
        out_specs=pl.BlockSpec((tm, tn), lambda i, j: (i, j)),
        out_shape=jax.ShapeDtypeStruct((t, D_MODEL), F32),
        compiler_params=_cparams(2),
        name="out_proj",
    )(o, w, x, mod)


def _rope_tables(n):
    pos = jnp.arange(n)
    quarter = HEAD // 4
    inv_freq = ROPE_THETA ** (-jnp.arange(quarter, dtype=F32) / quarter)

    def axis_angles(p):
        ang = p.astype(F32)[:, None] * inv_freq[None, :]
        return jnp.concatenate([ang, ang], axis=-1)

    ang = jnp.concatenate([axis_angles(pos // GRID_W), axis_angles(pos % GRID_W)], axis=-1)
    cos, sin = jnp.cos(ang), jnp.sin(ang)
    first = (jnp.arange(HEAD) % (2 * quarter)) < quarter
    return jnp.stack([cos, jnp.where(first, -sin, 0.0), jnp.where(first, 0.0, sin)])


def _rope(x, tab_ref, rows=slice(None)):
    up = pltpu.roll(x, HEAD - HEAD // 4, 1)
    down = pltpu.roll(x, HEAD // 4, 1)
    return x * tab_ref[0, rows, :] + up * tab_ref[1, rows, :] + down * tab_ref[2, rows, :]


def _kvprep_kernel(*refs, n_kheads, rope, emit_norm):
    k_ref, v_ref, kn_ref = refs[:3]
    pos = 3
    tab_ref = None
    if rope:
        tab_ref = refs[pos]
        pos += 1
    kh_ref, vt_ref = refs[pos], refs[pos + 1]
    kn_out, v_out = (refs[pos + 2], refs[pos + 3]) if emit_norm else (None, None)
    for h in range(n_kheads):
        sl = slice(h * HEAD, (h + 1) * HEAD)
        kn = _rms(k_ref[:, sl], kn_ref[...])
        if emit_norm:
            kn_out[:, sl] = kn
        if rope:
            kn = _rope(kn, tab_ref)
        kh_ref[:, sl] = kn.astype(BF16)
    v = v_ref[...]
    if emit_norm:
        v_out[...] = v
    vt_ref[...] = v.T.astype(BF16)


def _kv_prep(p, k_col, k_w, v_col, v_w, k_norm, rope_tab, n_batch, n_seq):
    t = p.shape[0]
    rope = rope_tab is not None
    emit_norm = not rope
    tr = min(512, n_seq)
    per_seq = n_seq // tr
    in_specs = [
        pl.BlockSpec((tr, k_w), lambda i: (i, k_col // k_w)),
        pl.BlockSpec((tr, v_w), lambda i: (i, v_col // v_w)),
        pl.BlockSpec((1, HEAD), lambda i: (0, 0)),
    ]
    args = [p, p, k_norm.reshape(1, HEAD)]
    if rope:
        in_specs.append(pl.BlockSpec((3, tr, HEAD), lambda i: (0, i % per_seq, 0)))
        args.append(rope_tab)
    out_specs = [
        pl.BlockSpec((tr, k_w), lambda i: (i, 0)),
        pl.BlockSpec((None, v_w, tr), lambda i: (i // per_seq, 0, i % per_seq)),
    ]
    out_shape = [jax.ShapeDtypeStruct((t, k_w), BF16), jax.ShapeDtypeStruct((n_batch, v_w, n_seq), BF16)]
    if emit_norm:
        out_specs += [pl.BlockSpec((tr, k_w), lambda i: (i, 0)), pl.BlockSpec((tr, v_w), lambda i: (i, 0))]
        out_shape += [jax.ShapeDtypeStruct((t, k_w), F32), jax.ShapeDtypeStruct((t, v_w), F32)]
    return pl.pallas_call(
        functools.partial(_kvprep_kernel, n_kheads=k_w // HEAD, rope=rope, emit_norm=emit_norm),
        grid=(t // tr,),
        in_specs=in_specs,
        out_specs=out_specs,
        out_shape=out_shape,
        compiler_params=_cparams(1),
        name="kv_prep",
    )(*args)


_NT = (((1,), (1,)), ((), ()))


def _key_chunks(k_ref, vt_ref, head, k_w, v_w):
    n = k_ref.shape[0]
    step = min(n, ATTN_KEY_CHUNK)
    ones = jnp.ones((BF16_SUBLANES, step), BF16)
    return [(k_ref[c0:c0 + step, head * k_w:(head + 1) * k_w],
             jnp.concatenate([vt_ref[head * v_w:(head + 1) * v_w, c0:c0 + step], ones], axis=0))
            for c0 in range(0, n, step)]


def _attend_chunks(problems, finish):
    pairs = [(i, j) for i, (_, chunks) in enumerate(problems) for j in range(len(chunks))]
    s, m, o = {}, {}, {}

    def score(t):
        i, j = pairs[t]
        q, chunks = problems[i]
        s[t] = lax.dot_general(chunks[j][0], q, _NT, preferred_element_type=F32)
        m[t] = s[t].max(axis=0, keepdims=True)

    for t in range(min(2, len(pairs))):
        score(t)
    for t, (i, j) in enumerate(pairs):
        e = jnp.exp2(s.pop(t) - m[t]).astype(BF16)
        if t + 2 < len(pairs):
            score(t + 2)
        chunks = problems[i][1]
        n = len(chunks)
        n_v = chunks[j][1].shape[0] - BF16_SUBLANES
        o[t] = jnp.dot(chunks[j][1], e, preferred_element_type=F32)
        if j == n - 1:
            ts = range(t - n + 1, t + 1)
            if n == 1:
                acc = o[t]
            else:
                m_all = functools.reduce(jnp.maximum, [m[u] for u in ts])
                acc = None
                for u in ts:
                    part = o[u] * jnp.exp2(m[u] - m_all)
                    acc = part if acc is None else acc + part
            finish(i, acc[:n_v], acc[n_v:n_v + 1])


def _diff_attn_kernel(*refs, has_ctx, lam_init, tq, heads):
    q_ref, g_ref, kh_ref, vt_ref, qn_ref, lamv_ref, sub_ref = refs[:7]
    w2 = 2 * HEAD
    pos = 7
    if has_ctx:
        tab_ref, ck_ref, cvt_ref = refs[pos:pos + 3]
        pos += 3
    o_ref = refs[pos]

    lv = lamv_ref[...]
    lam = (jnp.exp(jnp.sum(lv[0:1] * lv[1:2], keepdims=True))
           - jnp.exp(jnp.sum(lv[2:3] * lv[3:4], keepdims=True)) + lam_init)

    zero = jnp.zeros((tq, HEAD), BF16)
    n_blk = q_ref.shape[0] // tq
    problems = []
    for h in range(heads):
        chunks = _key_chunks(kh_ref, vt_ref, h, w2, w2) + (_key_chunks(ck_ref, cvt_ref, h, w2, w2) if has_ctx else [])
        for i in range(n_blk):
            r = slice(i * tq, (i + 1) * tq)
            rows = []
            for comp in range(2):
                q = _rms(q_ref[r, h * w2 + comp * HEAD:h * w2 + (comp + 1) * HEAD], qn_ref[...])
                if has_ctx:
                    q = _rope(q, tab_ref, r)
                q = (q * ATTN_SCALE_LOG2).astype(BF16)
                rows.append(jnp.concatenate([q, zero] if comp == 0 else [zero, q], axis=1))
            problems.append((jnp.concatenate(rows, axis=0), chunks))

    def finish(idx, acc, l):
        h, i = divmod(idx, n_blk)
        r = slice(i * tq, (i + 1) * tq)
        cols = slice(h * w2, (h + 1) * w2)
        o_t = acc[:, :tq] * (1.0 / l[:, :tq]) - acc[:, tq:] * (lam / l[:, tq:])
        o = _rms(o_t.T, sub_ref[...]) * (1.0 - lam_init)
        o_ref[r, cols] = (o * _silu(g_ref[r, cols])).astype(BF16)

    _attend_chunks(problems, finish)


def _diff_attention(p, kh, vt, q_norm, lam_vecs, subln, lam_init, n_batch, n_seq, tq, q_blocks=1, heads=1,
                    rope_tab=None, ctx=None):
    has_ctx = ctx is not None
    w2 = 2 * HEAD
    wc = heads * w2
    rows = tq * q_blocks
    nq = n_seq // rows
    assert (3 * D_MODEL) % wc == 0 and B_HEADS % heads == 0
    g_col0 = 3 * D_MODEL // wc
    in_specs = [
        pl.BlockSpec((rows, wc), lambda b, h, i: (b * nq + i, h)),
        pl.BlockSpec((rows, wc), lambda b, h, i: (b * nq + i, g_col0 + h)),
        pl.BlockSpec((n_seq, wc), lambda b, h, i: (b, h)),
        pl.BlockSpec((None, wc, n_seq), lambda b, h, i: (b, h, 0)),
        pl.BlockSpec((1, HEAD), lambda b, h, i: (0, 0)),
        pl.BlockSpec((4, HEAD), lambda b, h, i: (0, 0)),
        pl.BlockSpec((1, w2), lambda b, h, i: (0, 0)),
    ]
    args = [p, p, kh, vt, q_norm.reshape(1, HEAD), lam_vecs, subln.reshape(1, w2)]
    if has_ctx:
        ck, cvt = ctx
        n_ctx = ck.shape[1]
        in_specs += [
            pl.BlockSpec((3, rows, HEAD), lambda b, h, i: (0, i, 0)),
            pl.BlockSpec((None, n_ctx, wc), lambda b, h, i: (b, 0, h)),
            pl.BlockSpec((None, wc, n_ctx), lambda b, h, i: (b, h, 0)),
        ]
        args += [rope_tab, ck, cvt]
    return pl.pallas_call(
        functools.partial(_diff_attn_kernel, has_ctx=has_ctx, lam_init=lam_init, tq=tq, heads=heads),
        grid=(n_batch, B_HEADS // heads, nq),
        in_specs=in_specs,
        out_specs=pl.BlockSpec((rows, wc), lambda b, h, i: (b * nq + i, h)),
        out_shape=jax.ShapeDtypeStruct((n_batch * n_seq, D_MODEL), BF16),
        compiler_params=_cparams(3),
        name="diff_attention",
    )(*args)


def _gqa_kernel(*refs, has_ctx, tq, heads):
    q_ref, g_ref, kh_ref, vt_ref, qn_ref = refs[:5]
    wq = C_GROUP * HEAD
    pos = 5
    if has_ctx:
        tab_ref, ck_ref, cvt_ref = refs[pos:pos + 3]
        pos += 3
    o_ref = refs[pos]

    n_blk = q_ref.shape[0] // tq
    problems = []
    for h in range(heads):
        chunks = (_key_chunks(kh_ref, vt_ref, h, HEAD, HEAD)
                  + (_key_chunks(ck_ref, cvt_ref, h, HEAD, HEAD) if has_ctx else []))
        for i in range(n_blk):
            r = slice(i * tq, (i + 1) * tq)
            qs = []
            for gi in range(C_GROUP):
                c0 = h * wq + gi * HEAD
                q = _rms(q_ref[r, c0:c0 + HEAD], qn_ref[...])
                if has_ctx:
                    q = _rope(q, tab_ref, r)
                qs.append((q * ATTN_SCALE_LOG2).astype(BF16))
            problems.append((jnp.concatenate(qs, axis=0), chunks))

    def finish(idx, acc, l):
        h, i = divmod(idx, n_blk)
        r = slice(i * tq, (i + 1) * tq)
        o = (acc * (1.0 / l)).T
        for gi in range(C_GROUP):
            sl = slice(h * wq + gi * HEAD, h * wq + (gi + 1) * HEAD)
            o_ref[r, sl] = (o[gi * tq:(gi + 1) * tq] * _silu(g_ref[r, sl])).astype(BF16)

    _attend_chunks(problems, finish)


def _gqa_attention(p, kh, vt, q_norm, n_batch, n_seq, tq, q_blocks=1, heads=1, rope_tab=None, ctx=None):
    has_ctx = ctx is not None
    wq = heads * C_GROUP * HEAD
    wk = heads * HEAD
    rows = tq * q_blocks
    nq = n_seq // rows
    g_off = D_MODEL + 2 * C_KV_HEADS * HEAD
    assert g_off % wq == 0 and C_KV_HEADS % heads == 0
    g_col0 = g_off // wq
    in_specs = [
        pl.BlockSpec((rows, wq), lambda b, h, i: (b * nq + i, h)),
        pl.BlockSpec((rows, wq), lambda b, h, i: (b * nq + i, g_col0 + h)),
        pl.BlockSpec((n_seq, wk), lambda b, h, i: (b, h)),
        pl.BlockSpec((None, wk, n_seq), lambda b, h, i: (b, h, 0)),
        pl.BlockSpec((1, HEAD), lambda b, h, i: (0, 0)),
    ]
    args = [p, p, kh, vt, q_norm.reshape(1, HEAD)]
    if has_ctx:
        ck, cvt = ctx
        n_ctx = ck.shape[1]
        in_specs += [
            pl.BlockSpec((3, rows, HEAD), lambda b, h, i: (0, i, 0)),
            pl.BlockSpec((None, n_ctx, wk), lambda b, h, i: (b, 0, h)),
            pl.BlockSpec((None, wk, n_ctx), lambda b, h, i: (b, h, 0)),
        ]
        args += [rope_tab, ck, cvt]
    return pl.pallas_call(
        functools.partial(_gqa_kernel, has_ctx=has_ctx, tq=tq, heads=heads),
        grid=(n_batch, C_KV_HEADS // heads, nq),
        in_specs=in_specs,
        out_specs=pl.BlockSpec((rows, wq), lambda b, h, i: (b * nq + i, h)),
        out_shape=jax.ShapeDtypeStruct((n_batch * n_seq, D_MODEL), BF16),
        compiler_params=_cparams(3),
        name="gqa_attention",
    )(*args)


def _forget_gate(z, lb):
    t = jnp.exp2(jnp.abs(z) * -LOG2_E)
    u = 1.0 + t
    r = 1.0 / u
    tr = t * r
    pos = z >= 0.0
    sig = jnp.where(pos, r, tr)
    nsig = jnp.where(pos, tr, r)
    log2_sig = jnp.minimum(z, 0.0) * LOG2_E - jnp.log2(u)
    one_m = 1.0 - lb
    log2_f = jnp.where(lb > 0.0, jnp.log2(lb + one_m * sig), log2_sig)
    return log2_f, one_m * nsig


def _cumsum_rows(tri, x):
    hi = x.astype(BF16)
    mid = (x - hi.astype(F32)).astype(BF16)
    r = jnp.dot(tri, jnp.concatenate([hi, mid], axis=1), preferred_element_type=F32)
    return r[:, :HEAD] + r[:, HEAD:]


def _pair_rows(ref, b, odd):
    off = b if odd else 0
    if b >= 8:
        parts = [ref[pl.ds(i * 2 * b + off, b), :] for i in range(SCAN_HALF // b)]
    else:
        parts = [ref[pl.ds(off + r, SCAN_HALF // b, stride=2 * b), :] for r in range(b)]
    return parts[0] if len(parts) == 1 else jnp.concatenate(parts, axis=0)


def _pair_bcast(ref, b, row):
    if b >= 8:
        parts = [jnp.broadcast_to(ref[pl.ds(i * 2 * b + row, 1), :], (b, HEAD)) for i in range(SCAN_HALF // b)]
    else:
        parts = [ref[pl.ds(row, SCAN_HALF // b, stride=2 * b), :]] * b
    return parts[0] if len(parts) == 1 else jnp.concatenate(parts, axis=0)


def _pair_store(ref, b, odd, val):
    off = b if odd else 0
    if b >= 8:
        for i in range(SCAN_HALF // b):
            ref[pl.ds(i * 2 * b + off, b), :] = val[i * b:(i + 1) * b]
    else:
        n = SCAN_HALF // b
        for r in range(b):
            ref[pl.ds(off + r, n, stride=2 * b), :] = val[r * n:(r + 1) * n]


def _hgrn_kernel(*refs, n_chunks, has_state):
    q_ref, zf_ref, zb_ref, v_ref, g_ref, lb_ref, on_ref = refs[:7]
    s0_ref = refs[7] if has_state else None
    pos = 8
    o_ref = refs[pos]
    pos += 1
    sout_ref = None
    if not has_state:
        sout_ref = refs[pos]
        pos += 1
    tri_ref, mask_ref = refs[pos:pos + 2]
    pos += 2
    buf_sets = [refs[pos + k * SCAN_BUFS:pos + (k + 1) * SCAN_BUFS] for k in range(SCAN_CHUNKS_PER_ITER)]
    pos += SCAN_CHUNKS_PER_ITER * SCAN_BUFS
    if has_state:
        oall, qb_all, klb_all, decb_all = refs[pos:pos + 4]

    c = SCAN_CHUNK
    row = lax.broadcasted_iota(jnp.int32, (c, c), 0)
    col = lax.broadcasted_iota(jnp.int32, (c, c), 1)
    tri_ref[0] = (col <= row).astype(BF16)
    tri_ref[1] = (col >= row).astype(BF16)
    prow = lax.broadcasted_iota(jnp.int32, (SCAN_HALF, SCAN_HALF), 0)
    pcol = lax.broadcasted_iota(jnp.int32, (SCAN_HALF, SCAN_HALF), 1)
    for j in range(SCAN_LEVELS):
        b = 1 << j
        if b >= 8:
            same = (prow // b) == (pcol // b)
        else:
            same = (prow % (SCAN_HALF // b)) == (pcol % (SCAN_HALF // b))
        mask_ref[j] = same.astype(F32).astype(BF16)

    lb_f = lb_ref[0:1, :]
    lb_b = lb_ref[1:2, :]

    def gates(ci, bufs):
        sq, sv, skf, scf, skb, scb = bufs[:SCAN_IN_BUFS]
        for r0 in range(0, c, SCAN_GATE_ROWS):
            src = pl.ds(pl.multiple_of(ci * c + r0, SCAN_GATE_ROWS), SCAN_GATE_ROWS)
            dst = slice(r0, r0 + SCAN_GATE_ROWS)
            sq[dst, :] = _silu(q_ref[src, :])
            sv[dst, :] = v_ref[src, :]
            scf[dst, :], skf[dst, :] = _forget_gate(zf_ref[src, :], lb_f)
            scb[dst, :], skb[dst, :] = _forget_gate(zb_ref[src, :], lb_b)
        scf[...] = _cumsum_rows(tri_ref[0], scf[...])
        scb[...] = _cumsum_rows(tri_ref[1], scb[...])

    def pair_scores(bufs, j, fwd):
        sq, sv, skf, scf, skb, scb = bufs[:SCAN_IN_BUFS]
        b = 1 << j
        if fwd:
            edge = _pair_bcast(scf, b, b - 1)
            qt = _pair_rows(sq, b, True) * jnp.exp2(_pair_rows(scf, b, True) - edge)
            kt = _pair_rows(skf, b, False) * jnp.exp2(edge - _pair_rows(scf, b, False))
        else:
            edge = _pair_bcast(scb, b, b)
            qt = _pair_rows(sq, b, False) * jnp.exp2(_pair_rows(scb, b, False) - edge)
            kt = _pair_rows(skb, b, True) * jnp.exp2(edge - _pair_rows(scb, b, True))
        z = lax.dot_general(qt.astype(BF16), kt.astype(BF16), _NT, preferred_element_type=F32).astype(BF16)
        return z if j == SCAN_LEVELS - 1 else z * mask_ref[j]

    def pair_values(bufs, j, fwd, z):
        sv, lvl_out = bufs[1], bufs[SCAN_IN_BUFS:]
        b = 1 << j
        _pair_store(lvl_out[j], b, fwd, jnp.dot(z, _pair_rows(sv, b, not fwd).astype(BF16),
                                                preferred_element_type=F32))

    def chunk_result(bufs):
        q, v, kf, cf, kb, cb = (buf[...] for buf in bufs[:SCAN_IN_BUFS])
        o = jnp.sum(q * (kf + kb), axis=-1, keepdims=True) * v
        for lvl in bufs[SCAN_IN_BUFS:]:
            o = o + lvl[...]
        tot_f = cf[c - 1:c, :]
        tot_b = cb[0:1, :]
        klf = kf * jnp.exp2(tot_f - cf)
        klb = kb * jnp.exp2(tot_b - cb)
        return o, q, v, cf, cb, tot_f, tot_b, klf, klb

    def intra(cis):
        for ci, bufs in zip(cis, buf_sets):
            gates(ci, bufs)
        items = [(bufs, j, fwd) for bufs in buf_sets for j in range(SCAN_LEVELS) for fwd in (True, False)]
        scores = {}
        for t in range(len(items) + SCAN_SCORE_LOOKAHEAD):
            if t < len(items):
                scores[t] = pair_scores(*items[t])
            if t >= SCAN_SCORE_LOOKAHEAD:
                u = t - SCAN_SCORE_LOOKAHEAD
                pair_values(*items[u], scores.pop(u))
        return [chunk_result(bufs) for bufs in buf_sets]

    def finish(r, o):
        o = _rms(o, on_ref[...]) * _silu(g_ref[r, :])
        o_ref[r, :] = o.astype(BF16)

    def rows_of(ci):
        return pl.ds(pl.multiple_of(ci * c, c), c)

    per_iter = SCAN_CHUNKS_PER_ITER
    n_iters = n_chunks // per_iter

    if not has_state:
        def seqs(it, carry):
            cis = [it * per_iter + k for k in range(per_iter)]
            for ci, res in zip(cis, intra(cis)):
                o, q, v, cf, cb, tot_f, tot_b, klf, klb = res
                v_b = v.astype(BF16)
                sout_ref[ci, 0] = jnp.dot(klf.T.astype(BF16), v_b, preferred_element_type=F32)
                sout_ref[ci, 1] = jnp.dot(klb.T.astype(BF16), v_b, preferred_element_type=F32)
                finish(rows_of(ci), o)
            return carry

        lax.fori_loop(0, n_iters, seqs, 0)
        return

    def row_to_col(x):
        return jnp.broadcast_to(x, (HEAD, HEAD)).T

    def fwd(it, s):
        cis = [it * per_iter + k for k in range(per_iter)]
        for ci, res in zip(cis, intra(cis)):
            o, q, v, cf, cb, tot_f, tot_b, klf, klb = res
            r = rows_of(ci)
            oall[r, :] = o + jnp.dot((q * jnp.exp2(cf)).astype(BF16), s.astype(BF16),
                                     preferred_element_type=F32)
            qb_all[r, :] = (q * jnp.exp2(cb)).astype(BF16)
            klb_all[r, :] = klb
            decb_all[pl.ds(pl.multiple_of(ci * 8, 8), 1), :] = jnp.exp2(tot_b)
            s = row_to_col(jnp.exp2(tot_f)) * s + jnp.dot(klf.T.astype(BF16), v.astype(BF16),
                                                          preferred_element_type=F32)
        return s

    def bwd(i, s):
        ci = n_chunks - 1 - i
        r = rows_of(ci)
        finish(r, oall[r, :] + jnp.dot(qb_all[r, :], s.astype(BF16), preferred_element_type=F32))
        dec = decb_all[pl.ds(pl.multiple_of(ci * 8, 8), 1), :]
        return row_to_col(dec) * s + jnp.dot(klb_all[r, :].T.astype(BF16), v_ref[r, :].astype(BF16),
                                             preferred_element_type=F32)

    lax.fori_loop(0, n_iters, fwd, s0_ref[0])
    lax.fori_loop(0, n_chunks, bwd, s0_ref[1], unroll=4)


def _hgrn_scan(p, lb, o_norm, n_batch, n_seq, state=None, final_states=None, slot=0, seqs_per_step=8):
    has_state = state is not None
    c = SCAN_CHUNK
    if has_state:
        rows, n_steps = n_seq, n_batch
    else:
        assert n_seq == c and n_batch % seqs_per_step == 0
        rows, n_steps = seqs_per_step * c, n_batch // seqs_per_step
    n_chunks = rows // c
    col = lambda s: (lambda b, h: (b, s * A_HEADS + h))
    in_specs = [pl.BlockSpec((rows, HEAD), col(s)) for s in range(5)]
    in_specs += [
        pl.BlockSpec((2, HEAD), lambda b, h: (0, h)),
        pl.BlockSpec((1, HEAD), lambda b, h: (0, 0)),
    ]
    args = [p] * 5 + [lb, o_norm.reshape(1, HEAD)]
    out_specs = [pl.BlockSpec((rows, HEAD), lambda b, h: (b, h))]
    out_shape = [jax.ShapeDtypeStruct((n_batch * n_seq, D_MODEL), BF16)]
    scratch = [pltpu.VMEM((2, c, c), BF16), pltpu.VMEM((SCAN_LEVELS, SCAN_HALF, SCAN_HALF), BF16)]
    assert n_chunks % SCAN_CHUNKS_PER_ITER == 0
    scratch += [pltpu.VMEM((c, HEAD), F32)] * (SCAN_CHUNKS_PER_ITER * SCAN_BUFS)
    if has_state:
        in_specs.append(pl.BlockSpec((None, 2, None, HEAD, HEAD), lambda b, h: (b, 0, h, 0, 0)))
        args.append(state)
        scratch += [pltpu.VMEM((rows, HEAD), F32), pltpu.VMEM((rows, HEAD), BF16),
                    pltpu.VMEM((rows, HEAD), F32), pltpu.VMEM((n_chunks * 8, HEAD), F32)]
    else:
        in_specs.append(pl.BlockSpec(memory_space=pl.ANY))
        args.append(final_states)
        out_specs.append(pl.BlockSpec((n_chunks, None, 2, None, HEAD, HEAD), lambda b, h: (b, slot, 0, h, 0, 0)))
        out_shape.append(jax.ShapeDtypeStruct(final_states.shape, F32))
    res = pl.pallas_call(
        functools.partial(_hgrn_kernel, n_chunks=n_chunks, has_state=has_state),
        grid=(n_steps, A_HEADS),
        in_specs=in_specs,
        out_specs=out_specs,
        out_shape=out_shape,
        input_output_aliases={} if has_state else {len(args) - 1: 1},
        scratch_shapes=scratch,
        compiler_params=_cparams(2),
        name="hgrn2_scan",
    )(*args)
    return (res[0], None) if has_state else (res[0], res[1])


def _diff_lambda_init(layer):
    return 0.8 - 0.6 * math.exp(-0.3 * layer)


def kernel(x_prompt, x_sample, state_a, cache_b_k, cache_b_v, cache_c_k, cache_c_v, c, c_ctx, norm_w, mod_w, mod_b, a_w_in, a_w_out, a_o_norm, a_lower_bound, b_w_in, b_w_out, b_q_norm, b_k_norm, b_lambda, b_subln, c_w_in, c_w_out, c_q_norm, c_k_norm):
    n_ctx_b, n_ctx_s = x_prompt.shape[:2]
    n_lat_b, n_lat_s = x_sample.shape[:2]
    past = cache_b_k.shape[2]
    in_tm, in_tn = 1024, 1024
    out_tm, out_tn = 512, D_MODEL

    lb_all = jnp.cumsum(jax.nn.softmax(a_lower_bound.astype(F32), axis=0), axis=0)
    lb_all = lb_all - lb_all[0:1]

    cond = jnp.zeros((COND_ROWS, D_MODEL), F32).at[0].set(c_ctx).at[1:1 + n_lat_b].set(c)
    mods = _mod_rows(cond, mod_w, mod_b)
    rope_tab = _rope_tables(n_lat_s)

    groups = [
        dict(x=x_prompt.reshape(-1, D_MODEL), nb=n_ctx_b, ns=n_ctx_s, latent=False,
             row=lambda tm: (lambda i: 0)),
        dict(x=x_sample.reshape(-1, D_MODEL), nb=n_lat_b, ns=n_lat_s, latent=True,
             row=lambda tm: (lambda i: 1 + i // (n_lat_s // tm))),
    ]
    w_in_all = [w.astype(BF16) for w in (a_w_in, b_w_in, c_w_in)]
    w_out_all = [w.astype(BF16) for w in (a_w_out, b_w_out, c_w_out)]
    n_a_layers = a_w_in.shape[0]
    new_a = jnp.zeros((n_ctx_b, n_a_layers, 2, A_HEADS, HEAD, HEAD), F32)
    new_bk, new_bv, new_ck, new_cv = [], [], [], []

    for layer in range(DEPTH):
        kind, j = layer % N_MIXERS, layer // N_MIXERS
        for grp in groups:
            nb, ns, latent = grp["nb"], grp["ns"], grp["latent"]
            p = _in_proj(grp["x"], mods[layer], norm_w[layer], w_in_all[kind], j, grp["row"](in_tm), in_tm, in_tn)
            if kind == 0:
                if latent:
                    o, _ = _hgrn_scan(p, lb_all[layer], a_o_norm[j], nb, ns, state=state_a[:, j])
                else:
                    o, new_a = _hgrn_scan(p, lb_all[layer], a_o_norm[j], nb, ns, final_states=new_a, slot=j)
            elif kind == 1:
                tab = rope_tab if latent else None
                res = _kv_prep(p, D_MODEL, D_MODEL, 2 * D_MODEL, D_MODEL, b_k_norm[j], tab, nb, ns)
                ctx = None
                if latent:
                    ctx = (cache_b_k[:, j].reshape(nb, past, D_MODEL).astype(BF16),
                           jnp.swapaxes(cache_b_v[:, j].reshape(nb, past, D_MODEL), 1, 2).astype(BF16))
                else:
                    new_bk.append(res[2].reshape(nb, ns, B_HEADS, 2, HEAD))
                    new_bv.append(res[3].reshape(nb, ns, B_HEADS, 2 * HEAD))
                o = _diff_attention(p, res[0], res[1], b_q_norm[j], b_lambda[j], b_subln[j],
                                    _diff_lambda_init(layer), nb, ns, 512 if latent else 256,
                                    q_blocks=4 if latent else 1, heads=1 if latent else B_HEADS, rope_tab=tab, ctx=ctx)
            else:
                kvw = C_KV_HEADS * HEAD
                tab = rope_tab if latent else None
                res = _kv_prep(p, D_MODEL, kvw, D_MODEL + kvw, kvw, c_k_norm[j], tab, nb, ns)
                ctx = None
                if latent:
                    ctx = (cache_c_k[:, j].reshape(nb, past, kvw).astype(BF16),
                           jnp.swapaxes(cache_c_v[:, j].reshape(nb, past, kvw), 1, 2).astype(BF16))
                else:
                    new_ck.append(res[2].reshape(nb, ns, C_KV_HEADS, HEAD))
                    new_cv.append(res[3].reshape(nb, ns, C_KV_HEADS, HEAD))
                o = _gqa_attention(p, res[0], res[1], c_q_norm[j], nb, ns, 256, q_blocks=4 if latent else 1, heads=1 if latent else 2,
                                   rope_tab=tab, ctx=ctx)
            grp["x"] = _out_proj(o, w_out_all[kind], j, grp["x"], mods[layer], grp["row"](out_tm), out_tm, out_tn)

    y_prompt = groups[0]["x"].reshape(x_prompt.shape)
    y_sample = groups[1]["x"].reshape(x_sample.shape)
    return (y_prompt, y_sample, new_a, jnp.stack(new_bk, axis=1), jnp.stack(new_bv, axis=1),
            jnp.stack(new_ck, axis=1), jnp.stack(new_cv, axis=1))
```

```python
import functools
import math

import jax
import jax.numpy as jnp
from jax import lax
from jax.experimental import pallas as pl
from jax.experimental.pallas import tpu as pltpu

F32 = jnp.float32
BF16 = jnp.bfloat16

D_MODEL = 2048
DEPTH = 4
GRID_W = 64
N_MIXERS = 3
EPS = 1e-6
ROPE_THETA = 10000.0
HEAD = 128
A_HEADS = D_MODEL // HEAD
B_HEADS = D_MODEL // (2 * HEAD)
C_HEADS = D_MODEL // HEAD
C_KV_HEADS = C_HEADS // 4
C_GROUP = C_HEADS // C_KV_HEADS
COND_ROWS = 8

V7X_VMEM_LIMIT_BYTES = 56 * 1024 * 1024
SCAN_CHUNK = 256
SCAN_HALF = SCAN_CHUNK // 2
SCAN_LEVELS = SCAN_CHUNK.bit_length() - 1
SCAN_IN_BUFS = 6
SCAN_BUFS = SCAN_IN_BUFS + SCAN_LEVELS
SCAN_CHUNKS_PER_ITER = 4
SCAN_GATE_ROWS = 64
SCAN_SCORE_LOOKAHEAD = 8
LOG2_E = math.log2(math.e)
ATTN_KEY_CHUNK = 1024
ATTN_SCALE_LOG2 = (HEAD ** -0.5) * LOG2_E
BF16_SUBLANES = 16


def _cparams(n_axes):
    return pltpu.CompilerParams(
        dimension_semantics=("arbitrary",) * n_axes,
        vmem_limit_bytes=V7X_VMEM_LIMIT_BYTES,
    )


def _sigmoid(x):
    return 1.0 / (1.0 + jnp.exp(-x))


def _silu(x):
    return x * _sigmoid(x)


def _rms(x, w):
    return x * lax.rsqrt(jnp.mean(x * x, axis=-1, keepdims=True) + EPS) * w


def _mod_kernel(cond_ref, w_ref, b_ref, o_ref):
    s = _silu(cond_ref[...]).astype(BF16)
    o_ref[...] = jnp.dot(s, w_ref[...].astype(BF16), preferred_element_type=F32) + b_ref[...]


def _mod_rows(cond, mod_w, mod_b):
    d3 = 3 * D_MODEL
    tn = 768
    out = pl.pallas_call(
        _mod_kernel,
        grid=(DEPTH, d3 // tn),
        in_specs=[
            pl.BlockSpec((COND_ROWS, D_MODEL), lambda l, j: (0, 0)),
            pl.BlockSpec((None, D_MODEL, tn), lambda l, j: (l, 0, j)),
            pl.BlockSpec((None, 1, tn), lambda l, j: (l, 0, j)),
        ],
        out_specs=pl.BlockSpec((None, COND_ROWS, tn), lambda l, j: (l, 0, j)),
        out_shape=jax.ShapeDtypeStruct((DEPTH, COND_ROWS, d3), F32),
        compiler_params=_cparams(2),
        name="adaln_rows",
    )(cond, mod_w, mod_b.reshape(DEPTH, 1, d3))
    return out.reshape(DEPTH, COND_ROWS, 3, D_MODEL)


def _inproj_kernel(x_ref, mod_ref, nw_ref, w_ref, o_ref, h_ref):
    @pl.when(pl.program_id(1) == 0)
    def _():
        gain = nw_ref[...] * (1.0 + mod_ref[1:2, :])
        shift = mod_ref[0:1, :]

        def rows(ci, carry):
            r = pl.ds(pl.multiple_of(ci * BF16_SUBLANES, BF16_SUBLANES), BF16_SUBLANES)
            x = x_ref[r, :]
            inv = lax.rsqrt(jnp.mean(x * x, axis=-1, keepdims=True) + EPS)
            h_ref[r, :] = (x * inv * gain + shift).astype(BF16)
            return carry

        lax.fori_loop(0, x_ref.shape[0] // BF16_SUBLANES, rows, 0, unroll=4)

    o_ref[...] = jnp.dot(h_ref[...], w_ref[...], preferred_element_type=F32).astype(o_ref.dtype)


def _in_proj(x, mod, norm_w, w, layer, row_of_tile, tm, tn):
    t, n = x.shape[0], w.shape[2]
    return pl.pallas_call(
        _inproj_kernel,
        grid=(t // tm, n // tn),
        in_specs=[
            pl.BlockSpec((tm, D_MODEL), lambda i, j: (i, 0)),
            pl.BlockSpec((None, 3, D_MODEL), lambda i, j: (row_of_tile(i), 0, 0)),
            pl.BlockSpec((1, D_MODEL), lambda i, j: (0, 0)),
            pl.BlockSpec((None, D_MODEL, tn), lambda i, j: (layer, 0, j)),
        ],
        out_specs=pl.BlockSpec((tm, tn), lambda i, j: (i, j)),
        out_shape=jax.ShapeDtypeStruct((t, n), F32),
        scratch_shapes=[pltpu.VMEM((tm, D_MODEL), BF16)],
        compiler_params=_cparams(2),
        name="in_proj",
    )(x, mod, norm_w.reshape(1, D_MODEL), w)


def _outproj_kernel(o_ref, w_ref, x_ref, mod_ref, y_ref):
    acc = jnp.dot(o_ref[...], w_ref[...], preferred_element_type=F32)
    y_ref[...] = x_ref[...] + mod_ref[2:3, :] * acc


def _out_proj(o, w, layer, x, mod, row_of_tile, tm, tn):
    t, kdim = o.shape
    return pl.pallas_call(
        _outproj_kernel,
        grid=(t // tm, D_MODEL // tn),
        in_specs=[
            pl.BlockSpec((tm, kdim), lambda i, j: (i, 0)),
            pl.BlockSpec((None, kdim, tn), lambda i, j: (layer, 0, j)),
            pl.BlockSpec((tm, tn), lambda i, j: (i, j)),
            pl.BlockSpec((None, 3, tn), lambda i, j: (row_of_tile(i), 0, j)),
        ],
        out_specs=pl.BlockSpec((tm, tn), lambda i, j: (i, j)),
        out_shape=jax.ShapeDtypeStruct((t, D_MODEL), F32),
        compiler_params=_cparams(2),
        name="out_proj",
    )(o, w, x, mod)


def _rope_tables(n):
    pos = jnp.arange(n)
    quarter = HEAD // 4
    inv_freq = ROPE_THETA ** (-jnp.arange(quarter, dtype=F32) / quarter)

    def axis_angles(p):
        ang = p.astype(F32)[:, None] * inv_freq[None, :]
        return jnp.concatenate([ang, ang], axis=-1)

    ang = jnp.concatenate([axis_angles(pos // GRID_W), axis_angles(pos % GRID_W)], axis=-1)
    cos, sin = jnp.cos(ang), jnp.sin(ang)
    first = (jnp.arange(HEAD) % (2 * quarter)) < quarter
    return jnp.stack([cos, jnp.where(first, -sin, 0.0), jnp.where(first, 0.0, sin)])


def _rope(x, tab_ref, rows=slice(None)):
    up = pltpu.roll(x, HEAD - HEAD // 4, 1)
    down = pltpu.roll(x, HEAD // 4, 1)
    return x * tab_ref[0, rows, :] + up * tab_ref[1, rows, :] + down * tab_ref[2, rows, :]


def _kvprep_kernel(*refs, n_kheads, rope, emit_norm):
    k_ref, v_ref, kn_ref = refs[:3]
    pos = 3
    tab_ref = None
    if rope:
        tab_ref = refs[pos]
        pos += 1
    kh_ref, vt_ref = refs[pos], refs[pos + 1]
    kn_out, v_out = (refs[pos + 2], refs[pos + 3]) if emit_norm else (None, None)
    for h in range(n_kheads):
        sl = slice(h * HEAD, (h + 1) * HEAD)
        kn = _rms(k_ref[:, sl], kn_ref[...])
        if emit_norm:
            kn_out[:, sl] = kn
        if rope:
            kn = _rope(kn, tab_ref)
        kh_ref[:, sl] = kn.astype(BF16)
    v = v_ref[...]
    if emit_norm:
        v_out[...] = v
    vt_ref[...] = v.T.astype(BF16)


def _kv_prep(p, k_col, k_w, v_col, v_w, k_norm, rope_tab, n_batch, n_seq):
    t = p.shape[0]
    rope = rope_tab is not None
    emit_norm = not rope
    tr = min(512, n_seq)
    per_seq = n_seq // tr
    in_specs = [
        pl.BlockSpec((tr, k_w), lambda i: (i, k_col // k_w)),
        pl.BlockSpec((tr, v_w), lambda i: (i, v_col // v_w)),
        pl.BlockSpec((1, HEAD), lambda i: (0, 0)),
    ]
    args = [p, p, k_norm.reshape(1, HEAD)]
    if rope:
        in_specs.append(pl.BlockSpec((3, tr, HEAD), lambda i: (0, i % per_seq, 0)))
        args.append(rope_tab)
    out_specs = [
        pl.BlockSpec((tr, k_w), lambda i: (i, 0)),
        pl.BlockSpec((None, v_w, tr), lambda i: (i // per_seq, 0, i % per_seq)),
    ]
    out_shape = [jax.ShapeDtypeStruct((t, k_w), BF16), jax.ShapeDtypeStruct((n_batch, v_w, n_seq), BF16)]
    if emit_norm:
        out_specs += [pl.BlockSpec((tr, k_w), lambda i: (i, 0)), pl.BlockSpec((tr, v_w), lambda i: (i, 0))]
        out_shape += [jax.ShapeDtypeStruct((t, k_w), F32), jax.ShapeDtypeStruct((t, v_w), F32)]
    return pl.pallas_call(
        functools.partial(_kvprep_kernel, n_kheads=k_w // HEAD, rope=rope, emit_norm=emit_norm),
        grid=(t // tr,),
        in_specs=in_specs,
        out_specs=out_specs,
        out_shape=out_shape,
        compiler_params=_cparams(1),
        name="kv_prep",
    )(*args)


_NT = (((1,), (1,)), ((), ()))


def _key_chunks(k_ref, vt_ref, head, k_w, v_w):
    n = k_ref.shape[0]
    step = min(n, ATTN_KEY_CHUNK)
    ones = jnp.ones((BF16_SUBLANES, step), BF16)
    return [(k_ref[c0:c0 + step, head * k_w:(head + 1) * k_w],
             jnp.concatenate([vt_ref[head * v_w:(head + 1) * v_w, c0:c0 + step], ones], axis=0))
            for c0 in range(0, n, step)]


def _attend_chunks(problems, finish):
    pairs = [(i, j) for i, (_, chunks) in enumerate(problems) for j in range(len(chunks))]
    s, m, o = {}, {}, {}

    def score(t):
        i, j = pairs[t]
        q, chunks = problems[i]
        s[t] = lax.dot_general(chunks[j][0], q, _NT, preferred_element_type=F32)
        m[t] = s[t].max(axis=0, keepdims=True)

    for t in range(min(2, len(pairs))):
        score(t)
    for t, (i, j) in enumerate(pairs):
        e = jnp.exp2(s.pop(t) - m[t]).astype(BF16)
        if t + 2 < len(pairs):
            score(t + 2)
        chunks = problems[i][1]
        n = len(chunks)
        n_v = chunks[j][1].shape[0] - BF16_SUBLANES
        o[t] = jnp.dot(chunks[j][1], e, preferred_element_type=F32)
        if j == n - 1:
            ts = range(t - n + 1, t + 1)
            if n == 1:
                acc = o[t]
            else:
                m_all = functools.reduce(jnp.maximum, [m[u] for u in ts])
                acc = None
                for u in ts:
                    part = o[u] * jnp.exp2(m[u] - m_all)
                    acc = part if acc is None else acc + part
            finish(i, acc[:n_v], acc[n_v:n_v + 1])


def _diff_attn_kernel(*refs, has_ctx, lam_init, tq, heads):
    q_ref, g_ref, kh_ref, vt_ref, qn_ref, lamv_ref, sub_ref = refs[:7]
    w2 = 2 * HEAD
    pos = 7
    if has_ctx:
        tab_ref, ck_ref, cvt_ref = refs[pos:pos + 3]
        pos += 3
    o_ref = refs[pos]

    lv = lamv_ref[...]
    lam = (jnp.exp(jnp.sum(lv[0:1] * lv[1:2], keepdims=True))
           - jnp.exp(jnp.sum(lv[2:3] * lv[3:4], keepdims=True)) + lam_init)

    zero = jnp.zeros((tq, HEAD), BF16)
    n_blk = q_ref.shape[0] // tq
    problems = []
    for h in range(heads):
        chunks = _key_chunks(kh_ref, vt_ref, h, w2, w2) + (_key_chunks(ck_ref, cvt_ref, h, w2, w2) if has_ctx else [])
        for i in range(n_blk):
            r = slice(i * tq, (i + 1) * tq)
            rows = []
            for comp in range(2):
                q = _rms(q_ref[r, h * w2 + comp * HEAD:h * w2 + (comp + 1) * HEAD], qn_ref[...])
                if has_ctx:
                    q = _rope(q, tab_ref, r)
                q = (q * ATTN_SCALE_LOG2).astype(BF16)
                rows.append(jnp.concatenate([q, zero] if comp == 0 else [zero, q], axis=1))
            problems.append((jnp.concatenate(rows, axis=0), chunks))

    def finish(idx, acc, l):
        h, i = divmod(idx, n_blk)
        r = slice(i * tq, (i + 1) * tq)
        cols = slice(h * w2, (h + 1) * w2)
        o_t = acc[:, :tq] * (1.0 / l[:, :tq]) - acc[:, tq:] * (lam / l[:, tq:])
        o = _rms(o_t.T, sub_ref[...]) * (1.0 - lam_init)
        o_ref[r, cols] = (o * _silu(g_ref[r, cols])).astype(BF16)

    _attend_chunks(problems, finish)


def _diff_attention(p, kh, vt, q_norm, lam_vecs, subln, lam_init, n_batch, n_seq, tq, q_blocks=1, heads=1,
                    rope_tab=None, ctx=None):
    has_ctx = ctx is not None
    w2 = 2 * HEAD
    wc = heads * w2
    rows = tq * q_blocks
    nq = n_seq // rows
    assert (3 * D_MODEL) % wc == 0 and B_HEADS % heads == 0
    g_col0 = 3 * D_MODEL // wc
    in_specs = [
        pl.BlockSpec((rows, wc), lambda b, h, i: (b * nq + i, h)),
        pl.BlockSpec((rows, wc), lambda b, h, i: (b * nq + i, g_col0 + h)),
        pl.BlockSpec((n_seq, wc), lambda b, h, i: (b, h)),
        pl.BlockSpec((None, wc, n_seq), lambda b, h, i: (b, h, 0)),
        pl.BlockSpec((1, HEAD), lambda b, h, i: (0, 0)),
        pl.BlockSpec((4, HEAD), lambda b, h, i: (0, 0)),
        pl.BlockSpec((1, w2), lambda b, h, i: (0, 0)),
    ]
    args = [p, p, kh, vt, q_norm.reshape(1, HEAD), lam_vecs, subln.reshape(1, w2)]
    if has_ctx:
        ck, cvt = ctx
        n_ctx = ck.shape[1]
        in_specs += [
            pl.BlockSpec((3, rows, HEAD), lambda b, h, i: (0, i, 0)),
            pl.BlockSpec((None, n_ctx, wc), lambda b, h, i: (b, 0, h)),
            pl.BlockSpec((None, wc, n_ctx), lambda b, h, i: (b, h, 0)),
        ]
        args += [rope_tab, ck, cvt]
    return pl.pallas_call(
        functools.partial(_diff_attn_kernel, has_ctx=has_ctx, lam_init=lam_init, tq=tq, heads=heads),
        grid=(n_batch, B_HEADS // heads, nq),
        in_specs=in_specs,
        out_specs=pl.BlockSpec((rows, wc), lambda b, h, i: (b * nq + i, h)),
        out_shape=jax.ShapeDtypeStruct((n_batch * n_seq, D_MODEL), BF16),
        compiler_params=_cparams(3),
        name="diff_attention",
    )(*args)


def _gqa_kernel(*refs, has_ctx, tq, heads):
    q_ref, g_ref, kh_ref, vt_ref, qn_ref = refs[:5]
    wq = C_GROUP * HEAD
    pos = 5
    if has_ctx:
        tab_ref, ck_ref, cvt_ref = refs[pos:pos + 3]
        pos += 3
    o_ref = refs[pos]

    n_blk = q_ref.shape[0] // tq
    problems = []
    for h in range(heads):
        chunks = (_key_chunks(kh_ref, vt_ref, h, HEAD, HEAD)
                  + (_key_chunks(ck_ref, cvt_ref, h, HEAD, HEAD) if has_ctx else []))
        for i in range(n_blk):
            r = slice(i * tq, (i + 1) * tq)
            qs = []
            for gi in range(C_GROUP):
                c0 = h * wq + gi * HEAD
                q = _rms(q_ref[r, c0:c0 + HEAD], qn_ref[...])
                if has_ctx:
                    q = _rope(q, tab_ref, r)
                qs.append((q * ATTN_SCALE_LOG2).astype(BF16))
            problems.append((jnp.concatenate(qs, axis=0), chunks))

    def finish(idx, acc, l):
        h, i = divmod(idx, n_blk)
        r = slice(i * tq, (i + 1) * tq)
        o = (acc * (1.0 / l)).T
        for gi in range(C_GROUP):
            sl = slice(h * wq + gi * HEAD, h * wq + (gi + 1) * HEAD)
            o_ref[r, sl] = (o[gi * tq:(gi + 1) * tq] * _silu(g_ref[r, sl])).astype(BF16)

    _attend_chunks(problems, finish)


def _gqa_attention(p, kh, vt, q_norm, n_batch, n_seq, tq, q_blocks=1, heads=1, rope_tab=None, ctx=None):
    has_ctx = ctx is not None
    wq = heads * C_GROUP * HEAD
    wk = heads * HEAD
    rows = tq * q_blocks
    nq = n_seq // rows
    g_off = D_MODEL + 2 * C_KV_HEADS * HEAD
    assert g_off % wq == 0 and C_KV_HEADS % heads == 0
    g_col0 = g_off // wq
    in_specs = [
        pl.BlockSpec((rows, wq), lambda b, h, i: (b * nq + i, h)),
        pl.BlockSpec((rows, wq), lambda b, h, i: (b * nq + i, g_col0 + h)),
        pl.BlockSpec((n_seq, wk), lambda b, h, i: (b, h)),
        pl.BlockSpec((None, wk, n_seq), lambda b, h, i: (b, h, 0)),
        pl.BlockSpec((1, HEAD), lambda b, h, i: (0, 0)),
    ]
    args = [p, p, kh, vt, q_norm.reshape(1, HEAD)]
    if has_ctx:
        ck, cvt = ctx
        n_ctx = ck.shape[1]
        in_specs += [
            pl.BlockSpec((3, rows, HEAD), lambda b, h, i: (0, i, 0)),
            pl.BlockSpec((None, n_ctx, wk), lambda b, h, i: (b, 0, h)),
            pl.BlockSpec((None, wk, n_ctx), lambda b, h, i: (b, h, 0)),
        ]
        args += [rope_tab, ck, cvt]
    return pl.pallas_call(
        functools.partial(_gqa_kernel, has_ctx=has_ctx, tq=tq, heads=heads),
        grid=(n_batch, C_KV_HEADS // heads, nq),
        in_specs=in_specs,
        out_specs=pl.BlockSpec((rows, wq), lambda b, h, i: (b * nq + i, h)),
        out_shape=jax.ShapeDtypeStruct((n_batch * n_seq, D_MODEL), BF16),
        compiler_params=_cparams(3),
        name="gqa_attention",
    )(*args)


def _forget_gate(z, lb):
    t = jnp.exp2(jnp.abs(z) * -LOG2_E)
    u = 1.0 + t
    r = 1.0 / u
    tr = t * r
    pos = z >= 0.0
    sig = jnp.where(pos, r, tr)
    nsig = jnp.where(pos, tr, r)
    log2_sig = jnp.minimum(z, 0.0) * LOG2_E - jnp.log2(u)
    one_m = 1.0 - lb
    log2_f = jnp.where(lb > 0.0, jnp.log2(lb + one_m * sig), log2_sig)
    return log2_f, one_m * nsig


def _cumsum_rows(tri, x):
    hi = x.astype(BF16)
    mid = (x - hi.astype(F32)).astype(BF16)
    r = jnp.dot(tri, jnp.concatenate([hi, mid], axis=1), preferred_element_type=F32)
    return r[:, :HEAD] + r[:, HEAD:]


def _pair_rows(ref, b, odd):
    off = b if odd else 0
    if b >= 8:
        parts = [ref[pl.ds(i * 2 * b + off, b), :] for i in range(SCAN_HALF // b)]
    else:
        parts = [ref[pl.ds(off + r, SCAN_HALF // b, stride=2 * b), :] for r in range(b)]
    return parts[0] if len(parts) == 1 else jnp.concatenate(parts, axis=0)


def _pair_bcast(ref, b, row):
    if b >= 8:
        parts = [jnp.broadcast_to(ref[pl.ds(i * 2 * b + row, 1), :], (b, HEAD)) for i in range(SCAN_HALF // b)]
    else:
        parts = [ref[pl.ds(row, SCAN_HALF // b, stride=2 * b), :]] * b
    return parts[0] if len(parts) == 1 else jnp.concatenate(parts, axis=0)


def _pair_store(ref, b, odd, val):
    off = b if odd else 0
    if b >= 8:
        for i in range(SCAN_HALF // b):
            ref[pl.ds(i * 2 * b + off, b), :] = val[i * b:(i + 1) * b]
    else:
        n = SCAN_HALF // b
        for r in range(b):
            ref[pl.ds(off + r, n, stride=2 * b), :] = val[r * n:(r + 1) * n]


def _hgrn_kernel(*refs, n_chunks, has_state):
    q_ref, zf_ref, zb_ref, v_ref, g_ref, lb_ref, on_ref = refs[:7]
    s0_ref = refs[7] if has_state else None
    pos = 8
    o_ref = refs[pos]
    pos += 1
    sout_ref = None
    if not has_state:
        sout_ref = refs[pos]
        pos += 1
    tri_ref, mask_ref = refs[pos:pos + 2]
    pos += 2
    buf_sets = [refs[pos + k * SCAN_BUFS:pos + (k + 1) * SCAN_BUFS] for k in range(SCAN_CHUNKS_PER_ITER)]
    pos += SCAN_CHUNKS_PER_ITER * SCAN_BUFS
    if has_state:
        oall, qb_all, klb_all, decb_all = refs[pos:pos + 4]

    c = SCAN_CHUNK

    @pl.when((pl.program_id(0) == 0) & (pl.program_id(1) == 0))
    def _():
        row = lax.broadcasted_iota(jnp.int32, (c, c), 0)
        col = lax.broadcasted_iota(jnp.int32, (c, c), 1)
        tri_ref[0] = (col <= row).astype(BF16)
        tri_ref[1] = (col >= row).astype(BF16)
        prow = lax.broadcasted_iota(jnp.int32, (SCAN_HALF, SCAN_HALF), 0)
        pcol = lax.broadcasted_iota(jnp.int32, (SCAN_HALF, SCAN_HALF), 1)
        for j in range(SCAN_LEVELS):
            b = 1 << j
            if b >= 8:
                same = (prow // b) == (pcol // b)
            else:
                same = (prow % (SCAN_HALF // b)) == (pcol % (SCAN_HALF // b))
            mask_ref[j] = same.astype(F32).astype(BF16)

    lb_f = lb_ref[0:1, :]
    lb_b = lb_ref[1:2, :]

    def gates(ci, bufs):
        sq, sv, skf, scf, skb, scb = bufs[:SCAN_IN_BUFS]
        for r0 in range(0, c, SCAN_GATE_ROWS):
            src = pl.ds(pl.multiple_of(ci * c + r0, SCAN_GATE_ROWS), SCAN_GATE_ROWS)
            dst = slice(r0, r0 + SCAN_GATE_ROWS)
            sq[dst, :] = _silu(q_ref[src, :])
            sv[dst, :] = v_ref[src, :]
            scf[dst, :], skf[dst, :] = _forget_gate(zf_ref[src, :], lb_f)
            scb[dst, :], skb[dst, :] = _forget_gate(zb_ref[src, :], lb_b)
        scf[...] = _cumsum_rows(tri_ref[0], scf[...])
        scb[...] = _cumsum_rows(tri_ref[1], scb[...])

    def pair_scores(bufs, j, fwd):
        sq, sv, skf, scf, skb, scb = bufs[:SCAN_IN_BUFS]
        b = 1 << j
        if fwd:
            edge = _pair_bcast(scf, b, b - 1)
            qt = _pair_rows(sq, b, True) * jnp.exp2(_pair_rows(scf, b, True) - edge)
            kt = _pair_rows(skf, b, False) * jnp.exp2(edge - _pair_rows(scf, b, False))
        else:
            edge = _pair_bcast(scb, b, b)
            qt = _pair_rows(sq, b, False) * jnp.exp2(_pair_rows(scb, b, False) - edge)
            kt = _pair_rows(skb, b, True) * jnp.exp2(edge - _pair_rows(scb, b, True))
        z = lax.dot_general(qt.astype(BF16), kt.astype(BF16), _NT, preferred_element_type=F32).astype(BF16)
        return z if j == SCAN_LEVELS - 1 else z * mask_ref[j]

    def pair_values(bufs, j, fwd, z):
        sv, lvl_out = bufs[1], bufs[SCAN_IN_BUFS:]
        b = 1 << j
        _pair_store(lvl_out[j], b, fwd, jnp.dot(z, _pair_rows(sv, b, not fwd).astype(BF16),
                                                preferred_element_type=F32))

    def chunk_result(bufs):
        q, v, kf, cf, kb, cb = (buf[...] for buf in bufs[:SCAN_IN_BUFS])
        o = jnp.sum(q * (kf + kb), axis=-1, keepdims=True) * v
        for lvl in bufs[SCAN_IN_BUFS:]:
            o = o + lvl[...]
        tot_f = cf[c - 1:c, :]
        tot_b = cb[0:1, :]
        klf = kf * jnp.exp2(tot_f - cf)
        klb = kb * jnp.exp2(tot_b - cb)
        return o, q, v, cf, cb, tot_f, tot_b, klf, klb

    def intra(cis):
        for ci, bufs in zip(cis, buf_sets):
            gates(ci, bufs)
        items = [(bufs, j, fwd) for bufs in buf_sets for j in range(SCAN_LEVELS) for fwd in (True, False)]
        scores = {}
        for t in range(len(items) + SCAN_SCORE_LOOKAHEAD):
            if t < len(items):
                scores[t] = pair_scores(*items[t])
            if t >= SCAN_SCORE_LOOKAHEAD:
                u = t - SCAN_SCORE_LOOKAHEAD
                pair_values(*items[u], scores.pop(u))
        return [chunk_result(bufs) for bufs in buf_sets]

    def finish(r, o):
        o = _rms(o, on_ref[...]) * _silu(g_ref[r, :])
        o_ref[r, :] = o.astype(BF16)

    def rows_of(ci):
        return pl.ds(pl.multiple_of(ci * c, c), c)

    per_iter = SCAN_CHUNKS_PER_ITER
    n_iters = n_chunks // per_iter

    if not has_state:
        def seqs(it, carry):
            cis = [it * per_iter + k for k in range(per_iter)]
            for ci, res in zip(cis, intra(cis)):
                o, q, v, cf, cb, tot_f, tot_b, klf, klb = res
                v_b = v.astype(BF16)
                sout_ref[ci, 0] = jnp.dot(klf.T.astype(BF16), v_b, preferred_element_type=F32)
                sout_ref[ci, 1] = jnp.dot(klb.T.astype(BF16), v_b, preferred_element_type=F32)
                finish(rows_of(ci), o)
            return carry

        lax.fori_loop(0, n_iters, seqs, 0)
        return

    def row_to_col(x):
        return jnp.broadcast_to(x, (HEAD, HEAD)).T

    def fwd(it, s):
        cis = [it * per_iter + k for k in range(per_iter)]
        for ci, res in zip(cis, intra(cis)):
            o, q, v, cf, cb, tot_f, tot_b, klf, klb = res
            r = rows_of(ci)
            oall[r, :] = o + jnp.dot((q * jnp.exp2(cf)).astype(BF16), s.astype(BF16),
                                     preferred_element_type=F32)
            qb_all[r, :] = (q * jnp.exp2(cb)).astype(BF16)
            klb_all[r, :] = klb
            decb_all[pl.ds(pl.multiple_of(ci * 8, 8), 1), :] = jnp.exp2(tot_b)
            s = row_to_col(jnp.exp2(tot_f)) * s + jnp.dot(klf.T.astype(BF16), v.astype(BF16),
                                                          preferred_element_type=F32)
        return s

    def bwd(i, s):
        ci = n_chunks - 1 - i
        r = rows_of(ci)
        finish(r, oall[r, :] + jnp.dot(qb_all[r, :], s.astype(BF16), preferred_element_type=F32))
        dec = decb_all[pl.ds(pl.multiple_of(ci * 8, 8), 1), :]
        return row_to_col(dec) * s + jnp.dot(klb_all[r, :].T.astype(BF16), v_ref[r, :].astype(BF16),
                                             preferred_element_type=F32)

    lax.fori_loop(0, n_iters, fwd, s0_ref[0])
    lax.fori_loop(0, n_chunks, bwd, s0_ref[1], unroll=4)


def _hgrn_scan(p, lb, o_norm, n_batch, n_seq, state=None, final_states=None, slot=0, seqs_per_step=8):
    has_state = state is not None
    c = SCAN_CHUNK
    if has_state:
        rows, n_steps = n_seq, n_batch
    else:
        assert n_seq == c and n_batch % seqs_per_step == 0
        rows, n_steps = seqs_per_step * c, n_batch // seqs_per_step
    n_chunks = rows // c
    col = lambda s: (lambda b, h: (b, s * A_HEADS + h))
    in_specs = [pl.BlockSpec((rows, HEAD), col(s)) for s in range(5)]
    in_specs += [
        pl.BlockSpec((2, HEAD), lambda b, h: (0, h)),
        pl.BlockSpec((1, HEAD), lambda b, h: (0, 0)),
    ]
    args = [p] * 5 + [lb, o_norm.reshape(1, HEAD)]
    out_specs = [pl.BlockSpec((rows, HEAD), lambda b, h: (b, h))]
    out_shape = [jax.ShapeDtypeStruct((n_batch * n_seq, D_MODEL), BF16)]
    scratch = [pltpu.VMEM((2, c, c), BF16), pltpu.VMEM((SCAN_LEVELS, SCAN_HALF, SCAN_HALF), BF16)]
    assert n_chunks % SCAN_CHUNKS_PER_ITER == 0
    scratch += [pltpu.VMEM((c, HEAD), F32)] * (SCAN_CHUNKS_PER_ITER * SCAN_BUFS)
    if has_state:
        in_specs.append(pl.BlockSpec((None, 2, None, HEAD, HEAD), lambda b, h: (b, 0, h, 0, 0)))
        args.append(state)
        scratch += [pltpu.VMEM((rows, HEAD), F32), pltpu.VMEM((rows, HEAD), BF16),
                    pltpu.VMEM((rows, HEAD), F32), pltpu.VMEM((n_chunks * 8, HEAD), F32)]
    else:
        in_specs.append(pl.BlockSpec(memory_space=pl.ANY))
        args.append(final_states)
        out_specs.append(pl.BlockSpec((n_chunks, None, 2, None, HEAD, HEAD), lambda b, h: (b, slot, 0, h, 0, 0)))
        out_shape.append(jax.ShapeDtypeStruct(final_states.shape, F32))
    res = pl.pallas_call(
        functools.partial(_hgrn_kernel, n_chunks=n_chunks, has_state=has_state),
        grid=(n_steps, A_HEADS),
        in_specs=in_specs,
        out_specs=out_specs,
        out_shape=out_shape,
        input_output_aliases={} if has_state else {len(args) - 1: 1},
        scratch_shapes=scratch,
        compiler_params=_cparams(2),
        name="hgrn2_scan",
    )(*args)
    return (res[0], None) if has_state else (res[0], res[1])


def _diff_lambda_init(layer):
    return 0.8 - 0.6 * math.exp(-0.3 * layer)


def kernel(x_prompt, x_sample, state_a, cache_b_k, cache_b_v, cache_c_k, cache_c_v, c, c_ctx, norm_w, mod_w, mod_b, a_w_in, a_w_out, a_o_norm, a_lower_bound, b_w_in, b_w_out, b_q_norm, b_k_norm, b_lambda, b_subln, c_w_in, c_w_out, c_q_norm, c_k_norm):
    n_ctx_b, n_ctx_s = x_prompt.shape[:2]
    n_lat_b, n_lat_s = x_sample.shape[:2]
    past = cache_b_k.shape[2]
    in_tm, in_tn = 1024, 1024
    out_tm, out_tn = 512, D_MODEL

    lb_all = jnp.cumsum(jax.nn.softmax(a_lower_bound.astype(F32), axis=0), axis=0)
    lb_all = lb_all - lb_all[0:1]

    cond = jnp.zeros((COND_ROWS, D_MODEL), F32).at[0].set(c_ctx).at[1:1 + n_lat_b].set(c)
    mods = _mod_rows(cond, mod_w, mod_b)
    rope_tab = _rope_tables(n_lat_s)

    groups = [
        dict(x=x_prompt.reshape(-1, D_MODEL), nb=n_ctx_b, ns=n_ctx_s, latent=False,
             row=lambda tm: (lambda i: 0)),
        dict(x=x_sample.reshape(-1, D_MODEL), nb=n_lat_b, ns=n_lat_s, latent=True,
             row=lambda tm: (lambda i: 1 + i // (n_lat_s // tm))),
    ]
    w_in_all = [w.astype(BF16) for w in (a_w_in, b_w_in, c_w_in)]
    w_out_all = [w.astype(BF16) for w in (a_w_out, b_w_out, c_w_out)]
    n_a_layers = a_w_in.shape[0]
    new_a = jnp.zeros((n_ctx_b, n_a_layers, 2, A_HEADS, HEAD, HEAD), F32)
    new_bk, new_bv, new_ck, new_cv = [], [], [], []

    for layer in range(DEPTH):
        kind, j = layer % N_MIXERS, layer // N_MIXERS
        for grp in groups:
            nb, ns, latent = grp["nb"], grp["ns"], grp["latent"]
            p = _in_proj(grp["x"], mods[layer], norm_w[layer], w_in_all[kind], j, grp["row"](in_tm), in_tm, in_tn)
            if kind == 0:
                if latent:
                    o, _ = _hgrn_scan(p, lb_all[layer], a_o_norm[j], nb, ns, state=state_a[:, j])
                else:
                    o, new_a = _hgrn_scan(p, lb_all[layer], a_o_norm[j], nb, ns, final_states=new_a, slot=j)
            elif kind == 1:
                tab = rope_tab if latent else None
                res = _kv_prep(p, D_MODEL, D_MODEL, 2 * D_MODEL, D_MODEL, b_k_norm[j], tab, nb, ns)
                ctx = None
                if latent:
                    ctx = (cache_b_k[:, j].reshape(nb, past, D_MODEL).astype(BF16),
                           jnp.swapaxes(cache_b_v[:, j].reshape(nb, past, D_MODEL), 1, 2).astype(BF16))
                else:
                    new_bk.append(res[2].reshape(nb, ns, B_HEADS, 2, HEAD))
                    new_bv.append(res[3].reshape(nb, ns, B_HEADS, 2 * HEAD))
                o = _diff_attention(p, res[0], res[1], b_q_norm[j], b_lambda[j], b_subln[j],
                                    _diff_lambda_init(layer), nb, ns, 512 if latent else 256,
                                    q_blocks=4 if latent else 1, heads=1 if latent else B_HEADS, rope_tab=tab, ctx=ctx)
            else:
                kvw = C_KV_HEADS * HEAD
                tab = rope_tab if latent else None
                res = _kv_prep(p, D_MODEL, kvw, D_MODEL + kvw, kvw, c_k_norm[j], tab, nb, ns)
                ctx = None
                if latent:
                    ctx = (cache_c_k[:, j].reshape(nb, past, kvw).astype(BF16),
                           jnp.swapaxes(cache_c_v[:, j].reshape(nb, past, kvw), 1, 2).astype(BF16))
                else:
                    new_ck.append(res[2].reshape(nb, ns, C_KV_HEADS, HEAD))
                    new_cv.append(res[3].reshape(nb, ns, C_KV_HEADS, HEAD))
                o = _gqa_attention(p, res[0], res[1], c_q_norm[j], nb, ns, 256, q_blocks=4 if latent else 1, heads=1 if latent else 2,
                                   rope_tab=tab, ctx=ctx)
            grp["x"] = _out_proj(o, w_out_all[kind], j, grp["x"], mods[layer], grp["row"](out_tm), out_tm, out_tn)

    y_prompt = groups[0]["x"].reshape(x_prompt.shape)
    y_sample = groups[1]["x"].reshape(x_sample.shape)
    return (y_prompt, y_sample, new_a, jnp.stack(new_bk, axis=1), jnp.stack(new_bv, axis=1),
            jnp.stack(new_ck, axis=1), jnp.stack(new_cv, axis=1))
```

```python
import functools
import math

import jax
import jax.numpy as jnp
from jax import lax
from jax.experimental import pallas as pl
from jax.experimental.pallas import tpu as pltpu

F32 = jnp.float32
BF16 = jnp.bfloat16

D_MODEL = 2048
DEPTH = 4
GRID_W = 64
N_MIXERS = 3
EPS = 1e-6
ROPE_THETA = 10000.0
HEAD = 128
A_HEADS = D_MODEL // HEAD
B_HEADS = D_MODEL // (2 * HEAD)
C_HEADS = D_MODEL // HEAD
C_KV_HEADS = C_HEADS // 4
C_GROUP = C_HEADS // C_KV_HEADS
F32_SUBLANES = 8
COND_ROWS = F32_SUBLANES

V7X_VMEM_LIMIT_BYTES = 56 * 1024 * 1024
SCAN_CHUNK = 256
SCAN_HALF = SCAN_CHUNK // 2
SCAN_LEVELS = SCAN_CHUNK.bit_length() - 1
SCAN_IN_BUFS = 6
SCAN_BUFS = SCAN_IN_BUFS + SCAN_LEVELS
SCAN_CHUNKS_PER_ITER = 4
SCAN_GATE_ROWS = 64
SCAN_SCORE_LOOKAHEAD = 8
LOG2_E = math.log2(math.e)
ATTN_KEY_CHUNK = 1024
ATTN_SCALE_LOG2 = (HEAD ** -0.5) * LOG2_E
BF16_SUBLANES = 16


def _cparams(n_axes):
    return pltpu.CompilerParams(
        dimension_semantics=("arbitrary",) * n_axes,
        vmem_limit_bytes=V7X_VMEM_LIMIT_BYTES,
    )


def _sigmoid(x):
    return 1.0 / (1.0 + jnp.exp(-x))


def _silu(x):
    return x * _sigmoid(x)


def _rms(x, w):
    return x * lax.rsqrt(jnp.mean(x * x, axis=-1, keepdims=True) + EPS) * w


def _mod_kernel(cond_ref, w_ref, b_ref, o_ref):
    s = _silu(cond_ref[...]).astype(BF16)
    o_ref[...] = jnp.dot(s, w_ref[...].astype(BF16), preferred_element_type=F32) + b_ref[...]


def _mod_rows(cond, mod_w, mod_b):
    d3 = 3 * D_MODEL
    tn = 768
    out = pl.pallas_call(
        _mod_kernel,
        grid=(DEPTH, d3 // tn),
        in_specs=[
            pl.BlockSpec((COND_ROWS, D_MODEL), lambda l, j: (0, 0)),
            pl.BlockSpec((None, D_MODEL, tn), lambda l, j: (l, 0, j)),
            pl.BlockSpec((None, 1, tn), lambda l, j: (l, 0, j)),
        ],
        out_specs=pl.BlockSpec((None, COND_ROWS, tn), lambda l, j: (l, 0, j)),
        out_shape=jax.ShapeDtypeStruct((DEPTH, COND_ROWS, d3), F32),
        compiler_params=_cparams(2),
        name="adaln_rows",
    )(cond, mod_w, mod_b.reshape(DEPTH, 1, d3))
    return out.reshape(DEPTH, COND_ROWS, 3, D_MODEL)


def _inproj_kernel(x_ref, mod_ref, nw_ref, w_ref, o_ref, h_ref):
    @pl.when(pl.program_id(1) == 0)
    def _():
        gain = nw_ref[...] * (1.0 + mod_ref[1:2, :])
        shift = mod_ref[0:1, :]

        def rows(ci, carry):
            r = pl.ds(pl.multiple_of(ci * BF16_SUBLANES, BF16_SUBLANES), BF16_SUBLANES)
            x = x_ref[r, :]
            inv = lax.rsqrt(jnp.mean(x * x, axis=-1, keepdims=True) + EPS)
            h_ref[r, :] = (x * inv * gain + shift).astype(BF16)
            return carry

        lax.fori_loop(0, x_ref.shape[0] // BF16_SUBLANES, rows, 0, unroll=4)

    o_ref[...] = jnp.dot(h_ref[...], w_ref[...], preferred_element_type=F32).astype(o_ref.dtype)


def _in_proj(x, mod, norm_w, w, layer, row_of_tile, tm, tn):
    t, n = x.shape[0], w.shape[2]
    return pl.pallas_call(
        _inproj_kernel,
        grid=(t // tm, n // tn),
        in_specs=[
            pl.BlockSpec((tm, D_MODEL), lambda i, j: (i, 0)),
            pl.BlockSpec((None, 3, D_MODEL), lambda i, j: (row_of_tile(i), 0, 0)),
            pl.BlockSpec((1, D_MODEL), lambda i, j: (0, 0)),
            pl.BlockSpec((None, D_MODEL, tn), lambda i, j: (layer, 0, j)),
        ],
        out_specs=pl.BlockSpec((tm, tn), lambda i, j: (i, j)),
        out_shape=jax.ShapeDtypeStruct((t, n), F32),
        scratch_shapes=[pltpu.VMEM((tm, D_MODEL), BF16)],
        compiler_params=_cparams(2),
        name="in_proj",
    )(x, mod, norm_w.reshape(1, D_MODEL), w)


def _outproj_kernel(o_ref, w_ref, x_ref, mod_ref, y_ref):
    acc = jnp.dot(o_ref[...], w_ref[...], preferred_element_type=F32)
    y_ref[...] = x_ref[...] + mod_ref[2:3, :] * acc


def _out_proj(o, w, layer, x, mod, row_of_tile, tm, tn):
    t, kdim = o.shape
    return pl.pallas_call(
        _outproj_kernel,
        grid=(t // tm, D_MODEL // tn),
        in_specs=[
            pl.BlockSpec((tm, kdim), lambda i, j: (i, 0)),
            pl.BlockSpec((None, kdim, tn), lambda i, j: (layer, 0, j)),
            pl.BlockSpec((tm, tn), lambda i, j: (i, j)),
            pl.BlockSpec((None, 3, tn), lambda i, j: (row_of_tile(i), 0, j)),
        ],
        out_specs=pl.BlockSpec((tm, tn), lambda i, j: (i, j)),
        out_shape=jax.ShapeDtypeStruct((t, D_MODEL), F32),
        compiler_params=_cparams(2),
        name="out_proj",
    )(o, w, x, mod)


def _rope_tables(n):
    pos = jnp.arange(n)
    quarter = HEAD // 4
    inv_freq = ROPE_THETA ** (-jnp.arange(quarter, dtype=F32) / quarter)

    def axis_angles(p):
        ang = p.astype(F32)[:, None] * inv_freq[None, :]
        return jnp.concatenate([ang, ang], axis=-1)

    ang = jnp.concatenate([axis_angles(pos // GRID_W), axis_angles(pos % GRID_W)], axis=-1)
    cos, sin = jnp.cos(ang), jnp.sin(ang)
    first = (jnp.arange(HEAD) % (2 * quarter)) < quarter
    return jnp.stack([cos, jnp.where(first, -sin, 0.0), jnp.where(first, 0.0, sin)])


def _rope(x, tab_ref, rows=slice(None)):
    up = pltpu.roll(x, HEAD - HEAD // 4, 1)
    down = pltpu.roll(x, HEAD // 4, 1)
    return x * tab_ref[0, rows, :] + up * tab_ref[1, rows, :] + down * tab_ref[2, rows, :]


def _kvprep_kernel(*refs, n_kheads, rope, emit_norm):
    k_ref, v_ref, kn_ref = refs[:3]
    pos = 3
    tab_ref = None
    if rope:
        tab_ref = refs[pos]
        pos += 1
    kh_ref, vt_ref = refs[pos], refs[pos + 1]
    kn_out, v_out = (refs[pos + 2], refs[pos + 3]) if emit_norm else (None, None)
    for h in range(n_kheads):
        sl = slice(h * HEAD, (h + 1) * HEAD)
        kn = _rms(k_ref[:, sl], kn_ref[...])
        if emit_norm:
            kn_out[:, sl] = kn
        if rope:
            kn = _rope(kn, tab_ref)
        kh_ref[:, sl] = kn.astype(BF16)
    v = v_ref[...]
    if emit_norm:
        v_out[...] = v
    vt_ref[...] = v.T.astype(BF16)


def _kv_prep(p, k_col, k_w, v_col, v_w, k_norm, rope_tab, n_batch, n_seq):
    t = p.shape[0]
    rope = rope_tab is not None
    emit_norm = not rope
    tr = min(512, n_seq)
    per_seq = n_seq // tr
    in_specs = [
        pl.BlockSpec((tr, k_w), lambda i: (i, k_col // k_w)),
        pl.BlockSpec((tr, v_w), lambda i: (i, v_col // v_w)),
        pl.BlockSpec((1, HEAD), lambda i: (0, 0)),
    ]
    args = [p, p, k_norm.reshape(1, HEAD)]
    if rope:
        in_specs.append(pl.BlockSpec((3, tr, HEAD), lambda i: (0, i % per_seq, 0)))
        args.append(rope_tab)
    out_specs = [
        pl.BlockSpec((tr, k_w), lambda i: (i, 0)),
        pl.BlockSpec((None, v_w, tr), lambda i: (i // per_seq, 0, i % per_seq)),
    ]
    out_shape = [jax.ShapeDtypeStruct((t, k_w), BF16), jax.ShapeDtypeStruct((n_batch, v_w, n_seq), BF16)]
    if emit_norm:
        out_specs += [pl.BlockSpec((tr, k_w), lambda i: (i, 0)), pl.BlockSpec((tr, v_w), lambda i: (i, 0))]
        out_shape += [jax.ShapeDtypeStruct((t, k_w), F32), jax.ShapeDtypeStruct((t, v_w), F32)]
    return pl.pallas_call(
        functools.partial(_kvprep_kernel, n_kheads=k_w // HEAD, rope=rope, emit_norm=emit_norm),
        grid=(t // tr,),
        in_specs=in_specs,
        out_specs=out_specs,
        out_shape=out_shape,
        compiler_params=_cparams(1),
        name="kv_prep",
    )(*args)


_NT = (((1,), (1,)), ((), ()))


def _key_chunks(k_ref, vt_ref, head, k_w, v_w):
    n = k_ref.shape[0]
    step = min(n, ATTN_KEY_CHUNK)
    ones = jnp.ones((BF16_SUBLANES, step), BF16)
    return [(k_ref[c0:c0 + step, head * k_w:(head + 1) * k_w],
             jnp.concatenate([vt_ref[head * v_w:(head + 1) * v_w, c0:c0 + step], ones], axis=0))
            for c0 in range(0, n, step)]


def _attend_chunks(problems, finish):
    pairs = [(i, j) for i, (_, chunks) in enumerate(problems) for j in range(len(chunks))]
    s, m, o = {}, {}, {}

    def score(t):
        i, j = pairs[t]
        q, chunks = problems[i]
        s[t] = lax.dot_general(chunks[j][0], q, _NT, preferred_element_type=F32)
        m[t] = s[t].max(axis=0, keepdims=True)

    for t in range(min(2, len(pairs))):
        score(t)
    for t, (i, j) in enumerate(pairs):
        e = jnp.exp2(s.pop(t) - m[t]).astype(BF16)
        if t + 2 < len(pairs):
            score(t + 2)
        chunks = problems[i][1]
        n = len(chunks)
        n_v = chunks[j][1].shape[0] - BF16_SUBLANES
        o[t] = jnp.dot(chunks[j][1], e, preferred_element_type=F32)
        if j == n - 1:
            ts = range(t - n + 1, t + 1)
            if n == 1:
                acc = o[t]
            else:
                m_all = functools.reduce(jnp.maximum, [m[u] for u in ts])
                acc = None
                for u in ts:
                    part = o[u] * jnp.exp2(m[u] - m_all)
                    acc = part if acc is None else acc + part
            finish(i, acc[:n_v], acc[n_v:n_v + 1])


def _diff_attn_kernel(*refs, has_ctx, lam_init, tq, heads):
    q_ref, g_ref, kh_ref, vt_ref, qn_ref, lamv_ref, sub_ref = refs[:7]
    w2 = 2 * HEAD
    pos = 7
    if has_ctx:
        tab_ref, ck_ref, cvt_ref = refs[pos:pos + 3]
        pos += 3
    o_ref = refs[pos]

    lv = lamv_ref[...]
    lam = (jnp.exp(jnp.sum(lv[0:1] * lv[1:2], keepdims=True))
           - jnp.exp(jnp.sum(lv[2:3] * lv[3:4], keepdims=True)) + lam_init)

    zero = jnp.zeros((tq, HEAD), BF16)
    n_blk = q_ref.shape[0] // tq
    problems = []
    for h in range(heads):
        chunks = _key_chunks(kh_ref, vt_ref, h, w2, w2) + (_key_chunks(ck_ref, cvt_ref, h, w2, w2) if has_ctx else [])
        for i in range(n_blk):
            r = slice(i * tq, (i + 1) * tq)
            rows = []
            for comp in range(2):
                q = _rms(q_ref[r, h * w2 + comp * HEAD:h * w2 + (comp + 1) * HEAD], qn_ref[...])
                if has_ctx:
                    q = _rope(q, tab_ref, r)
                q = (q * ATTN_SCALE_LOG2).astype(BF16)
                rows.append(jnp.concatenate([q, zero] if comp == 0 else [zero, q], axis=1))
            problems.append((jnp.concatenate(rows, axis=0), chunks))

    def finish(idx, acc, l):
        h, i = divmod(idx, n_blk)
        r = slice(i * tq, (i + 1) * tq)
        cols = slice(h * w2, (h + 1) * w2)
        o_t = acc[:, :tq] * (1.0 / l[:, :tq]) - acc[:, tq:] * (lam / l[:, tq:])
        o = _rms(o_t.T, sub_ref[...]) * (1.0 - lam_init)
        o_ref[r, cols] = (o * _silu(g_ref[r, cols])).astype(BF16)

    _attend_chunks(problems, finish)


def _diff_attention(p, kh, vt, q_norm, lam_vecs, subln, lam_init, n_batch, n_seq, tq, q_blocks=1, heads=1,
                    rope_tab=None, ctx=None):
    has_ctx = ctx is not None
    w2 = 2 * HEAD
    wc = heads * w2
    rows = tq * q_blocks
    nq = n_seq // rows
    assert (3 * D_MODEL) % wc == 0 and B_HEADS % heads == 0
    g_col0 = 3 * D_MODEL // wc
    in_specs = [
        pl.BlockSpec((rows, wc), lambda b, h, i: (b * nq + i, h)),
        pl.BlockSpec((rows, wc), lambda b, h, i: (b * nq + i, g_col0 + h)),
        pl.BlockSpec((n_seq, wc), lambda b, h, i: (b, h)),
        pl.BlockSpec((None, wc, n_seq), lambda b, h, i: (b, h, 0)),
        pl.BlockSpec((1, HEAD), lambda b, h, i: (0, 0)),
        pl.BlockSpec((4, HEAD), lambda b, h, i: (0, 0)),
        pl.BlockSpec((1, w2), lambda b, h, i: (0, 0)),
    ]
    args = [p, p, kh, vt, q_norm.reshape(1, HEAD), lam_vecs, subln.reshape(1, w2)]
    if has_ctx:
        ck, cvt = ctx
        n_ctx = ck.shape[1]
        in_specs += [
            pl.BlockSpec((3, rows, HEAD), lambda b, h, i: (0, i, 0)),
            pl.BlockSpec((None, n_ctx, wc), lambda b, h, i: (b, 0, h)),
            pl.BlockSpec((None, wc, n_ctx), lambda b, h, i: (b, h, 0)),
        ]
        args += [rope_tab, ck, cvt]
    return pl.pallas_call(
        functools.partial(_diff_attn_kernel, has_ctx=has_ctx, lam_init=lam_init, tq=tq, heads=heads),
        grid=(n_batch, B_HEADS // heads, nq),
        in_specs=in_specs,
        out_specs=pl.BlockSpec((rows, wc), lambda b, h, i: (b * nq + i, h)),
        out_shape=jax.ShapeDtypeStruct((n_batch * n_seq, D_MODEL), BF16),
        compiler_params=_cparams(3),
        name="diff_attention",
    )(*args)


def _gqa_kernel(*refs, has_ctx, tq, heads):
    q_ref, g_ref, kh_ref, vt_ref, qn_ref = refs[:5]
    wq = C_GROUP * HEAD
    pos = 5
    if has_ctx:
        tab_ref, ck_ref, cvt_ref = refs[pos:pos + 3]
        pos += 3
    o_ref = refs[pos]

    n_blk = q_ref.shape[0] // tq
    problems = []
    for h in range(heads):
        chunks = (_key_chunks(kh_ref, vt_ref, h, HEAD, HEAD)
                  + (_key_chunks(ck_ref, cvt_ref, h, HEAD, HEAD) if has_ctx else []))
        for i in range(n_blk):
            r = slice(i * tq, (i + 1) * tq)
            qs = []
            for gi in range(C_GROUP):
                c0 = h * wq + gi * HEAD
                q = _rms(q_ref[r, c0:c0 + HEAD], qn_ref[...])
                if has_ctx:
                    q = _rope(q, tab_ref, r)
                qs.append((q * ATTN_SCALE_LOG2).astype(BF16))
            problems.append((jnp.concatenate(qs, axis=0), chunks))

    def finish(idx, acc, l):
        h, i = divmod(idx, n_blk)
        r = slice(i * tq, (i + 1) * tq)
        o = (acc * (1.0 / l)).T
        for gi in range(C_GROUP):
            sl = slice(h * wq + gi * HEAD, h * wq + (gi + 1) * HEAD)
            o_ref[r, sl] = (o[gi * tq:(gi + 1) * tq] * _silu(g_ref[r, sl])).astype(BF16)

    _attend_chunks(problems, finish)


def _gqa_attention(p, kh, vt, q_norm, n_batch, n_seq, tq, q_blocks=1, heads=1, rope_tab=None, ctx=None):
    has_ctx = ctx is not None
    wq = heads * C_GROUP * HEAD
    wk = heads * HEAD
    rows = tq * q_blocks
    nq = n_seq // rows
    g_off = D_MODEL + 2 * C_KV_HEADS * HEAD
    assert g_off % wq == 0 and C_KV_HEADS % heads == 0
    g_col0 = g_off // wq
    in_specs = [
        pl.BlockSpec((rows, wq), lambda b, h, i: (b * nq + i, h)),
        pl.BlockSpec((rows, wq), lambda b, h, i: (b * nq + i, g_col0 + h)),
        pl.BlockSpec((n_seq, wk), lambda b, h, i: (b, h)),
        pl.BlockSpec((None, wk, n_seq), lambda b, h, i: (b, h, 0)),
        pl.BlockSpec((1, HEAD), lambda b, h, i: (0, 0)),
    ]
    args = [p, p, kh, vt, q_norm.reshape(1, HEAD)]
    if has_ctx:
        ck, cvt = ctx
        n_ctx = ck.shape[1]
        in_specs += [
            pl.BlockSpec((3, rows, HEAD), lambda b, h, i: (0, i, 0)),
            pl.BlockSpec((None, n_ctx, wk), lambda b, h, i: (b, 0, h)),
            pl.BlockSpec((None, wk, n_ctx), lambda b, h, i: (b, h, 0)),
        ]
        args += [rope_tab, ck, cvt]
    return pl.pallas_call(
        functools.partial(_gqa_kernel, has_ctx=has_ctx, tq=tq, heads=heads),
        grid=(n_batch, C_KV_HEADS // heads, nq),
        in_specs=in_specs,
        out_specs=pl.BlockSpec((rows, wq), lambda b, h, i: (b * nq + i, h)),
        out_shape=jax.ShapeDtypeStruct((n_batch * n_seq, D_MODEL), BF16),
        compiler_params=_cparams(3),
        name="gqa_attention",
    )(*args)


def _forget_gate(z, lb):
    t = jnp.exp2(jnp.abs(z) * -LOG2_E)
    u = 1.0 + t
    r = 1.0 / u
    tr = t * r
    pos = z >= 0.0
    sig = jnp.where(pos, r, tr)
    nsig = jnp.where(pos, tr, r)
    log2_sig = jnp.minimum(z, 0.0) * LOG2_E - jnp.log2(u)
    one_m = 1.0 - lb
    log2_f = jnp.where(lb > 0.0, jnp.log2(lb + one_m * sig), log2_sig)
    return log2_f, one_m * nsig


def _cumsum_rows(tri, x):
    hi = x.astype(BF16)
    mid = (x - hi.astype(F32)).astype(BF16)
    r = jnp.dot(tri, jnp.concatenate([hi, mid], axis=1), preferred_element_type=F32)
    return r[:, :HEAD] + r[:, HEAD:]


def _pair_rows(ref, b, odd):
    off = b if odd else 0
    if b >= 8:
        parts = [ref[pl.ds(i * 2 * b + off, b), :] for i in range(SCAN_HALF // b)]
    else:
        parts = [ref[pl.ds(off + r, SCAN_HALF // b, stride=2 * b), :] for r in range(b)]
    return parts[0] if len(parts) == 1 else jnp.concatenate(parts, axis=0)


def _pair_bcast(ref, b, row):
    if b >= 8:
        parts = [jnp.broadcast_to(ref[pl.ds(i * 2 * b + row, 1), :], (b, HEAD)) for i in range(SCAN_HALF // b)]
    else:
        parts = [ref[pl.ds(row, SCAN_HALF // b, stride=2 * b), :]] * b
    return parts[0] if len(parts) == 1 else jnp.concatenate(parts, axis=0)


def _pair_store(ref, b, odd, val):
    off = b if odd else 0
    if b >= 8:
        for i in range(SCAN_HALF // b):
            ref[pl.ds(i * 2 * b + off, b), :] = val[i * b:(i + 1) * b]
    else:
        n = SCAN_HALF // b
        for r in range(b):
            ref[pl.ds(off + r, n, stride=2 * b), :] = val[r * n:(r + 1) * n]


def _hgrn_kernel(*refs, n_chunks, has_state):
    q_ref, zf_ref, zb_ref, v_ref, g_ref, lb_ref, on_ref = refs[:7]
    s0_ref = refs[7] if has_state else None
    pos = 8
    o_ref = refs[pos]
    pos += 1
    sout_ref = None
    if not has_state:
        sout_ref = refs[pos]
        pos += 1
    tri_ref, mask_ref = refs[pos:pos + 2]
    pos += 2
    buf_sets = [refs[pos + k * SCAN_BUFS:pos + (k + 1) * SCAN_BUFS] for k in range(SCAN_CHUNKS_PER_ITER)]
    pos += SCAN_CHUNKS_PER_ITER * SCAN_BUFS
    if has_state:
        oall, qb_all, klb_all, decb_all = refs[pos:pos + 4]

    c = SCAN_CHUNK

    @pl.when((pl.program_id(0) == 0) & (pl.program_id(1) == 0))
    def _():
        row = lax.broadcasted_iota(jnp.int32, (c, c), 0)
        col = lax.broadcasted_iota(jnp.int32, (c, c), 1)
        tri_ref[0] = (col <= row).astype(BF16)
        tri_ref[1] = (col >= row).astype(BF16)
        prow = lax.broadcasted_iota(jnp.int32, (SCAN_HALF, SCAN_HALF), 0)
        pcol = lax.broadcasted_iota(jnp.int32, (SCAN_HALF, SCAN_HALF), 1)
        for j in range(SCAN_LEVELS):
            b = 1 << j
            if b >= 8:
                same = (prow // b) == (pcol // b)
            else:
                same = (prow % (SCAN_HALF // b)) == (pcol % (SCAN_HALF // b))
            mask_ref[j] = same.astype(F32).astype(BF16)

    lb_f = lb_ref[0:1, :]
    lb_b = lb_ref[1:2, :]

    def gates(ci, bufs):
        sq, sv, skf, scf, skb, scb = bufs[:SCAN_IN_BUFS]
        for r0 in range(0, c, SCAN_GATE_ROWS):
            src = pl.ds(pl.multiple_of(ci * c + r0, SCAN_GATE_ROWS), SCAN_GATE_ROWS)
            dst = slice(r0, r0 + SCAN_GATE_ROWS)
            sq[dst, :] = _silu(q_ref[src, :])
            sv[dst, :] = v_ref[src, :]
            scf[dst, :], skf[dst, :] = _forget_gate(zf_ref[src, :], lb_f)
            scb[dst, :], skb[dst, :] = _forget_gate(zb_ref[src, :], lb_b)
        scf[...] = _cumsum_rows(tri_ref[0], scf[...])
        scb[...] = _cumsum_rows(tri_ref[1], scb[...])

    def pair_scores(bufs, j, fwd):
        sq, sv, skf, scf, skb, scb = bufs[:SCAN_IN_BUFS]
        b = 1 << j
        if fwd:
            edge = _pair_bcast(scf, b, b - 1)
            qt = _pair_rows(sq, b, True) * jnp.exp2(_pair_rows(scf, b, True) - edge)
            kt = _pair_rows(skf, b, False) * jnp.exp2(edge - _pair_rows(scf, b, False))
        else:
            edge = _pair_bcast(scb, b, b)
            qt = _pair_rows(sq, b, False) * jnp.exp2(_pair_rows(scb, b, False) - edge)
            kt = _pair_rows(skb, b, True) * jnp.exp2(edge - _pair_rows(scb, b, True))
        z = lax.dot_general(qt.astype(BF16), kt.astype(BF16), _NT, preferred_element_type=F32).astype(BF16)
        return z if j == SCAN_LEVELS - 1 else z * mask_ref[j]

    def pair_values(bufs, j, fwd, z):
        sv, lvl_out = bufs[1], bufs[SCAN_IN_BUFS:]
        b = 1 << j
        _pair_store(lvl_out[j], b, fwd, jnp.dot(z, _pair_rows(sv, b, not fwd).astype(BF16),
                                                preferred_element_type=F32))

    def chunk_result(bufs):
        q, v, kf, cf, kb, cb = (buf[...] for buf in bufs[:SCAN_IN_BUFS])
        o = jnp.sum(q * (kf + kb), axis=-1, keepdims=True) * v
        for lvl in bufs[SCAN_IN_BUFS:]:
            o = o + lvl[...]
        tot_f = cf[c - 1:c, :]
        tot_b = cb[0:1, :]
        klf = kf * jnp.exp2(tot_f - cf)
        klb = kb * jnp.exp2(tot_b - cb)
        return o, q, v, cf, cb, tot_f, tot_b, klf, klb

    def intra(cis):
        for ci, bufs in zip(cis, buf_sets):
            gates(ci, bufs)
        items = [(bufs, j, fwd) for bufs in buf_sets for j in range(SCAN_LEVELS) for fwd in (True, False)]
        scores = {}
        for t in range(len(items) + SCAN_SCORE_LOOKAHEAD):
            if t < len(items):
                scores[t] = pair_scores(*items[t])
            if t >= SCAN_SCORE_LOOKAHEAD:
                u = t - SCAN_SCORE_LOOKAHEAD
                pair_values(*items[u], scores.pop(u))
        return [chunk_result(bufs) for bufs in buf_sets]

    def finish(r, o):
        o = _rms(o, on_ref[...]) * _silu(g_ref[r, :])
        o_ref[r, :] = o.astype(BF16)

    def rows_of(ci):
        return pl.ds(pl.multiple_of(ci * c, c), c)

    per_iter = SCAN_CHUNKS_PER_ITER
    n_iters = n_chunks // per_iter

    if not has_state:
        def seqs(it, carry):
            cis = [it * per_iter + k for k in range(per_iter)]
            for ci, res in zip(cis, intra(cis)):
                o, q, v, cf, cb, tot_f, tot_b, klf, klb = res
                v_b = v.astype(BF16)
                sout_ref[ci, 0] = jnp.dot(klf.T.astype(BF16), v_b, preferred_element_type=F32)
                sout_ref[ci, 1] = jnp.dot(klb.T.astype(BF16), v_b, preferred_element_type=F32)
                finish(rows_of(ci), o)
            return carry

        lax.fori_loop(0, n_iters, seqs, 0)
        return

    def row_to_col(x):
        return jnp.broadcast_to(x, (HEAD, HEAD)).T

    def fwd(it, s):
        cis = [it * per_iter + k for k in range(per_iter)]
        for ci, res in zip(cis, intra(cis)):
            o, q, v, cf, cb, tot_f, tot_b, klf, klb = res
            r = rows_of(ci)
            oall[r, :] = o + jnp.dot((q * jnp.exp2(cf)).astype(BF16), s.astype(BF16),
                                     preferred_element_type=F32)
            qb_all[r, :] = (q * jnp.exp2(cb)).astype(BF16)
            klb_all[r, :] = klb
            decb_all[pl.ds(pl.multiple_of(ci * F32_SUBLANES, F32_SUBLANES), 1), :] = jnp.exp2(tot_b)
            s = row_to_col(jnp.exp2(tot_f)) * s + jnp.dot(klf.T.astype(BF16), v.astype(BF16),
                                                          preferred_element_type=F32)
        return s

    def bwd(i, s):
        ci = n_chunks - 1 - i
        r = rows_of(ci)
        finish(r, oall[r, :] + jnp.dot(qb_all[r, :], s.astype(BF16), preferred_element_type=F32))
        dec = decb_all[pl.ds(pl.multiple_of(ci * F32_SUBLANES, F32_SUBLANES), 1), :]
        return row_to_col(dec) * s + jnp.dot(klb_all[r, :].T.astype(BF16), v_ref[r, :].astype(BF16),
                                             preferred_element_type=F32)

    lax.fori_loop(0, n_iters, fwd, s0_ref[0])
    lax.fori_loop(0, n_chunks, bwd, s0_ref[1], unroll=4)


def _hgrn_scan(p, lb, o_norm, n_batch, n_seq, state=None, final_states=None, slot=0, seqs_per_step=16):
    has_state = state is not None
    c = SCAN_CHUNK
    if has_state:
        rows, n_steps = n_seq, n_batch
    else:
        assert n_seq == c and n_batch % seqs_per_step == 0
        rows, n_steps = seqs_per_step * c, n_batch // seqs_per_step
    n_chunks = rows // c
    col = lambda s: (lambda b, h: (b, s * A_HEADS + h))
    in_specs = [pl.BlockSpec((rows, HEAD), col(s)) for s in range(5)]
    in_specs += [
        pl.BlockSpec((2, HEAD), lambda b, h: (0, h)),
        pl.BlockSpec((1, HEAD), lambda b, h: (0, 0)),
    ]
    args = [p] * 5 + [lb, o_norm.reshape(1, HEAD)]
    out_specs = [pl.BlockSpec((rows, HEAD), lambda b, h: (b, h))]
    out_shape = [jax.ShapeDtypeStruct((n_batch * n_seq, D_MODEL), BF16)]
    scratch = [pltpu.VMEM((2, c, c), BF16), pltpu.VMEM((SCAN_LEVELS, SCAN_HALF, SCAN_HALF), BF16)]
    assert n_chunks % SCAN_CHUNKS_PER_ITER == 0
    scratch += [pltpu.VMEM((c, HEAD), F32)] * (SCAN_CHUNKS_PER_ITER * SCAN_BUFS)
    if has_state:
        in_specs.append(pl.BlockSpec((None, 2, None, HEAD, HEAD), lambda b, h: (b, 0, h, 0, 0)))
        args.append(state)
        scratch += [pltpu.VMEM((rows, HEAD), F32), pltpu.VMEM((rows, HEAD), BF16),
                    pltpu.VMEM((rows, HEAD), F32), pltpu.VMEM((n_chunks * F32_SUBLANES, HEAD), F32)]
    else:
        in_specs.append(pl.BlockSpec(memory_space=pl.ANY))
        args.append(final_states)
        out_specs.append(pl.BlockSpec((n_chunks, None, 2, None, HEAD, HEAD), lambda b, h: (b, slot, 0, h, 0, 0)))
        out_shape.append(jax.ShapeDtypeStruct(final_states.shape, F32))
    res = pl.pallas_call(
        functools.partial(_hgrn_kernel, n_chunks=n_chunks, has_state=has_state),
        grid=(n_steps, A_HEADS),
        in_specs=in_specs,
        out_specs=out_specs,
        out_shape=out_shape,
        input_output_aliases={} if has_state else {len(args) - 1: 1},
        scratch_shapes=scratch,
        compiler_params=_cparams(2),
        name="hgrn2_scan",
    )(*args)
    return (res[0], None) if has_state else (res[0], res[1])


def _diff_lambda_init(layer):
    return 0.8 - 0.6 * math.exp(-0.3 * layer)


def kernel(x_prompt, x_sample, state_a, cache_b_k, cache_b_v, cache_c_k, cache_c_v, c, c_ctx, norm_w, mod_w, mod_b, a_w_in, a_w_out, a_o_norm, a_lower_bound, b_w_in, b_w_out, b_q_norm, b_k_norm, b_lambda, b_subln, c_w_in, c_w_out, c_q_norm, c_k_norm):
    n_ctx_b, n_ctx_s = x_prompt.shape[:2]
    n_lat_b, n_lat_s = x_sample.shape[:2]
    past = cache_b_k.shape[2]
    in_tm, in_tn = 1024, 1024
    out_tm, out_tn = 512, D_MODEL

    lb_all = jnp.cumsum(jax.nn.softmax(a_lower_bound.astype(F32), axis=0), axis=0)
    lb_all = lb_all - lb_all[0:1]

    cond = jnp.zeros((COND_ROWS, D_MODEL), F32).at[0].set(c_ctx).at[1:1 + n_lat_b].set(c)
    mods = _mod_rows(cond, mod_w, mod_b)
    rope_tab = _rope_tables(n_lat_s)

    groups = [
        dict(x=x_prompt.reshape(-1, D_MODEL), nb=n_ctx_b, ns=n_ctx_s, latent=False,
             row=lambda tm: (lambda i: 0)),
        dict(x=x_sample.reshape(-1, D_MODEL), nb=n_lat_b, ns=n_lat_s, latent=True,
             row=lambda tm: (lambda i: 1 + i // (n_lat_s // tm))),
    ]
    w_in_all = [w.astype(BF16) for w in (a_w_in, b_w_in, c_w_in)]
    w_out_all = [w.astype(BF16) for w in (a_w_out, b_w_out, c_w_out)]
    n_a_layers = a_w_in.shape[0]
    new_a = jnp.zeros((n_ctx_b, n_a_layers, 2, A_HEADS, HEAD, HEAD), F32)
    new_bk, new_bv, new_ck, new_cv = [], [], [], []

    for layer in range(DEPTH):
        kind, j = layer % N_MIXERS, layer // N_MIXERS
        for grp in groups:
            nb, ns, latent = grp["nb"], grp["ns"], grp["latent"]
            p = _in_proj(grp["x"], mods[layer], norm_w[layer], w_in_all[kind], j, grp["row"](in_tm), in_tm, in_tn)
            if kind == 0:
                if latent:
                    o, _ = _hgrn_scan(p, lb_all[layer], a_o_norm[j], nb, ns, state=state_a[:, j])
                else:
                    o, new_a = _hgrn_scan(p, lb_all[layer], a_o_norm[j], nb, ns, final_states=new_a, slot=j)
            elif kind == 1:
                tab = rope_tab if latent else None
                res = _kv_prep(p, D_MODEL, D_MODEL, 2 * D_MODEL, D_MODEL, b_k_norm[j], tab, nb, ns)
                ctx = None
                if latent:
                    ctx = (cache_b_k[:, j].reshape(nb, past, D_MODEL).astype(BF16),
                           jnp.swapaxes(cache_b_v[:, j].reshape(nb, past, D_MODEL), 1, 2).astype(BF16))
                else:
                    new_bk.append(res[2].reshape(nb, ns, B_HEADS, 2, HEAD))
                    new_bv.append(res[3].reshape(nb, ns, B_HEADS, 2 * HEAD))
                o = _diff_attention(p, res[0], res[1], b_q_norm[j], b_lambda[j], b_subln[j],
                                    _diff_lambda_init(layer), nb, ns, 512 if latent else 256,
                                    q_blocks=4 if latent else 1, heads=1 if latent else B_HEADS, rope_tab=tab, ctx=ctx)
            else:
                kvw = C_KV_HEADS * HEAD
                tab = rope_tab if latent else None
                res = _kv_prep(p, D_MODEL, kvw, D_MODEL + kvw, kvw, c_k_norm[j], tab, nb, ns)
                ctx = None
                if latent:
                    ctx = (cache_c_k[:, j].reshape(nb, past, kvw).astype(BF16),
                           jnp.swapaxes(cache_c_v[:, j].reshape(nb, past, kvw), 1, 2).astype(BF16))
                else:
                    new_ck.append(res[2].reshape(nb, ns, C_KV_HEADS, HEAD))
                    new_cv.append(res[3].reshape(nb, ns, C_KV_HEADS, HEAD))
                o = _gqa_attention(p, res[0], res[1], c_q_norm[j], nb, ns, 256, q_blocks=4 if latent else 1, heads=1 if latent else 2,
                                   rope_tab=tab, ctx=ctx)
            grp["x"] = _out_proj(o, w_out_all[kind], j, grp["x"], mods[layer], grp["row"](out_tm), out_tm, out_tn)

    y_prompt = groups[0]["x"].reshape(x_prompt.shape)
    y_sample = groups[1]["x"].reshape(x_sample.shape)
    return (y_prompt, y_sample, new_a, jnp.stack(new_bk, axis=1), jnp.stack(new_bv, axis=1),
            jnp.stack(new_ck, axis=1), jnp.stack(new_cv, axis=1))
```

```python
import functools
import math

import jax
import jax.numpy as jnp
from jax import lax
from jax.experimental import pallas as pl
from jax.experimental.pallas import tpu as pltpu

F32 = jnp.float32
BF16 = jnp.bfloat16

D_MODEL = 2048
DEPTH = 4
GRID_W = 64
N_MIXERS = 3
EPS = 1e-6
ROPE_THETA = 10000.0
HEAD = 128
A_HEADS = D_MODEL // HEAD
B_HEADS = D_MODEL // (2 * HEAD)
C_HEADS = D_MODEL // HEAD
C_KV_HEADS = C_HEADS // 4
C_GROUP = C_HEADS // C_KV_HEADS
F32_SUBLANES = 8
COND_ROWS = F32_SUBLANES

V7X_VMEM_LIMIT_BYTES = 56 * 1024 * 1024
SCAN_CHUNK = 256
SCAN_HALF = SCAN_CHUNK // 2
SCAN_LEVELS = SCAN_CHUNK.bit_length() - 1
SCAN_IN_BUFS = 6
SCAN_BUFS = SCAN_IN_BUFS + SCAN_LEVELS
SCAN_CHUNKS_PER_ITER = 4
SCAN_GATE_ROWS = 64
SCAN_SCORE_LOOKAHEAD = 8
LOG2_E = math.log2(math.e)
ATTN_KEY_CHUNK = 1024
ATTN_SCALE_LOG2 = (HEAD ** -0.5) * LOG2_E
BF16_SUBLANES = 16


def _cparams(n_axes):
    return pltpu.CompilerParams(
        dimension_semantics=("arbitrary",) * n_axes,
        vmem_limit_bytes=V7X_VMEM_LIMIT_BYTES,
    )


def _sigmoid(x):
    return 1.0 / (1.0 + jnp.exp(-x))


def _silu(x):
    return x * _sigmoid(x)


def _rms(x, w):
    return x * lax.rsqrt(jnp.mean(x * x, axis=-1, keepdims=True) + EPS) * w


def _mod_kernel(cond_ref, w_ref, b_ref, o_ref):
    s = _silu(cond_ref[...]).astype(BF16)
    o_ref[...] = jnp.dot(s, w_ref[...].astype(BF16), preferred_element_type=F32) + b_ref[...]


def _mod_rows(cond, mod_w, mod_b):
    d3 = 3 * D_MODEL
    tn = 768
    out = pl.pallas_call(
        _mod_kernel,
        grid=(DEPTH, d3 // tn),
        in_specs=[
            pl.BlockSpec((COND_ROWS, D_MODEL), lambda l, j: (0, 0)),
            pl.BlockSpec((None, D_MODEL, tn), lambda l, j: (l, 0, j)),
            pl.BlockSpec((None, 1, tn), lambda l, j: (l, 0, j)),
        ],
        out_specs=pl.BlockSpec((None, COND_ROWS, tn), lambda l, j: (l, 0, j)),
        out_shape=jax.ShapeDtypeStruct((DEPTH, COND_ROWS, d3), F32),
        compiler_params=_cparams(2),
        name="adaln_rows",
    )(cond, mod_w, mod_b.reshape(DEPTH, 1, d3))
    return out.reshape(DEPTH, COND_ROWS, 3, D_MODEL)


def _inproj_kernel(x_ref, mod_ref, nw_ref, w_ref, o_ref, h_ref):
    @pl.when(pl.program_id(1) == 0)
    def _():
        gain = nw_ref[...] * (1.0 + mod_ref[1:2, :])
        shift = mod_ref[0:1, :]

        def rows(ci, carry):
            r = pl.ds(pl.multiple_of(ci * BF16_SUBLANES, BF16_SUBLANES), BF16_SUBLANES)
            x = x_ref[r, :]
            inv = lax.rsqrt(jnp.mean(x * x, axis=-1, keepdims=True) + EPS)
            h_ref[r, :] = (x * inv * gain + shift).astype(BF16)
            return carry

        lax.fori_loop(0, x_ref.shape[0] // BF16_SUBLANES, rows, 0, unroll=4)

    o_ref[...] = jnp.dot(h_ref[...], w_ref[...], preferred_element_type=F32).astype(o_ref.dtype)


def _in_proj(x, mod, norm_w, w, layer, row_of_tile, tm, tn):
    t, n = x.shape[0], w.shape[2]
    return pl.pallas_call(
        _inproj_kernel,
        grid=(t // tm, n // tn),
        in_specs=[
            pl.BlockSpec((tm, D_MODEL), lambda i, j: (i, 0)),
            pl.BlockSpec((None, 3, D_MODEL), lambda i, j: (row_of_tile(i), 0, 0)),
            pl.BlockSpec((1, D_MODEL), lambda i, j: (0, 0)),
            pl.BlockSpec((None, D_MODEL, tn), lambda i, j: (layer, 0, j)),
        ],
        out_specs=pl.BlockSpec((tm, tn), lambda i, j: (i, j)),
        out_shape=jax.ShapeDtypeStruct((t, n), F32),
        scratch_shapes=[pltpu.VMEM((tm, D_MODEL), BF16)],
        compiler_params=_cparams(2),
        name="in_proj",
    )(x, mod, norm_w.reshape(1, D_MODEL), w)


def _outproj_kernel(o_ref, w_ref, x_ref, mod_ref, y_ref):
    acc = jnp.dot(o_ref[...], w_ref[...], preferred_element_type=F32)
    y_ref[...] = x_ref[...] + mod_ref[2:3, :] * acc


def _out_proj(o, w, layer, x, mod, row_of_tile, tm, tn):
    t, kdim = o.shape
    return pl.pallas_call(
        _outproj_kernel,
        grid=(t // tm, D_MODEL // tn),
        in_specs=[
            pl.BlockSpec((tm, kdim), lambda i, j: (i, 0)),
            pl.BlockSpec((None, kdim, tn), lambda i, j: (layer, 0, j)),
            pl.BlockSpec((tm, tn), lambda i, j: (i, j)),
            pl.BlockSpec((None, 3, tn), lambda i, j: (row_of_tile(i), 0, j)),
        ],
        out_specs=pl.BlockSpec((tm, tn), lambda i, j: (i, j)),
        out_shape=jax.ShapeDtypeStruct((t, D_MODEL), F32),
        compiler_params=_cparams(2),
        name="out_proj",
    )(o, w, x, mod)


def _rope_tables(n):
    pos = jnp.arange(n)
    quarter = HEAD // 4
    inv_freq = ROPE_THETA ** (-jnp.arange(quarter, dtype=F32) / quarter)

    def axis_angles(p):
        ang = p.astype(F32)[:, None] * inv_freq[None, :]
        return jnp.concatenate([ang, ang], axis=-1)

    ang = jnp.concatenate([axis_angles(pos // GRID_W), axis_angles(pos % GRID_W)], axis=-1)
    cos, sin = jnp.cos(ang), jnp.sin(ang)
    first = (jnp.arange(HEAD) % (2 * quarter)) < quarter
    return jnp.stack([cos, jnp.where(first, -sin, 0.0), jnp.where(first, 0.0, sin)])


def _rope(x, tab_ref, rows=slice(None)):
    up = pltpu.roll(x, HEAD - HEAD // 4, 1)
    down = pltpu.roll(x, HEAD // 4, 1)
    return x * tab_ref[0, rows, :] + up * tab_ref[1, rows, :] + down * tab_ref[2, rows, :]


def _kvprep_kernel(*refs, n_kheads, rope, emit_norm):
    k_ref, v_ref, kn_ref = refs[:3]
    pos = 3
    tab_ref = None
    if rope:
        tab_ref = refs[pos]
        pos += 1
    kh_ref, vt_ref = refs[pos], refs[pos + 1]
    kn_out, v_out = (refs[pos + 2], refs[pos + 3]) if emit_norm else (None, None)
    for h in range(n_kheads):
        sl = slice(h * HEAD, (h + 1) * HEAD)
        kn = _rms(k_ref[:, sl], kn_ref[...])
        if emit_norm:
            kn_out[:, sl] = kn
        if rope:
            kn = _rope(kn, tab_ref)
        kh_ref[:, sl] = kn.astype(BF16)
    v = v_ref[...]
    if emit_norm:
        v_out[...] = v
    vt_ref[...] = v.T.astype(BF16)


def _kv_prep(p, k_col, k_w, v_col, v_w, k_norm, rope_tab, n_batch, n_seq):
    t = p.shape[0]
    rope = rope_tab is not None
    emit_norm = not rope
    tr = min(512, n_seq)
    per_seq = n_seq // tr
    in_specs = [
        pl.BlockSpec((tr, k_w), lambda i: (i, k_col // k_w)),
        pl.BlockSpec((tr, v_w), lambda i: (i, v_col // v_w)),
        pl.BlockSpec((1, HEAD), lambda i: (0, 0)),
    ]
    args = [p, p, k_norm.reshape(1, HEAD)]
    if rope:
        in_specs.append(pl.BlockSpec((3, tr, HEAD), lambda i: (0, i % per_seq, 0)))
        args.append(rope_tab)
    out_specs = [
        pl.BlockSpec((tr, k_w), lambda i: (i, 0)),
        pl.BlockSpec((None, v_w, tr), lambda i: (i // per_seq, 0, i % per_seq)),
    ]
    out_shape = [jax.ShapeDtypeStruct((t, k_w), BF16), jax.ShapeDtypeStruct((n_batch, v_w, n_seq), BF16)]
    if emit_norm:
        out_specs += [pl.BlockSpec((tr, k_w), lambda i: (i, 0)), pl.BlockSpec((tr, v_w), lambda i: (i, 0))]
        out_shape += [jax.ShapeDtypeStruct((t, k_w), F32), jax.ShapeDtypeStruct((t, v_w), F32)]
    return pl.pallas_call(
        functools.partial(_kvprep_kernel, n_kheads=k_w // HEAD, rope=rope, emit_norm=emit_norm),
        grid=(t // tr,),
        in_specs=in_specs,
        out_specs=out_specs,
        out_shape=out_shape,
        compiler_params=_cparams(1),
        name="kv_prep",
    )(*args)


_NT = (((1,), (1,)), ((), ()))


def _key_chunks(k_ref, vt_ref, head, k_w, v_w):
    n = k_ref.shape[0]
    step = min(n, ATTN_KEY_CHUNK)
    ones = jnp.ones((BF16_SUBLANES, step), BF16)
    return [(k_ref[c0:c0 + step, head * k_w:(head + 1) * k_w],
             jnp.concatenate([vt_ref[head * v_w:(head + 1) * v_w, c0:c0 + step], ones], axis=0))
            for c0 in range(0, n, step)]


def _attend_chunks(problems, finish):
    pairs = [(i, j) for i, (_, chunks) in enumerate(problems) for j in range(len(chunks))]
    s, m, o = {}, {}, {}

    def score(t):
        i, j = pairs[t]
        q, chunks = problems[i]
        s[t] = lax.dot_general(chunks[j][0], q, _NT, preferred_element_type=F32)
        m[t] = s[t].max(axis=0, keepdims=True)

    for t in range(min(2, len(pairs))):
        score(t)
    for t, (i, j) in enumerate(pairs):
        e = jnp.exp2(s.pop(t) - m[t]).astype(BF16)
        if t + 2 < len(pairs):
            score(t + 2)
        chunks = problems[i][1]
        n = len(chunks)
        n_v = chunks[j][1].shape[0] - BF16_SUBLANES
        o[t] = jnp.dot(chunks[j][1], e, preferred_element_type=F32)
        if j == n - 1:
            ts = range(t - n + 1, t + 1)
            if n == 1:
                acc = o[t]
            else:
                m_all = functools.reduce(jnp.maximum, [m[u] for u in ts])
                acc = None
                for u in ts:
                    part = o[u] * jnp.exp2(m[u] - m_all)
                    acc = part if acc is None else acc + part
            finish(i, acc[:n_v], acc[n_v:n_v + 1])


def _diff_attn_kernel(*refs, has_ctx, lam_init, tq, heads):
    q_ref, g_ref, kh_ref, vt_ref, qn_ref, lamv_ref, sub_ref = refs[:7]
    w2 = 2 * HEAD
    pos = 7
    if has_ctx:
        tab_ref, ck_ref, cvt_ref = refs[pos:pos + 3]
        pos += 3
    o_ref = refs[pos]

    lv = lamv_ref[...]
    lam = (jnp.exp(jnp.sum(lv[0:1] * lv[1:2], keepdims=True))
           - jnp.exp(jnp.sum(lv[2:3] * lv[3:4], keepdims=True)) + lam_init)

    zero = jnp.zeros((tq, HEAD), BF16)
    n_blk = q_ref.shape[0] // tq
    problems = []
    for h in range(heads):
        chunks = _key_chunks(kh_ref, vt_ref, h, w2, w2) + (_key_chunks(ck_ref, cvt_ref, h, w2, w2) if has_ctx else [])
        for i in range(n_blk):
            r = slice(i * tq, (i + 1) * tq)
            rows = []
            for comp in range(2):
                q = _rms(q_ref[r, h * w2 + comp * HEAD:h * w2 + (comp + 1) * HEAD], qn_ref[...])
                if has_ctx:
                    q = _rope(q, tab_ref, r)
                q = (q * ATTN_SCALE_LOG2).astype(BF16)
                rows.append(jnp.concatenate([q, zero] if comp == 0 else [zero, q], axis=1))
            problems.append((jnp.concatenate(rows, axis=0), chunks))

    def finish(idx, acc, l):
        h, i = divmod(idx, n_blk)
        r = slice(i * tq, (i + 1) * tq)
        cols = slice(h * w2, (h + 1) * w2)
        o_t = acc[:, :tq] * (1.0 / l[:, :tq]) - acc[:, tq:] * (lam / l[:, tq:])
        o = _rms(o_t.T, sub_ref[...]) * (1.0 - lam_init)
        o_ref[r, cols] = (o * _silu(g_ref[r, cols])).astype(BF16)

    _attend_chunks(problems, finish)


def _diff_attention(p, kh, vt, q_norm, lam_vecs, subln, lam_init, n_batch, n_seq, tq, q_blocks=1, heads=1,
                    rope_tab=None, ctx=None):
    has_ctx = ctx is not None
    w2 = 2 * HEAD
    wc = heads * w2
    rows = tq * q_blocks
    nq = n_seq // rows
    assert (3 * D_MODEL) % wc == 0 and B_HEADS % heads == 0
    g_col0 = 3 * D_MODEL // wc
    in_specs = [
        pl.BlockSpec((rows, wc), lambda b, h, i: (b * nq + i, h)),
        pl.BlockSpec((rows, wc), lambda b, h, i: (b * nq + i, g_col0 + h)),
        pl.BlockSpec((n_seq, wc), lambda b, h, i: (b, h)),
        pl.BlockSpec((None, wc, n_seq), lambda b, h, i: (b, h, 0)),
        pl.BlockSpec((1, HEAD), lambda b, h, i: (0, 0)),
        pl.BlockSpec((4, HEAD), lambda b, h, i: (0, 0)),
        pl.BlockSpec((1, w2), lambda b, h, i: (0, 0)),
    ]
    args = [p, p, kh, vt, q_norm.reshape(1, HEAD), lam_vecs, subln.reshape(1, w2)]
    if has_ctx:
        ck, cvt = ctx
        n_ctx = ck.shape[1]
        in_specs += [
            pl.BlockSpec((3, rows, HEAD), lambda b, h, i: (0, i, 0)),
            pl.BlockSpec((None, n_ctx, wc), lambda b, h, i: (b, 0, h)),
            pl.BlockSpec((None, wc, n_ctx), lambda b, h, i: (b, h, 0)),
        ]
        args += [rope_tab, ck, cvt]
    return pl.pallas_call(
        functools.partial(_diff_attn_kernel, has_ctx=has_ctx, lam_init=lam_init, tq=tq, heads=heads),
        grid=(n_batch, B_HEADS // heads, nq),
        in_specs=in_specs,
        out_specs=pl.BlockSpec((rows, wc), lambda b, h, i: (b * nq + i, h)),
        out_shape=jax.ShapeDtypeStruct((n_batch * n_seq, D_MODEL), BF16),
        compiler_params=_cparams(3),
        name="diff_attention",
    )(*args)


def _gqa_kernel(*refs, has_ctx, tq, heads):
    q_ref, g_ref, kh_ref, vt_ref, qn_ref = refs[:5]
    wq = C_GROUP * HEAD
    pos = 5
    if has_ctx:
        tab_ref, ck_ref, cvt_ref = refs[pos:pos + 3]
        pos += 3
    o_ref = refs[pos]

    n_blk = q_ref.shape[0] // tq
    problems = []
    for h in range(heads):
        chunks = (_key_chunks(kh_ref, vt_ref, h, HEAD, HEAD)
                  + (_key_chunks(ck_ref, cvt_ref, h, HEAD, HEAD) if has_ctx else []))
        for i in range(n_blk):
            r = slice(i * tq, (i + 1) * tq)
            qs = []
            for gi in range(C_GROUP):
                c0 = h * wq + gi * HEAD
                q = _rms(q_ref[r, c0:c0 + HEAD], qn_ref[...])
                if has_ctx:
                    q = _rope(q, tab_ref, r)
                qs.append((q * ATTN_SCALE_LOG2).astype(BF16))
            problems.append((jnp.concatenate(qs, axis=0), chunks))

    def finish(idx, acc, l):
        h, i = divmod(idx, n_blk)
        r = slice(i * tq, (i + 1) * tq)
        o = (acc * (1.0 / l)).T
        for gi in range(C_GROUP):
            sl = slice(h * wq + gi * HEAD, h * wq + (gi + 1) * HEAD)
            o_ref[r, sl] = (o[gi * tq:(gi + 1) * tq] * _silu(g_ref[r, sl])).astype(BF16)

    _attend_chunks(problems, finish)


def _gqa_attention(p, kh, vt, q_norm, n_batch, n_seq, tq, q_blocks=1, heads=1, rope_tab=None, ctx=None):
    has_ctx = ctx is not None
    wq = heads * C_GROUP * HEAD
    wk = heads * HEAD
    rows = tq * q_blocks
    nq = n_seq // rows
    g_off = D_MODEL + 2 * C_KV_HEADS * HEAD
    assert g_off % wq == 0 and C_KV_HEADS % heads == 0
    g_col0 = g_off // wq
    in_specs = [
        pl.BlockSpec((rows, wq), lambda b, h, i: (b * nq + i, h)),
        pl.BlockSpec((rows, wq), lambda b, h, i: (b * nq + i, g_col0 + h)),
        pl.BlockSpec((n_seq, wk), lambda b, h, i: (b, h)),
        pl.BlockSpec((None, wk, n_seq), lambda b, h, i: (b, h, 0)),
        pl.BlockSpec((1, HEAD), lambda b, h, i: (0, 0)),
    ]
    args = [p, p, kh, vt, q_norm.reshape(1, HEAD)]
    if has_ctx:
        ck, cvt = ctx
        n_ctx = ck.shape[1]
        in_specs += [
            pl.BlockSpec((3, rows, HEAD), lambda b, h, i: (0, i, 0)),
            pl.BlockSpec((None, n_ctx, wk), lambda b, h, i: (b, 0, h)),
            pl.BlockSpec((None, wk, n_ctx), lambda b, h, i: (b, h, 0)),
        ]
        args += [rope_tab, ck, cvt]
    return pl.pallas_call(
        functools.partial(_gqa_kernel, has_ctx=has_ctx, tq=tq, heads=heads),
        grid=(n_batch, C_KV_HEADS // heads, nq),
        in_specs=in_specs,
        out_specs=pl.BlockSpec((rows, wq), lambda b, h, i: (b * nq + i, h)),
        out_shape=jax.ShapeDtypeStruct((n_batch * n_seq, D_MODEL), BF16),
        compiler_params=_cparams(3),
        name="gqa_attention",
    )(*args)


def _forget_gate(z, lb):
    t = jnp.exp2(jnp.abs(z) * -LOG2_E)
    u = 1.0 + t
    r = 1.0 / u
    tr = t * r
    pos = z >= 0.0
    sig = jnp.where(pos, r, tr)
    nsig = jnp.where(pos, tr, r)
    log2_sig = jnp.minimum(z, 0.0) * LOG2_E - jnp.log2(u)
    one_m = 1.0 - lb
    log2_f = jnp.where(lb > 0.0, jnp.log2(lb + one_m * sig), log2_sig)
    return log2_f, one_m * nsig


def _cumsum_rows(tri, x):
    hi = x.astype(BF16)
    mid = (x - hi.astype(F32)).astype(BF16)
    r = jnp.dot(tri, jnp.concatenate([hi, mid], axis=1), preferred_element_type=F32)
    return r[:, :HEAD] + r[:, HEAD:]


def _pair_rows(ref, b, odd):
    off = b if odd else 0
    if b >= 8:
        parts = [ref[pl.ds(i * 2 * b + off, b), :] for i in range(SCAN_HALF // b)]
    else:
        parts = [ref[pl.ds(off + r, SCAN_HALF // b, stride=2 * b), :] for r in range(b)]
    return parts[0] if len(parts) == 1 else jnp.concatenate(parts, axis=0)


def _pair_bcast(ref, b, row):
    if b >= 8:
        parts = [jnp.broadcast_to(ref[pl.ds(i * 2 * b + row, 1), :], (b, HEAD)) for i in range(SCAN_HALF // b)]
    else:
        parts = [ref[pl.ds(row, SCAN_HALF // b, stride=2 * b), :]] * b
    return parts[0] if len(parts) == 1 else jnp.concatenate(parts, axis=0)


def _pair_store(ref, b, odd, val):
    off = b if odd else 0
    if b >= 8:
        for i in range(SCAN_HALF // b):
            ref[pl.ds(i * 2 * b + off, b), :] = val[i * b:(i + 1) * b]
    else:
        n = SCAN_HALF // b
        for r in range(b):
            ref[pl.ds(off + r, n, stride=2 * b), :] = val[r * n:(r + 1) * n]


def _hgrn_kernel(*refs, n_chunks, has_state):
    q_ref, zf_ref, zb_ref, v_ref, g_ref, lb_ref, on_ref = refs[:7]
    s0_ref = refs[7] if has_state else None
    pos = 8
    o_ref = refs[pos]
    pos += 1
    sout_ref = None
    if not has_state:
        sout_ref = refs[pos]
        pos += 1
    tri_ref, mask_ref = refs[pos:pos + 2]
    pos += 2
    buf_sets = [refs[pos + k * SCAN_BUFS:pos + (k + 1) * SCAN_BUFS] for k in range(SCAN_CHUNKS_PER_ITER)]
    pos += SCAN_CHUNKS_PER_ITER * SCAN_BUFS
    if has_state:
        oall, qb_all, klb_all, decb_all = refs[pos:pos + 4]

    c = SCAN_CHUNK

    @pl.when((pl.program_id(0) == 0) & (pl.program_id(1) == 0))
    def _():
        row = lax.broadcasted_iota(jnp.int32, (c, c), 0)
        col = lax.broadcasted_iota(jnp.int32, (c, c), 1)
        tri_ref[0] = (col <= row).astype(BF16)
        tri_ref[1] = (col >= row).astype(BF16)
        prow = lax.broadcasted_iota(jnp.int32, (SCAN_HALF, SCAN_HALF), 0)
        pcol = lax.broadcasted_iota(jnp.int32, (SCAN_HALF, SCAN_HALF), 1)
        for j in range(SCAN_LEVELS):
            b = 1 << j
            if b >= 8:
                same = (prow // b) == (pcol // b)
            else:
                same = (prow % (SCAN_HALF // b)) == (pcol % (SCAN_HALF // b))
            mask_ref[j] = same.astype(F32).astype(BF16)

    lb_f = lb_ref[0:1, :]
    lb_b = lb_ref[1:2, :]

    def gates(ci, bufs):
        sq, sv, skf, scf, skb, scb = bufs[:SCAN_IN_BUFS]
        for r0 in range(0, c, SCAN_GATE_ROWS):
            src = pl.ds(pl.multiple_of(ci * c + r0, SCAN_GATE_ROWS), SCAN_GATE_ROWS)
            dst = slice(r0, r0 + SCAN_GATE_ROWS)
            sq[dst, :] = _silu(q_ref[src, :])
            sv[dst, :] = v_ref[src, :]
            scf[dst, :], skf[dst, :] = _forget_gate(zf_ref[src, :], lb_f)
            scb[dst, :], skb[dst, :] = _forget_gate(zb_ref[src, :], lb_b)
        scf[...] = _cumsum_rows(tri_ref[0], scf[...])
        scb[...] = _cumsum_rows(tri_ref[1], scb[...])

    def pair_scores(bufs, j, fwd):
        sq, sv, skf, scf, skb, scb = bufs[:SCAN_IN_BUFS]
        b = 1 << j
        if fwd:
            edge = _pair_bcast(scf, b, b - 1)
            qt = _pair_rows(sq, b, True) * jnp.exp2(_pair_rows(scf, b, True) - edge)
            kt = _pair_rows(skf, b, False) * jnp.exp2(edge - _pair_rows(scf, b, False))
        else:
            edge = _pair_bcast(scb, b, b)
            qt = _pair_rows(sq, b, False) * jnp.exp2(_pair_rows(scb, b, False) - edge)
            kt = _pair_rows(skb, b, True) * jnp.exp2(edge - _pair_rows(scb, b, True))
        z = lax.dot_general(qt.astype(BF16), kt.astype(BF16), _NT, preferred_element_type=F32).astype(BF16)
        return z if j == SCAN_LEVELS - 1 else z * mask_ref[j]

    def pair_values(bufs, j, fwd, z):
        sv, lvl_out = bufs[1], bufs[SCAN_IN_BUFS:]
        b = 1 << j
        _pair_store(lvl_out[j], b, fwd, jnp.dot(z, _pair_rows(sv, b, not fwd).astype(BF16),
                                                preferred_element_type=F32))

    def chunk_result(bufs):
        q, v, kf, cf, kb, cb = (buf[...] for buf in bufs[:SCAN_IN_BUFS])
        o = jnp.sum(q * (kf + kb), axis=-1, keepdims=True) * v
        for lvl in bufs[SCAN_IN_BUFS:]:
            o = o + lvl[...]
        tot_f = cf[c - 1:c, :]
        tot_b = cb[0:1, :]
        klf = kf * jnp.exp2(tot_f - cf)
        klb = kb * jnp.exp2(tot_b - cb)
        return o, q, v, cf, cb, tot_f, tot_b, klf, klb

    def intra(cis):
        for ci, bufs in zip(cis, buf_sets):
            gates(ci, bufs)
        items = [(bufs, j, fwd) for bufs in buf_sets for j in range(SCAN_LEVELS) for fwd in (True, False)]
        scores = {}
        for t in range(len(items) + SCAN_SCORE_LOOKAHEAD):
            if t < len(items):
                scores[t] = pair_scores(*items[t])
            if t >= SCAN_SCORE_LOOKAHEAD:
                u = t - SCAN_SCORE_LOOKAHEAD
                pair_values(*items[u], scores.pop(u))
        return [chunk_result(bufs) for bufs in buf_sets]

    def finish(r, o):
        o = _rms(o, on_ref[...]) * _silu(g_ref[r, :])
        o_ref[r, :] = o.astype(BF16)

    def rows_of(ci):
        return pl.ds(pl.multiple_of(ci * c, c), c)

    per_iter = SCAN_CHUNKS_PER_ITER
    n_iters = n_chunks // per_iter

    if not has_state:
        def seqs(it, carry):
            cis = [it * per_iter + k for k in range(per_iter)]
            for ci, res in zip(cis, intra(cis)):
                o, q, v, cf, cb, tot_f, tot_b, klf, klb = res
                v_b = v.astype(BF16)
                sout_ref[ci, 0] = jnp.dot(klf.T.astype(BF16), v_b, preferred_element_type=F32)
                sout_ref[ci, 1] = jnp.dot(klb.T.astype(BF16), v_b, preferred_element_type=F32)
                finish(rows_of(ci), o)
            return carry

        lax.fori_loop(0, n_iters, seqs, 0)
        return

    def row_to_col(x):
        return jnp.broadcast_to(x, (HEAD, HEAD)).T

    def fwd(it, s):
        cis = [it * per_iter + k for k in range(per_iter)]
        for ci, res in zip(cis, intra(cis)):
            o, q, v, cf, cb, tot_f, tot_b, klf, klb = res
            r = rows_of(ci)
            oall[r, :] = o + jnp.dot((q * jnp.exp2(cf)).astype(BF16), s.astype(BF16),
                                     preferred_element_type=F32)
            qb_all[r, :] = (q * jnp.exp2(cb)).astype(BF16)
            klb_all[r, :] = klb
            decb_all[pl.ds(pl.multiple_of(ci * F32_SUBLANES, F32_SUBLANES), 1), :] = jnp.exp2(tot_b)
            s = row_to_col(jnp.exp2(tot_f)) * s + jnp.dot(klf.T.astype(BF16), v.astype(BF16),
                                                          preferred_element_type=F32)
        return s

    def bwd(i, s):
        ci = n_chunks - 1 - i
        r = rows_of(ci)
        finish(r, oall[r, :] + jnp.dot(qb_all[r, :], s.astype(BF16), preferred_element_type=F32))
        dec = decb_all[pl.ds(pl.multiple_of(ci * F32_SUBLANES, F32_SUBLANES), 1), :]
        return row_to_col(dec) * s + jnp.dot(klb_all[r, :].T.astype(BF16), v_ref[r, :].astype(BF16),
                                             preferred_element_type=F32)

    lax.fori_loop(0, n_iters, fwd, s0_ref[0])
    lax.fori_loop(0, n_chunks, bwd, s0_ref[1], unroll=8)


def _hgrn_scan(p, lb, o_norm, n_batch, n_seq, state=None, final_states=None, slot=0, seqs_per_step=16):
    has_state = state is not None
    c = SCAN_CHUNK
    if has_state:
        rows, n_steps = n_seq, n_batch
    else:
        assert n_seq == c and n_batch % seqs_per_step == 0
        rows, n_steps = seqs_per_step * c, n_batch // seqs_per_step
    n_chunks = rows // c
    col = lambda s: (lambda b, h: (b, s * A_HEADS + h))
    in_specs = [pl.BlockSpec((rows, HEAD), col(s)) for s in range(5)]
    in_specs += [
        pl.BlockSpec((2, HEAD), lambda b, h: (0, h)),
        pl.BlockSpec((1, HEAD), lambda b, h: (0, 0)),
    ]
    args = [p] * 5 + [lb, o_norm.reshape(1, HEAD)]
    out_specs = [pl.BlockSpec((rows, HEAD), lambda b, h: (b, h))]
    out_shape = [jax.ShapeDtypeStruct((n_batch * n_seq, D_MODEL), BF16)]
    scratch = [pltpu.VMEM((2, c, c), BF16), pltpu.VMEM((SCAN_LEVELS, SCAN_HALF, SCAN_HALF), BF16)]
    assert n_chunks % SCAN_CHUNKS_PER_ITER == 0
    scratch += [pltpu.VMEM((c, HEAD), F32)] * (SCAN_CHUNKS_PER_ITER * SCAN_BUFS)
    if has_state:
        in_specs.append(pl.BlockSpec((None, 2, None, HEAD, HEAD), lambda b, h: (b, 0, h, 0, 0)))
        args.append(state)
        scratch += [pltpu.VMEM((rows, HEAD), F32), pltpu.VMEM((rows, HEAD), BF16),
                    pltpu.VMEM((rows, HEAD), F32), pltpu.VMEM((n_chunks * F32_SUBLANES, HEAD), F32)]
    else:
        in_specs.append(pl.BlockSpec(memory_space=pl.ANY))
        args.append(final_states)
        out_specs.append(pl.BlockSpec((n_chunks, None, 2, None, HEAD, HEAD), lambda b, h: (b, slot, 0, h, 0, 0)))
        out_shape.append(jax.ShapeDtypeStruct(final_states.shape, F32))
    res = pl.pallas_call(
        functools.partial(_hgrn_kernel, n_chunks=n_chunks, has_state=has_state),
        grid=(n_steps, A_HEADS),
        in_specs=in_specs,
        out_specs=out_specs,
        out_shape=out_shape,
        input_output_aliases={} if has_state else {len(args) - 1: 1},
        scratch_shapes=scratch,
        compiler_params=_cparams(2),
        name="hgrn2_scan",
    )(*args)
    return (res[0], None) if has_state else (res[0], res[1])


def _diff_lambda_init(layer):
    return 0.8 - 0.6 * math.exp(-0.3 * layer)


def kernel(x_prompt, x_sample, state_a, cache_b_k, cache_b_v, cache_c_k, cache_c_v, c, c_ctx, norm_w, mod_w, mod_b, a_w_in, a_w_out, a_o_norm, a_lower_bound, b_w_in, b_w_out, b_q_norm, b_k_norm, b_lambda, b_subln, c_w_in, c_w_out, c_q_norm, c_k_norm):
    n_ctx_b, n_ctx_s = x_prompt.shape[:2]
    n_lat_b, n_lat_s = x_sample.shape[:2]
    past = cache_b_k.shape[2]
    in_tm, in_tn = 1024, 1024
    out_tm, out_tn = 512, D_MODEL

    lb_all = jnp.cumsum(jax.nn.softmax(a_lower_bound.astype(F32), axis=0), axis=0)
    lb_all = lb_all - lb_all[0:1]

    cond = jnp.zeros((COND_ROWS, D_MODEL), F32).at[0].set(c_ctx).at[1:1 + n_lat_b].set(c)
    mods = _mod_rows(cond, mod_w, mod_b)
    rope_tab = _rope_tables(n_lat_s)

    groups = [
        dict(x=x_prompt.reshape(-1, D_MODEL), nb=n_ctx_b, ns=n_ctx_s, latent=False,
             row=lambda tm: (lambda i: 0)),
        dict(x=x_sample.reshape(-1, D_MODEL), nb=n_lat_b, ns=n_lat_s, latent=True,
             row=lambda tm: (lambda i: 1 + i // (n_lat_s // tm))),
    ]
    w_in_all = [w.astype(BF16) for w in (a_w_in, b_w_in, c_w_in)]
    w_out_all = [w.astype(BF16) for w in (a_w_out, b_w_out, c_w_out)]
    n_a_layers = a_w_in.shape[0]
    new_a = jnp.zeros((n_ctx_b, n_a_layers, 2, A_HEADS, HEAD, HEAD), F32)
    new_bk, new_bv, new_ck, new_cv = [], [], [], []

    for layer in range(DEPTH):
        kind, j = layer % N_MIXERS, layer // N_MIXERS
        for grp in groups:
            nb, ns, latent = grp["nb"], grp["ns"], grp["latent"]
            p = _in_proj(grp["x"], mods[layer], norm_w[layer], w_in_all[kind], j, grp["row"](in_tm), in_tm, in_tn)
            if kind == 0:
                if latent:
                    o, _ = _hgrn_scan(p, lb_all[layer], a_o_norm[j], nb, ns, state=state_a[:, j])
                else:
                    o, new_a = _hgrn_scan(p, lb_all[layer], a_o_norm[j], nb, ns, final_states=new_a, slot=j)
            elif kind == 1:
                tab = rope_tab if latent else None
                res = _kv_prep(p, D_MODEL, D_MODEL, 2 * D_MODEL, D_MODEL, b_k_norm[j], tab, nb, ns)
                ctx = None
                if latent:
                    ctx = (cache_b_k[:, j].reshape(nb, past, D_MODEL).astype(BF16),
                           jnp.swapaxes(cache_b_v[:, j].reshape(nb, past, D_MODEL), 1, 2).astype(BF16))
                else:
                    new_bk.append(res[2].reshape(nb, ns, B_HEADS, 2, HEAD))
                    new_bv.append(res[3].reshape(nb, ns, B_HEADS, 2 * HEAD))
                o = _diff_attention(p, res[0], res[1], b_q_norm[j], b_lambda[j], b_subln[j],
                                    _diff_lambda_init(layer), nb, ns, 512 if latent else 256,
                                    q_blocks=4 if latent else 1, heads=1 if latent else B_HEADS, rope_tab=tab, ctx=ctx)
            else:
                kvw = C_KV_HEADS * HEAD
                tab = rope_tab if latent else None
                res = _kv_prep(p, D_MODEL, kvw, D_MODEL + kvw, kvw, c_k_norm[j], tab, nb, ns)
                ctx = None
                if latent:
                    ctx = (cache_c_k[:, j].reshape(nb, past, kvw).astype(BF16),
                           jnp.swapaxes(cache_c_v[:, j].reshape(nb, past, kvw), 1, 2).astype(BF16))
                else:
                    new_ck.append(res[2].reshape(nb, ns, C_KV_HEADS, HEAD))
                    new_cv.append(res[3].reshape(nb, ns, C_KV_HEADS, HEAD))
                o = _gqa_attention(p, res[0], res[1], c_q_norm[j], nb, ns, 256, q_blocks=4 if latent else 1, heads=1 if latent else 2,
                                   rope_tab=tab, ctx=ctx)
            grp["x"] = _out_proj(o, w_out_all[kind], j, grp["x"], mods[layer], grp["row"](out_tm), out_tm, out_tn)

    y_prompt = groups[0]["x"].reshape(x_prompt.shape)
    y_sample = groups[1]["x"].reshape(x_sample.shape)
    return (y_prompt, y_sample, new_a, jnp.stack(new_bk, axis=1), jnp.stack(new_bv, axis=1),
            jnp.stack(new_ck, axis=1), jnp.stack(new_cv, axis=1))
```
